```python
import math
import jax, jax.numpy as jnp
from jax import lax
import numpy as np

D_MODEL = 2048
BATCH = 4
SEQ = 4096
DEPTH = 4

D_FF = 4 * D_MODEL
NORM_EPS = 1e-6

MLA_HEADS = 8
MLA_Q_RANK = 512
MLA_KV_RANK = 512
MLA_NOPE_DIM = 128
MLA_ROPE_DIM = 64
MLA_V_DIM = 128
ROPE_THETA = 10000.0
Q_BLOCK = 128

MOBA_HEADS = 8
MOBA_HEAD_DIM = 128
MOBA_BLOCK = 256
MOBA_TOPK = 3
MOBA_Q_CHUNK = 16

DIL_WINDOWS = (128, 512, 2048)
DIL_RATES = (1, 4, 16)
N_DIL = 3
DIL_HEADS_PER_GROUP = 4
DIL_HEAD_DIM = 128

SWA_Q_HEADS = 16
SWA_KV_HEADS = 2
SWA_HEAD_DIM = 64
SWA_WINDOW = 128

BAND = 128

EVEN_IN = MLA_Q_RANK + MLA_KV_RANK + MLA_ROPE_DIM + 3 * MOBA_HEADS * MOBA_HEAD_DIM
EVEN_OUT = MLA_HEADS * MLA_V_DIM + MOBA_HEADS * MOBA_HEAD_DIM
ODD_IN = 3 * N_DIL * DIL_HEADS_PER_GROUP * DIL_HEAD_DIM + (SWA_Q_HEADS + 2 * SWA_KV_HEADS) * SWA_HEAD_DIM
ODD_OUT = DIL_HEADS_PER_GROUP * DIL_HEAD_DIM + SWA_Q_HEADS * SWA_HEAD_DIM

kernel_name = "hybrid_mla_moba_dilated_swa_trunk"


def rms_norm(x, g):
    xf = x.astype(jnp.float32)
    y = xf * lax.rsqrt(jnp.mean(xf * xf, axis=-1, keepdims=True) + NORM_EPS)
    return (y * g.astype(jnp.float32)).astype(x.dtype)


def alibi_slopes(n):
    return jnp.exp2(-8.0 * jnp.arange(1, n + 1, dtype=jnp.float32) / n)


def apply_rope(x, pos):
    half = x.shape[-1] // 2
    inv_freq = ROPE_THETA ** (-jnp.arange(half, dtype=jnp.float32) / half)
    ang = pos.astype(jnp.float32)[:, None] * inv_freq[None, :]
    shape = (1, x.shape[1]) + (1,) * (x.ndim - 3) + (half,)
    cos = jnp.cos(ang).reshape(shape)
    sin = jnp.sin(ang).reshape(shape)
    xf = x.astype(jnp.float32)
    x1, x2 = xf[..., :half], xf[..., half:]
    return jnp.concatenate([x1 * cos - x2 * sin, x2 * cos + x1 * sin], axis=-1).astype(x.dtype)


def mla_attention(q_lat, kv_lat, k_pe, q_norm, w_uq, kv_norm, w_ukv):
    B, S, _ = q_lat.shape
    H, dn, dr, dv = MLA_HEADS, MLA_NOPE_DIM, MLA_ROPE_DIM, MLA_V_DIM
    pos = jnp.arange(S, dtype=jnp.int32)
    q = (rms_norm(q_lat, q_norm) @ w_uq).reshape(B, S, H, dn + dr)
    kv = (rms_norm(kv_lat, kv_norm) @ w_ukv).reshape(B, S, H, dn + dv)
    q = jnp.concatenate([q[..., :dn], apply_rope(q[..., dn:], pos)], axis=-1)
    k_rot = jnp.broadcast_to(apply_rope(k_pe, pos)[:, :, None, :], (B, S, H, dr))
    k = jnp.concatenate([kv[..., :dn], k_rot], axis=-1)
    v = kv[..., dn:]
    scale = (dn + dr) ** -0.5

    def query_block(c):
        q0 = c * Q_BLOCK
        qc = lax.dynamic_slice_in_dim(q, q0, Q_BLOCK, axis=1)
        s = jnp.einsum("bqhd,bkhd->bhqk", qc, k, preferred_element_type=jnp.float32) * scale
        qpos = q0 + jnp.arange(Q_BLOCK, dtype=jnp.int32)
        s = jnp.where(pos[None, :] <= qpos[:, None], s, -jnp.inf)
        p = jax.nn.softmax(s, axis=-1)
        return jnp.einsum("bhqk,bkhd->bqhd", p.astype(v.dtype), v, preferred_element_type=jnp.float32)

    o = lax.map(query_block, jnp.arange(S // Q_BLOCK))
    return o.transpose(1, 0, 2, 3, 4).reshape(B, S, H * dv).astype(q_lat.dtype)


def moba_attention(q, k, v):
    B, S, H, dh = q.shape
    nb = -(-S // MOBA_BLOCK)
    pad = nb * MOBA_BLOCK - S
    kp = jnp.pad(k, ((0, 0), (0, pad), (0, 0), (0, 0)))
    vp = jnp.pad(v, ((0, 0), (0, pad), (0, 0), (0, 0)))
    kb = kp.reshape(B, nb, MOBA_BLOCK, H, dh).transpose(0, 3, 1, 2, 4)
    vb = vp.reshape(B, nb, MOBA_BLOCK, H, dh).transpose(0, 3, 1, 2, 4)
    kmean = jnp.mean(kb.astype(jnp.float32), axis=3).astype(q.dtype)
    k_sel = min(MOBA_TOPK, nb)
    slopes = alibi_slopes(H)
    scale = dh ** -0.5
    blk_ids = jnp.arange(nb, dtype=jnp.int32)
    offs = jnp.arange(MOBA_BLOCK, dtype=jnp.int32)
    b_ix = jnp.arange(B)[:, None, None, None]
    h_ix = jnp.arange(H)[None, :, None, None]

    def query_chunk(c):
        t0 = c * MOBA_Q_CHUNK
        qc = lax.dynamic_slice_in_dim(q, t0, MOBA_Q_CHUNK, axis=1).transpose(0, 2, 1, 3)
        tpos = t0 + jnp.arange(MOBA_Q_CHUNK, dtype=jnp.int32)
        own = t0 // MOBA_BLOCK
        gate = jnp.einsum("bhqd,bhnd->bhqn", qc, kmean, preferred_element_type=jnp.float32)
        gate = jnp.where(blk_ids < own, gate, -jnp.inf)
        gval, gidx = lax.top_k(gate, k_sel)
        own_idx = jnp.broadcast_to(own, gidx.shape[:-1] + (1,)).astype(gidx.dtype)
        idx = jnp.concatenate([gidx, own_idx], axis=-1)
        ok = jnp.concatenate([gval > -jnp.inf, jnp.ones(own_idx.shape, bool)], axis=-1)
        kg = kb[b_ix, h_ix, idx]
        vg = vb[b_ix, h_ix, idx]
        s = jnp.einsum("bhqd,bhqnjd->bhqnj", qc, kg, preferred_element_type=jnp.float32) * scale
        kpos = idx[..., None] * MOBA_BLOCK + offs
        dist = tpos[None, None, :, None, None] - kpos
        mask = ok[..., None] & (dist >= 0)
        s = jnp.where(mask, s - slopes[None, :, None, None, None] * dist.astype(jnp.float32), -jnp.inf)
        p = jax.nn.softmax(s.reshape(B, H, MOBA_Q_CHUNK, -1), axis=-1).reshape(s.shape)
        return jnp.einsum("bhqnj,bhqnjd->bqhd", p.astype(vg.dtype), vg, preferred_element_type=jnp.float32)

    o = lax.map(query_chunk, jnp.arange(S // MOBA_Q_CHUNK))
    return o.transpose(1, 0, 2, 3, 4).reshape(B, S, H * dh).astype(q.dtype)


def banded_window_attention(q, k, v, slopes, max_dist, pos_scale):
    N, L, Hk, G, dh = q.shape
    nb = -(-L // BAND)
    Lp = nb * BAND
    qp = jnp.pad(q, ((0, 0), (0, Lp - L), (0, 0), (0, 0), (0, 0))).reshape(N, nb, BAND, Hk, G, dh)
    kp = jnp.pad(k, ((0, 0), (BAND, Lp - L), (0, 0), (0, 0))).reshape(N, nb + 1, BAND, Hk, dh)
    vp = jnp.pad(v, ((0, 0), (BAND, Lp - L), (0, 0), (0, 0))).reshape(N, nb + 1, BAND, Hk, dh)
    kw = jnp.concatenate([kp[:, :-1], kp[:, 1:]], axis=2)
    vw = jnp.concatenate([vp[:, :-1], vp[:, 1:]], axis=2)
    s = jnp.einsum("nbqhgd,nbkhd->nbhgqk", qp, kw, preferred_element_type=jnp.float32) * (dh ** -0.5)
    blk = jnp.arange(nb, dtype=jnp.int32)[:, None, None]
    qpos = blk * BAND + jnp.arange(BAND, dtype=jnp.int32)[None, :, None]
    kpos = blk * BAND - BAND + jnp.arange(2 * BAND, dtype=jnp.int32)[None, None, :]
    dist = qpos - kpos
    mask = (dist >= 0) & (dist <= max_dist) & (kpos >= 0)
    bias = -(slopes[None, :, :, None, None] * (pos_scale * dist).astype(jnp.float32)[:, None, None])
    s = jnp.where(mask[:, None, None], s + bias, -jnp.inf)
    m = jnp.max(s, axis=-1)
    p = jnp.exp(s - m[..., None])
    l = jnp.sum(p, axis=-1)
    acc = jnp.einsum("nbhgqk,nbkhd->nbqhgd", p.astype(vw.dtype), vw, preferred_element_type=jnp.float32)
    acc = acc.reshape(N, Lp, Hk, G, dh)[:, :L]
    m = m.transpose(0, 1, 4, 2, 3).reshape(N, Lp, Hk, G)[:, :L]
    l = l.transpose(0, 1, 4, 2, 3).reshape(N, Lp, Hk, G)[:, :L]
    return acc, m, l


def dilated_attention(q, k, v):
    B, S, _, dh = q.shape
    HG = DIL_HEADS_PER_GROUP
    slopes = alibi_slopes(N_DIL * HG).reshape(N_DIL, HG)
    accs, ms, ls = [], [], []
    for g in range(N_DIL):
        w, d = DIL_WINDOWS[g], DIL_RATES[g]

        def strided(x):
            return x.reshape(B, S // d, d, HG, dh).transpose(0, 2, 1, 3, 4).reshape(B * d, S // d, HG, dh)

        sl = slice(g * HG, (g + 1) * HG)
        acc, m, l = banded_window_attention(strided(q[:, :, sl])[:, :, :, None], strided(k[:, :, sl]),
                                            strided(v[:, :, sl]), slopes[g][:, None], w // d, d)
        accs.append(acc.reshape(B, d, S // d, HG, dh).transpose(0, 2, 1, 3, 4).reshape(B, S, HG, dh))
        ms.append(m.reshape(B, d, S // d, HG).transpose(0, 2, 1, 3).reshape(B, S, HG))
        ls.append(l.reshape(B, d, S // d, HG).transpose(0, 2, 1, 3).reshape(B, S, HG))
    acc = jnp.stack(accs)
    m = jnp.stack(ms)
    l = jnp.stack(ls)
    wts = jnp.exp(m - jnp.max(m, axis=0, keepdims=True))
    out = jnp.sum(acc * wts[..., None], axis=0) / jnp.sum(l * wts, axis=0)[..., None]
    return out.reshape(B, S, HG * dh).astype(q.dtype)


def swa_sink_attention(q, k, v, sinks):
    B, S, Hq, dh = q.shape
    Hkv = k.shape[2]
    G = Hq // Hkv
    slopes = alibi_slopes(Hq).reshape(Hkv, G)
    acc, m, l = banded_window_attention(q.reshape(B, S, Hkv, G, dh), k, v, slopes, SWA_WINDOW - 1, 1)
    sk = sinks.astype(jnp.float32).reshape(Hkv, G)
    mx = jnp.maximum(m, sk)
    a = jnp.exp(m - mx)
    out = acc * a[..., None] / (l * a + jnp.exp(sk - mx))[..., None]
    return out.reshape(B, S, Hq * dh).astype(q.dtype)


def even_mixer(h, w_in, q_norm, w_uq, kv_norm, w_ukv, w_out):
    B, S, _ = h.shape
    hd = MOBA_HEADS * MOBA_HEAD_DIM
    sizes = [MLA_Q_RANK, MLA_KV_RANK, MLA_ROPE_DIM, hd, hd, hd]
    z = h @ w_in
    q_lat, kv_lat, k_pe, qb, kb, vb = jnp.split(z, np.cumsum(sizes)[:-1].tolist(), axis=-1)
    a = mla_attention(q_lat, kv_lat, k_pe, q_norm, w_uq, kv_norm, w_ukv)
    shp = (B, S, MOBA_HEADS, MOBA_HEAD_DIM)
    b = moba_attention(qb.reshape(shp), kb.reshape(shp), vb.reshape(shp))
    return jnp.concatenate([a, b], axis=-1).astype(h.dtype) @ w_out


def odd_mixer(h, w_in, sinks, w_out):
    B, S, _ = h.shape
    cd = N_DIL * DIL_HEADS_PER_GROUP * DIL_HEAD_DIM
    sizes = [cd, cd, cd, SWA_Q_HEADS * SWA_HEAD_DIM, SWA_KV_HEADS * SWA_HEAD_DIM, SWA_KV_HEADS * SWA_HEAD_DIM]
    z = h @ w_in
    qc, kc, vc, qd, kd, vd = jnp.split(z, np.cumsum(sizes)[:-1].tolist(), axis=-1)
    cshp = (B, S, N_DIL * DIL_HEADS_PER_GROUP, DIL_HEAD_DIM)
    c = dilated_attention(qc.reshape(cshp), kc.reshape(cshp), vc.reshape(cshp))
    kvshp = (B, S, SWA_KV_HEADS, SWA_HEAD_DIM)
    d = swa_sink_attention(qd.reshape(B, S, SWA_Q_HEADS, SWA_HEAD_DIM), kd.reshape(kvshp),
                           vd.reshape(kvshp), sinks)
    return jnp.concatenate([c, d], axis=-1).astype(h.dtype) @ w_out


def squared_relu_mlp(h, w_up, w_down):
    return jnp.square(jax.nn.relu(h @ w_up)) @ w_down


def setup_inputs(seed: int = 0) -> dict:
    key = jax.random.key(seed)
    ks = jax.random.split(key, 15)
    n_even = (DEPTH + 1) // 2
    n_odd = DEPTH // 2
    f32 = jnp.float32

    def w(k, shape, fan_in):
        return jax.random.normal(k, shape, f32) * fan_in ** -0.5

    def gain(k, shape):
        return 1.0 + 0.05 * jax.random.normal(k, shape, f32)

    return {
        "x": jax.random.normal(ks[0], (BATCH, SEQ, D_MODEL), f32),
        "attn_norm": gain(ks[1], (DEPTH, D_MODEL)),
        "mlp_norm": gain(ks[2], (DEPTH, D_MODEL)),
        "w_up": w(ks[3], (DEPTH, D_MODEL, D_FF), D_MODEL),
        "w_down": w(ks[4], (DEPTH, D_FF, D_MODEL), D_FF),
        "ev_w_in": w(ks[5], (n_even, D_MODEL, EVEN_IN), D_MODEL),
        "ev_q_norm": gain(ks[6], (n_even, MLA_Q_RANK)),
        "ev_w_uq": w(ks[7], (n_even, MLA_Q_RANK, MLA_HEADS * (MLA_NOPE_DIM + MLA_ROPE_DIM)), MLA_Q_RANK),
        "ev_kv_norm": gain(ks[8], (n_even, MLA_KV_RANK)),
        "ev_w_ukv": w(ks[9], (n_even, MLA_KV_RANK, MLA_HEADS * (MLA_NOPE_DIM + MLA_V_DIM)), MLA_KV_RANK),
        "ev_w_out": w(ks[10], (n_even, EVEN_OUT, D_MODEL), EVEN_OUT),
        "od_w_in": w(ks[11], (n_odd, D_MODEL, ODD_IN), D_MODEL),
        "od_sinks": 0.5 * jax.random.normal(ks[12], (n_odd, SWA_Q_HEADS), f32),
        "od_w_out": w(ks[13], (n_odd, ODD_OUT, D_MODEL), ODD_OUT),
        "final_norm": gain(ks[14], (D_MODEL,)),
    }


def reference(x, attn_norm, mlp_norm, w_up, w_down, ev_w_in, ev_q_norm, ev_w_uq, ev_kv_norm,
              ev_w_ukv, ev_w_out, od_w_in, od_sinks, od_w_out, final_norm):
    for layer in range(DEPTH):
        i = layer // 2
        h = rms_norm(x, attn_norm[layer])
        if layer % 2 == 0:
            mix = even_mixer(h, ev_w_in[i], ev_q_norm[i], ev_w_uq[i], ev_kv_norm[i], ev_w_ukv[i], ev_w_out[i])
        else:
            mix = odd_mixer(h, od_w_in[i], od_sinks[i], od_w_out[i])
        x = x + mix.astype(x.dtype)
        h = rms_norm(x, mlp_norm[layer])
        x = x + squared_relu_mlp(h, w_up[layer], w_down[layer]).astype(x.dtype)
    return rms_norm(x, final_norm)
```

```python
import functools

import jax
import jax.numpy as jnp
from jax import lax
from jax.experimental import pallas as pl
from jax.experimental.pallas import tpu as pltpu

F32 = jnp.float32
BF16 = jnp.bfloat16

NORM_EPS = 1e-6
D_FF_MULT = 4

MLA_HEADS = 8
MLA_Q_RANK = 512
MLA_KV_RANK = 512
MLA_NOPE_DIM = 128
MLA_ROPE_DIM = 64
MLA_V_DIM = 128
ROPE_THETA = 10000.0

MOBA_HEADS = 8
MOBA_HEAD_DIM = 128
MOBA_BLOCK = 256
MOBA_TOPK = 3

DIL_WINDOWS = (128, 512, 2048)
DIL_RATES = (1, 4, 16)
N_DIL = 3
DIL_HEADS_PER_GROUP = 4
DIL_HEAD_DIM = 128

SWA_Q_HEADS = 16
SWA_KV_HEADS = 2
SWA_HEAD_DIM = 64
SWA_WINDOW = 128

BAND = 128

LANES = 128
MLA_QK_PAD = 256
EVEN_Z = 4224
ODD_Z = 6144
VMEM_LIMIT = 56 * 1024 * 1024


def _params(sem):
    return pltpu.CompilerParams(dimension_semantics=sem, vmem_limit_bytes=VMEM_LIMIT)


def _rms(x, g):
    ms = jnp.mean(x * x, axis=-1, keepdims=True)
    return x * lax.rsqrt(ms + NORM_EPS) * g


def _qk(q, k):
    return lax.dot_general(q, k, (((1,), (1,)), ((), ())), preferred_element_type=F32)


def _norm_matmul_kernel(x_ref, g_ref, w_ref, o_ref, h_ref):
    @pl.when(pl.program_id(1) == 0)
    def _():
        h_ref[...] = _rms(x_ref[...], g_ref[...]).astype(BF16)

    o_ref[...] = jnp.dot(h_ref[...], w_ref[...], preferred_element_type=F32).astype(o_ref.dtype)


def norm_matmul(x, g, w, *, tm, tn):
    m, k = x.shape
    n = w.shape[1]
    return pl.pallas_call(
        _norm_matmul_kernel,
        grid=(m // tm, n // tn),
        in_specs=[
            pl.BlockSpec((tm, k), lambda i, j: (i, 0)),
            pl.BlockSpec((1, k), lambda i, j: (0, 0)),
            pl.BlockSpec((k, tn), lambda i, j: (0, j)),
        ],
        out_specs=pl.BlockSpec((tm, tn), lambda i, j: (i, j)),
        out_shape=jax.ShapeDtypeStruct((m, n), BF16),
        scratch_shapes=[pltpu.VMEM((tm, k), BF16)],
        compiler_params=_params(("parallel", "arbitrary")),
        name="norm_matmul",
    )(x, g, w)


def _mlp_kernel(x_ref, g_ref, wu_ref, wd_ref, fg_ref, o_ref, h_ref, acc_ref, *, final_norm):
    f = pl.program_id(1)

    @pl.when(f == 0)
    def _():
        h_ref[...] = _rms(x_ref[...], g_ref[...]).astype(BF16)
        acc_ref[...] = jnp.zeros_like(acc_ref)

    u = jnp.dot(h_ref[...], wu_ref[...], preferred_element_type=F32)
    u = jnp.maximum(u, 0.0)
    u = u * u
    acc_ref[...] += jnp.dot(u.astype(BF16), wd_ref[...], preferred_element_type=F32)

    @pl.when(f == pl.num_programs(1) - 1)
    def _():
        y = x_ref[...] + acc_ref[...]
        if final_norm:
            y = _rms(y, fg_ref[...])
        o_ref[...] = y


def mlp_residual(x, g, w_up, w_down, final_g, *, tm, tf, final_norm):
    m, d = x.shape
    ff = w_up.shape[1]
    return pl.pallas_call(
        functools.partial(_mlp_kernel, final_norm=final_norm),
        grid=(m // tm, ff // tf),
        in_specs=[
            pl.BlockSpec((tm, d), lambda i, f: (i, 0)),
            pl.BlockSpec((1, d), lambda i, f: (0, 0)),
            pl.BlockSpec((d, tf), lambda i, f: (0, f)),
            pl.BlockSpec((tf, d), lambda i, f: (f, 0)),
            pl.BlockSpec((1, d), lambda i, f: (0, 0)),
        ],
        out_specs=pl.BlockSpec((tm, d), lambda i, f: (i, 0)),
        out_shape=jax.ShapeDtypeStruct((m, d), F32),
        scratch_shapes=[pltpu.VMEM((tm, d), BF16), pltpu.VMEM((tm, d), F32)],
        compiler_params=_params(("parallel", "arbitrary")),
        name="mlp_residual",
    )(x, g, w_up, w_down, final_g)


def _proj_residual_kernel(x_ref, a_ref, b_ref, wa_ref, wb_ref, o_ref):
    y = jnp.dot(a_ref[...], wa_ref[...], preferred_element_type=F32)
    y = y + jnp.dot(b_ref[...], wb_ref[...], preferred_element_type=F32)
    o_ref[...] = x_ref[...] + y


def proj_residual(x, a, b, wa, wb, *, tm):
    m, d = x.shape
    ka, kb = a.shape[1], b.shape[1]
    return pl.pallas_call(
        _proj_residual_kernel,
        grid=(m // tm,),
        in_specs=[
            pl.BlockSpec((tm, d), lambda i: (i, 0)),
            pl.BlockSpec((tm, ka), lambda i: (i, 0)),
            pl.BlockSpec((tm, kb), lambda i: (i, 0)),
            pl.BlockSpec((ka, d), lambda i: (0, 0)),
            pl.BlockSpec((kb, d), lambda i: (0, 0)),
        ],
        out_specs=pl.BlockSpec((tm, d), lambda i: (i, 0)),
        out_shape=jax.ShapeDtypeStruct((m, d), F32),
        compiler_params=_params(("parallel",)),
        name="proj_residual",
    )(x, a, b, wa, wb)


def _mla_up_kernel(ql_ref, kvl_ref, kpe_ref, qn_ref, kvn_ref, wq_ref, wk_ref, wv_ref, cos_ref, sin_ref,
                   q_ref, k_ref, v_ref):
    cos = cos_ref[...]
    sin = sin_ref[...]

    def rope(t):
        return t * cos + pltpu.roll(t, LANES // 2, 1) * sin

    hq = _rms(ql_ref[...].astype(F32), qn_ref[...]).astype(BF16)
    yq = jnp.dot(hq, wq_ref[...], preferred_element_type=F32)
    hkv = _rms(kvl_ref[...].astype(F32), kvn_ref[...]).astype(BF16)
    yk = jnp.dot(hkv, wk_ref[...], preferred_element_type=F32)
    yv = jnp.dot(hkv, wv_ref[...], preferred_element_type=F32)
    krot = rope(kpe_ref[...].astype(F32)).astype(BF16)
    for h in range(MLA_HEADS):
        lo = h * MLA_QK_PAD
        mid = lo + MLA_NOPE_DIM
        hi = lo + MLA_QK_PAD
        q_ref[:, lo:mid] = yq[:, lo:mid].astype(BF16)
        q_ref[:, mid:hi] = rope(yq[:, mid:hi]).astype(BF16)
        k_ref[:, lo:mid] = yk[:, h * MLA_NOPE_DIM:(h + 1) * MLA_NOPE_DIM].astype(BF16)
        k_ref[:, mid:hi] = krot
    v_ref[...] = yv.astype(BF16)


def mla_up(z, qn, kvn, wq, wk, wv, cos, sin, *, seq, tm):
    t = z.shape[0]
    r = MLA_Q_RANK
    nq = MLA_HEADS * MLA_QK_PAD
    nv = MLA_HEADS * MLA_V_DIM
    kpe_blk = (EVEN_Z - LANES) // LANES
    pos_blocks = seq // tm
    return pl.pallas_call(
        _mla_up_kernel,
        grid=(t // tm,),
        in_specs=[
            pl.BlockSpec((tm, r), lambda i: (i, 0)),
            pl.BlockSpec((tm, r), lambda i: (i, 1)),
            pl.BlockSpec((tm, LANES), lambda i: (i, kpe_blk)),
            pl.BlockSpec((1, r), lambda i: (0, 0)),
            pl.BlockSpec((1, r), lambda i: (0, 0)),
            pl.BlockSpec((r, nq), lambda i: (0, 0)),
            pl.BlockSpec((r, nv), lambda i: (0, 0)),
            pl.BlockSpec((r, nv), lambda i: (0, 0)),
            pl.BlockSpec((tm, LANES), lambda i: (i % pos_blocks, 0)),
            pl.BlockSpec((tm, LANES), lambda i: (i % pos_blocks, 0)),
        ],
        out_specs=[
            pl.BlockSpec((tm, nq), lambda i: (i, 0)),
            pl.BlockSpec((tm, nq), lambda i: (i, 0)),
            pl.BlockSpec((tm, nv), lambda i: (i, 0)),
        ],
        out_shape=[
            jax.ShapeDtypeStruct((t, nq), BF16),
            jax.ShapeDtypeStruct((t, nq), BF16),
            jax.ShapeDtypeStruct((t, nv), BF16),
        ],
        compiler_params=_params(("parallel",)),
        name="mla_up",
    )(z, z, z, qn, kvn, wq, wk, wv, cos, sin)


def _softmax_step(s, v, carry):
    m, l, acc = carry
    m_new = jnp.maximum(m, jnp.max(s, axis=-1, keepdims=True))
    p = jnp.exp(s - m_new)
    alpha = jnp.exp(m - m_new)
    l = alpha * l + jnp.sum(p, axis=-1, keepdims=True)
    acc = alpha * acc + jnp.dot(p.astype(BF16), v, preferred_element_type=F32)
    return m_new, l, acc


def _mla_attn_kernel(q_ref, k_ref, v_ref, o_ref, *, blk):
    qi = pl.program_id(2)
    q = q_ref[...]
    dv = v_ref.shape[-1]

    def kv(j):
        start = pl.multiple_of(j * blk, blk)
        return k_ref[pl.ds(start, blk), :], v_ref[pl.ds(start, blk), :]

    k0, v0 = kv(qi)
    row = lax.broadcasted_iota(jnp.int32, (blk, blk), 0)
    col = lax.broadcasted_iota(jnp.int32, (blk, blk), 1)
    s0 = jnp.where(col <= row, _qk(q, k0), -jnp.inf)
    init = (jnp.full((blk, 1), -jnp.inf, F32), jnp.zeros((blk, 1), F32), jnp.zeros((blk, dv), F32))
    carry = _softmax_step(s0, v0, init)

    def body(j, c):
        kj, vj = kv(j)
        return _softmax_step(_qk(q, kj), vj, c)

    _, l, acc = lax.fori_loop(0, qi, body, carry)
    o_ref[...] = (acc / l).astype(o_ref.dtype)


def mla_attention(q, k, v, *, batch, seq, blk):
    dq = MLA_QK_PAD
    dv = MLA_V_DIM
    return pl.pallas_call(
        functools.partial(_mla_attn_kernel, blk=blk),
        grid=(batch, MLA_HEADS, seq // blk),
        in_specs=[
            pl.BlockSpec((None, blk, dq), lambda b, h, i: (b, i, h)),
            pl.BlockSpec((None, seq, dq), lambda b, h, i: (b, 0, h)),
            pl.BlockSpec((None, seq, dv), lambda b, h, i: (b, 0, h)),
        ],
        out_specs=pl.BlockSpec((None, blk, dv), lambda b, h, i: (b, i, h)),
        out_shape=jax.ShapeDtypeStruct((batch, seq, MLA_HEADS * dv), BF16),
        compiler_params=_params(("parallel", "parallel", "arbitrary")),
        name="mla_attention",
    )(q, k, v)


def _moba_kernel(slopes_ref, q_ref, k_ref, v_ref, o_ref, kmean_ref, *, nb):
    blk = MOBA_BLOCK
    h = pl.program_id(1)
    i = pl.program_id(2)
    slope = slopes_ref[h]

    @pl.when(i == 0)
    def _():
        kmean_ref[...] = jnp.zeros_like(kmean_ref)
        for n in range(nb):
            kn = k_ref[n * blk:(n + 1) * blk, :].astype(F32)
            kmean_ref[n:n + 1, :] = jnp.mean(kn, axis=0, keepdims=True)

    q = q_ref[...]

    km = kmean_ref[...]
    km_hi = km.astype(BF16)
    km_lo = (km - km_hi.astype(F32)).astype(BF16)
    gate = _qk(q, km_hi) + _qk(q, km_lo)

    lane = lax.broadcasted_iota(jnp.int32, (blk, LANES), 1)
    valid = lane < i
    g = jnp.where(valid, gate, -jnp.inf)
    rank = jnp.zeros((blk, LANES), jnp.int32)
    for n in range(nb):
        gn = g[:, n:n + 1]
        beats = (gn > g) | ((gn == g) & (lane > n))
        rank = rank + beats.astype(jnp.int32)
    selbias = jnp.where(valid & (rank < min(MOBA_TOPK, nb)), 0.0, -jnp.inf)

    def kv(n):
        start = pl.multiple_of(n * blk, blk)
        return k_ref[pl.ds(start, blk), :], v_ref[pl.ds(start, blk), :]

    kcol = lax.broadcasted_iota(jnp.int32, (1, blk), 1)

    k0, v0 = kv(i)
    row = lax.broadcasted_iota(jnp.int32, (blk, blk), 0)
    col = lax.broadcasted_iota(jnp.int32, (blk, blk), 1)
    s0 = _qk(q, k0) + kcol.astype(F32) * slope
    s0 = jnp.where(col <= row, s0, -jnp.inf)
    init = (jnp.full((blk, 1), -jnp.inf, F32), jnp.zeros((blk, 1), F32), jnp.zeros((blk, v_ref.shape[-1]), F32))
    carry = _softmax_step(s0, v0, init)

    def body(n, c):
        kn, vn = kv(n)
        colbias = (kcol + (n - i) * blk).astype(F32) * slope
        rowbias = jnp.max(jnp.where(lane == n, selbias, -jnp.inf), axis=-1, keepdims=True)
        return _softmax_step(_qk(q, kn) + colbias + rowbias, vn, c)

    _, l, acc = lax.fori_loop(0, i, body, carry)
    o_ref[...] = (acc / l).astype(o_ref.dtype)


def moba_attention(z, slopes, *, batch, seq):
    blk = MOBA_BLOCK
    dh = MOBA_HEAD_DIM
    nb = seq // blk
    qo, ko, vo = 1024 // dh, 2048 // dh, 3072 // dh
    return pl.pallas_call(
        functools.partial(_moba_kernel, nb=nb),
        grid=(batch, MOBA_HEADS, nb),
        in_specs=[
            pl.BlockSpec(memory_space=pltpu.SMEM),
            pl.BlockSpec((None, blk, dh), lambda b, h, i: (b, i, qo + h)),
            pl.BlockSpec((None, seq, dh), lambda b, h, i: (b, 0, ko + h)),
            pl.BlockSpec((None, seq, dh), lambda b, h, i: (b, 0, vo + h)),
        ],
        out_specs=pl.BlockSpec((None, blk, dh), lambda b, h, i: (b, i, h)),
        out_shape=jax.ShapeDtypeStruct((batch, seq, MOBA_HEADS * dh), BF16),
        scratch_shapes=[pltpu.VMEM((LANES, dh), F32)],
        compiler_params=_params(("parallel", "parallel", "arbitrary")),
        name="moba_attention",
    )(slopes, z, z, z)


def _band_mask(j, max_dist):
    row = lax.broadcasted_iota(jnp.int32, (BAND, 2 * BAND), 0)
    col = lax.broadcasted_iota(jnp.int32, (BAND, 2 * BAND), 1)
    dist = BAND + row - col
    mask = (dist >= 0) & (dist <= max_dist) & ((col >= BAND) | (j > 0))
    return mask, dist.astype(F32)


def _dil_kernel(q_ref, kp_ref, ko_ref, vp_ref, vo_ref, o_ref, lse_ref, *, group):
    j = pl.program_id(2)
    rate = DIL_RATES[group]
    mask, distf = _band_mask(j, DIL_WINDOWS[group] // rate)
    dh = DIL_HEAD_DIM
    for hh in range(DIL_HEADS_PER_GROUP):
        head = group * DIL_HEADS_PER_GROUP + hh
        slope = 2.0 ** (-8.0 * (head + 1) / (N_DIL * DIL_HEADS_PER_GROUP)) * rate
        sl = slice(hh * dh, (hh + 1) * dh)
        kw = jnp.concatenate([kp_ref[:, sl], ko_ref[:, sl]], axis=0)
        vw = jnp.concatenate([vp_ref[:, sl], vo_ref[:, sl]], axis=0)
        s = _qk(q_ref[:, sl], kw) - slope * distf
        s = jnp.where(mask, s, -jnp.inf)
        m = jnp.max(s, axis=-1, keepdims=True)
        p = jnp.exp(s - m)
        l = jnp.sum(p, axis=-1, keepdims=True)
        acc = jnp.dot(p.astype(BF16), vw, preferred_element_type=F32)
        o_ref[:, sl] = acc / l
        lse_ref[:, sl] = jnp.broadcast_to(m + jnp.log(l), (BAND, dh))


def dilated_group(z, *, group, batch, seq):
    rate = DIL_RATES[group]
    length = seq // rate
    w = DIL_HEADS_PER_GROUP * DIL_HEAD_DIM
    zb = ODD_Z // w
    zv = z.reshape(batch, length, rate * ODD_Z)
    qo, ko, vo = 2 + group, 5 + group, 8 + group

    def spec(off, prev):
        if prev:
            return pl.BlockSpec((None, BAND, w), lambda b, r, j: (b, jnp.maximum(j - 1, 0), r * zb + off))
        return pl.BlockSpec((None, BAND, w), lambda b, r, j: (b, j, r * zb + off))

    out_spec = pl.BlockSpec((None, BAND, w), lambda b, r, j: (b, j, r))
    o, lse = pl.pallas_call(
        functools.partial(_dil_kernel, group=group),
        grid=(batch, rate, length // BAND),
        in_specs=[spec(qo, False), spec(ko, True), spec(ko, False), spec(vo, True), spec(vo, False)],
        out_specs=[out_spec, out_spec],
        out_shape=[jax.ShapeDtypeStruct((batch, length, rate * w), F32)] * 2,
        compiler_params=_params(("parallel", "parallel", "arbitrary")),
        name=f"dilated_group{group}",
    )(zv, zv, zv, zv, zv)
    return o.reshape(batch * seq, w), lse.reshape(batch * seq, w)


def _dil_merge_kernel(o0_ref, l0_ref, o1_ref, l1_ref, o2_ref, l2_ref, out_ref):
    l0, l1, l2 = l0_ref[...], l1_ref[...], l2_ref[...]
    mx = jnp.maximum(jnp.maximum(l0, l1), l2)
    w0, w1, w2 = jnp.exp(l0 - mx), jnp.exp(l1 - mx), jnp.exp(l2 - mx)
    num = o0_ref[...] * w0 + o1_ref[...] * w1 + o2_ref[...] * w2
    out_ref[...] = (num / (w0 + w1 + w2)).astype(out_ref.dtype)


def dilated_merge(parts, *, tm):
    t, w = parts[0].shape
    spec = pl.BlockSpec((tm, w), lambda i: (i, 0))
    return pl.pallas_call(
        _dil_merge_kernel,
        grid=(t // tm,),
        in_specs=[spec] * 6,
        out_specs=spec,
        out_shape=jax.ShapeDtypeStruct((t, w), BF16),
        compiler_params=_params(("parallel",)),
        name="dilated_merge",
    )(*parts)


def _swa_kernel(sinks_ref, q_ref, kp_ref, ko_ref, vp_ref, vo_ref, o_ref):
    j = pl.program_id(1)
    mask, distf = _band_mask(j, SWA_WINDOW - 1)
    heads_per_kv = SWA_Q_HEADS // SWA_KV_HEADS
    half = LANES // 2

    kw = jnp.concatenate([kp_ref[...], ko_ref[...]], axis=0).astype(F32)
    vw = jnp.concatenate([vp_ref[...], vo_ref[...]], axis=0).astype(F32)
    low = lax.broadcasted_iota(jnp.int32, (2 * BAND, LANES), 1) < half

    def halves(x):
        xr = pltpu.roll(x, half, 1)
        even = (jnp.where(low, x, 0.0).astype(BF16), jnp.where(low, xr, 0.0).astype(BF16))
        odd = (jnp.where(low, 0.0, xr).astype(BF16), jnp.where(low, 0.0, x).astype(BF16))
        return even, odd

    k_even, k_odd = halves(kw)
    v_even, v_odd = halves(vw)

    for pair in range(SWA_Q_HEADS // 2):
        kvh = (2 * pair) // heads_per_kv
        slab = q_ref[:, pair * LANES:(pair + 1) * LANES]
        out = None
        for par, (kk, vv) in enumerate(((k_even[kvh], v_even[kvh]), (k_odd[kvh], v_odd[kvh]))):
            head = 2 * pair + par
            slope = 2.0 ** (-8.0 * (head + 1) / SWA_Q_HEADS)
            s = _qk(slab, kk) - slope * distf
            s = jnp.where(mask, s, -jnp.inf)
            m = jnp.max(s, axis=-1, keepdims=True)
            p = jnp.exp(s - m)
            l = jnp.sum(p, axis=-1, keepdims=True)
            acc = jnp.dot(p.astype(BF16), vv, preferred_element_type=F32)
            sk = sinks_ref[head]
            mx = jnp.maximum(m, sk)
            a = jnp.exp(m - mx)
            o = acc * a / (l * a + jnp.exp(sk - mx))
            out = o if out is None else out + o
        o_ref[:, pair * LANES:(pair + 1) * LANES] = out.astype(o_ref.dtype)


def swa_attention(z, sinks, *, batch, seq):
    wq = SWA_Q_HEADS * SWA_HEAD_DIM
    ko, vo = 5632 // LANES, 5760 // LANES

    def kvspec(off, prev):
        if prev:
            return pl.BlockSpec((None, BAND, LANES), lambda b, j: (b, jnp.maximum(j - 1, 0), off))
        return pl.BlockSpec((None, BAND, LANES), lambda b, j: (b, j, off))

    return pl.pallas_call(
        _swa_kernel,
        grid=(batch, seq // BAND),
        in_specs=[
            pl.BlockSpec(memory_space=pltpu.SMEM),
            pl.BlockSpec((None, BAND, wq), lambda b, j: (b, j, 0)),
            kvspec(ko, True), kvspec(ko, False), kvspec(vo, True), kvspec(vo, False),
        ],
        out_specs=pl.BlockSpec((None, BAND, wq), lambda b, j: (b, j, 0)),
        out_shape=jax.ShapeDtypeStruct((batch, seq, wq), BF16),
        compiler_params=_params(("parallel", "arbitrary")),
        name="swa_attention",
    )(sinks, z, z, z, z, z)


def _rope_pad(w):
    half = MLA_ROPE_DIM // 2
    z = jnp.zeros(w.shape[:-1] + (half,), w.dtype)
    return jnp.concatenate([w[..., :half], z, w[..., half:], z], axis=-1)


def _even_weights(w_in, w_uq, w_ukv, w_out):
    hd = MOBA_HEADS * MOBA_HEAD_DIM
    o = 0
    q_lat = w_in[:, o:o + MLA_Q_RANK]; o += MLA_Q_RANK
    kv_lat = w_in[:, o:o + MLA_KV_RANK]; o += MLA_KV_RANK
    k_pe = w_in[:, o:o + MLA_ROPE_DIM]; o += MLA_ROPE_DIM
    qb = w_in[:, o:o + hd] * (MOBA_HEAD_DIM ** -0.5); o += hd
    kb = w_in[:, o:o + hd]; o += hd
    vb = w_in[:, o:o + hd]
    w_in_p = jnp.concatenate([q_lat, kv_lat, qb, kb, vb, _rope_pad(k_pe)], axis=-1).astype(BF16)

    r = w_uq.shape[0]
    uq = w_uq.reshape(r, MLA_HEADS, MLA_NOPE_DIM + MLA_ROPE_DIM) * ((MLA_NOPE_DIM + MLA_ROPE_DIM) ** -0.5)
    wq = jnp.concatenate([uq[..., :MLA_NOPE_DIM], _rope_pad(uq[..., MLA_NOPE_DIM:])], axis=-1)
    wq = wq.reshape(r, MLA_HEADS * MLA_QK_PAD).astype(BF16)
    ukv = w_ukv.reshape(w_ukv.shape[0], MLA_HEADS, MLA_NOPE_DIM + MLA_V_DIM)
    wk = ukv[..., :MLA_NOPE_DIM].reshape(-1, MLA_HEADS * MLA_NOPE_DIM).astype(BF16)
    wv = ukv[..., MLA_NOPE_DIM:].reshape(-1, MLA_HEADS * MLA_V_DIM).astype(BF16)
    na = MLA_HEADS * MLA_V_DIM
    return w_in_p, wq, wk, wv, w_out[:na].astype(BF16), w_out[na:].astype(BF16)


def _odd_weights(w_in, w_out):
    cd = N_DIL * DIL_HEADS_PER_GROUP * DIL_HEAD_DIM
    qd_w = SWA_Q_HEADS * SWA_HEAD_DIM
    kd_w = SWA_KV_HEADS * SWA_HEAD_DIM
    o = 0
    qc = w_in[:, o:o + cd] * (DIL_HEAD_DIM ** -0.5); o += cd
    kc = w_in[:, o:o + cd]; o += cd
    vc = w_in[:, o:o + cd]; o += cd
    qd = w_in[:, o:o + qd_w] * (SWA_HEAD_DIM ** -0.5); o += qd_w
    kd = w_in[:, o:o + kd_w]; o += kd_w
    vd = w_in[:, o:o + kd_w]
    used = qd_w + 3 * cd + 2 * kd_w
    pad = jnp.zeros((w_in.shape[0], ODD_Z - used), w_in.dtype)
    w_in_p = jnp.concatenate([qd, qc, kc, vc, kd, vd, pad], axis=-1).astype(BF16)
    nc = DIL_HEADS_PER_GROUP * DIL_HEAD_DIM
    return w_in_p, w_out[:nc].astype(BF16), w_out[nc:].astype(BF16)


def _rope_tables(seq):
    half = MLA_ROPE_DIM // 2
    inv_freq = ROPE_THETA ** (-jnp.arange(half, dtype=F32) / half)
    ang = jnp.arange(seq, dtype=jnp.int32).astype(F32)[:, None] * inv_freq[None, :]
    c, s = jnp.cos(ang), jnp.sin(ang)
    z = jnp.zeros_like(c)
    return jnp.concatenate([c, z, c, z], axis=-1), jnp.concatenate([-s, z, s, z], axis=-1)


def kernel(x, attn_norm, mlp_norm, w_up, w_down, ev_w_in, ev_q_norm, ev_w_uq, ev_kv_norm, ev_w_ukv, ev_w_out,
           od_w_in, od_sinks, od_w_out, final_norm):
    batch, seq, d = x.shape
    depth = attn_norm.shape[0]
    t = batch * seq
    xt = x.reshape(t, d)
    cos, sin = _rope_tables(seq)
    moba_slopes = jnp.exp2(-8.0 * jnp.arange(1, MOBA_HEADS + 1, dtype=F32) / MOBA_HEADS)
    fg = final_norm.reshape(1, d)

    for layer in range(depth):
        i = layer // 2
        g = attn_norm[layer].reshape(1, d)
        if layer % 2 == 0:
            w_in_p, wq, wk, wv, wo_a, wo_b = _even_weights(ev_w_in[i], ev_w_uq[i], ev_w_ukv[i], ev_w_out[i])
            z = norm_matmul(xt, g, w_in_p, tm=512, tn=1408)
            q, k, v = mla_up(z, ev_q_norm[i].reshape(1, -1), ev_kv_norm[i].reshape(1, -1), wq, wk, wv, cos, sin,
                             seq=seq, tm=256)
            a = mla_attention(q.reshape(batch, seq, -1), k.reshape(batch, seq, -1), v.reshape(batch, seq, -1),
                              batch=batch, seq=seq, blk=256)
            b = moba_attention(z.reshape(batch, seq, EVEN_Z), moba_slopes, batch=batch, seq=seq)
            xt = proj_residual(xt, a.reshape(t, -1), b.reshape(t, -1), wo_a, wo_b, tm=512)
        else:
            w_in_p, wo_c, wo_d = _odd_weights(od_w_in[i], od_w_out[i])
            z = norm_matmul(xt, g, w_in_p, tm=512, tn=1536)
            z3 = z.reshape(batch, seq, ODD_Z)
            parts = []
            for grp in range(N_DIL):
                parts.extend(dilated_group(z3, group=grp, batch=batch, seq=seq))
            c = dilated_merge(parts, tm=1024)
            dd = swa_attention(z3, od_sinks[i], batch=batch, seq=seq)
            xt = proj_residual(xt, c, dd.reshape(t, -1), wo_c, wo_d, tm=512)
        xt = mlp_residual(xt, mlp_norm[layer].reshape(1, d), w_up[layer].astype(BF16), w_down[layer].astype(BF16), fg,
                          tm=512, tf=1024, final_norm=(layer == depth - 1))
    return xt.reshape(batch, seq, d)
```

```python
import functools

import jax
import jax.numpy as jnp
from jax import lax
from jax.experimental import pallas as pl
from jax.experimental.pallas import tpu as pltpu

F32 = jnp.float32
BF16 = jnp.bfloat16

NORM_EPS = 1e-6
D_FF_MULT = 4

MLA_HEADS = 8
MLA_Q_RANK = 512
MLA_KV_RANK = 512
MLA_NOPE_DIM = 128
MLA_ROPE_DIM = 64
MLA_V_DIM = 128
ROPE_THETA = 10000.0

MOBA_HEADS = 8
MOBA_HEAD_DIM = 128
MOBA_BLOCK = 256
MOBA_TOPK = 3

DIL_WINDOWS = (128, 512, 2048)
DIL_RATES = (1, 4, 16)
N_DIL = 3
DIL_HEADS_PER_GROUP = 4
DIL_HEAD_DIM = 128

SWA_Q_HEADS = 16
SWA_KV_HEADS = 2
SWA_HEAD_DIM = 64
SWA_WINDOW = 128

BAND = 128

LANES = 128
MLA_QK_PAD = 256
EVEN_Z = 4224
ODD_Z = 6144
VMEM_LIMIT = 56 * 1024 * 1024


def _params(sem):
    return pltpu.CompilerParams(dimension_semantics=sem, vmem_limit_bytes=VMEM_LIMIT)


def _rms(x, g):
    ms = jnp.mean(x * x, axis=-1, keepdims=True)
    return x * lax.rsqrt(ms + NORM_EPS) * g


def _qk(q, k):
    return lax.dot_general(q, k, (((1,), (1,)), ((), ())), preferred_element_type=F32)


def _norm_matmul_kernel(x_ref, g_ref, w_ref, o_ref, h_ref):
    @pl.when(pl.program_id(1) == 0)
    def _():
        h_ref[...] = _rms(x_ref[...], g_ref[...]).astype(BF16)

    o_ref[...] = jnp.dot(h_ref[...], w_ref[...], preferred_element_type=F32).astype(o_ref.dtype)


def norm_matmul(x, g, w, *, tm, tn):
    m, k = x.shape
    n = w.shape[1]
    return pl.pallas_call(
        _norm_matmul_kernel,
        grid=(m // tm, n // tn),
        in_specs=[
            pl.BlockSpec((tm, k), lambda i, j: (i, 0)),
            pl.BlockSpec((1, k), lambda i, j: (0, 0)),
            pl.BlockSpec((k, tn), lambda i, j: (0, j)),
        ],
        out_specs=pl.BlockSpec((tm, tn), lambda i, j: (i, j)),
        out_shape=jax.ShapeDtypeStruct((m, n), BF16),
        scratch_shapes=[pltpu.VMEM((tm, k), BF16)],
        compiler_params=_params(("parallel", "arbitrary")),
        name="norm_matmul",
    )(x, g, w)


def _mlp_kernel(x_ref, g_ref, wu_ref, wd_ref, fg_ref, o_ref, h_ref, acc_ref, *, final_norm):
    f = pl.program_id(1)

    @pl.when(f == 0)
    def _():
        h_ref[...] = _rms(x_ref[...], g_ref[...]).astype(BF16)
        acc_ref[...] = jnp.zeros_like(acc_ref)

    u = jnp.dot(h_ref[...], wu_ref[...], preferred_element_type=F32)
    u = jnp.maximum(u, 0.0)
    u = u * u
    acc_ref[...] += jnp.dot(u.astype(BF16), wd_ref[...], preferred_element_type=F32)

    @pl.when(f == pl.num_programs(1) - 1)
    def _():
        y = x_ref[...] + acc_ref[...]
        if final_norm:
            y = _rms(y, fg_ref[...])
        o_ref[...] = y


def mlp_residual(x, g, w_up, w_down, final_g, *, tm, tf, final_norm):
    m, d = x.shape
    ff = w_up.shape[1]
    return pl.pallas_call(
        functools.partial(_mlp_kernel, final_norm=final_norm),
        grid=(m // tm, ff // tf),
        in_specs=[
            pl.BlockSpec((tm, d), lambda i, f: (i, 0)),
            pl.BlockSpec((1, d), lambda i, f: (0, 0)),
            pl.BlockSpec((d, tf), lambda i, f: (0, f)),
            pl.BlockSpec((tf, d), lambda i, f: (f, 0)),
            pl.BlockSpec((1, d), lambda i, f: (0, 0)),
        ],
        out_specs=pl.BlockSpec((tm, d), lambda i, f: (i, 0)),
        out_shape=jax.ShapeDtypeStruct((m, d), F32),
        scratch_shapes=[pltpu.VMEM((tm, d), BF16), pltpu.VMEM((tm, d), F32)],
        compiler_params=_params(("parallel", "arbitrary")),
        name="mlp_residual",
    )(x, g, w_up, w_down, final_g)


def _proj_residual_kernel(x_ref, a_ref, b_ref, wa_ref, wb_ref, o_ref):
    y = jnp.dot(a_ref[...], wa_ref[...], preferred_element_type=F32)
    y = y + jnp.dot(b_ref[...], wb_ref[...], preferred_element_type=F32)
    o_ref[...] = x_ref[...] + y


def proj_residual(x, a, b, wa, wb, *, tm):
    m, d = x.shape
    ka, kb = a.shape[1], b.shape[1]
    return pl.pallas_call(
        _proj_residual_kernel,
        grid=(m // tm,),
        in_specs=[
            pl.BlockSpec((tm, d), lambda i: (i, 0)),
            pl.BlockSpec((tm, ka), lambda i: (i, 0)),
            pl.BlockSpec((tm, kb), lambda i: (i, 0)),
            pl.BlockSpec((ka, d), lambda i: (0, 0)),
            pl.BlockSpec((kb, d), lambda i: (0, 0)),
        ],
        out_specs=pl.BlockSpec((tm, d), lambda i: (i, 0)),
        out_shape=jax.ShapeDtypeStruct((m, d), F32),
        compiler_params=_params(("parallel",)),
        name="proj_residual",
    )(x, a, b, wa, wb)


def _mla_up_kernel(ql_ref, kvl_ref, kpe_ref, qn_ref, kvn_ref, wq_ref, wk_ref, wv_ref, cos_ref, sin_ref,
                   q_ref, k_ref, v_ref):
    cos = cos_ref[...]
    sin = sin_ref[...]

    def rope(t):
        return t * cos + pltpu.roll(t, LANES // 2, 1) * sin

    hq = _rms(ql_ref[...].astype(F32), qn_ref[...]).astype(BF16)
    yq = jnp.dot(hq, wq_ref[...], preferred_element_type=F32)
    hkv = _rms(kvl_ref[...].astype(F32), kvn_ref[...]).astype(BF16)
    yk = jnp.dot(hkv, wk_ref[...], preferred_element_type=F32)
    yv = jnp.dot(hkv, wv_ref[...], preferred_element_type=F32)
    krot = rope(kpe_ref[...].astype(F32)).astype(BF16)
    for h in range(MLA_HEADS):
        lo = h * MLA_QK_PAD
        mid = lo + MLA_NOPE_DIM
        hi = lo + MLA_QK_PAD
        q_ref[:, lo:mid] = yq[:, lo:mid].astype(BF16)
        q_ref[:, mid:hi] = rope(yq[:, mid:hi]).astype(BF16)
        k_ref[:, lo:mid] = yk[:, h * MLA_NOPE_DIM:(h + 1) * MLA_NOPE_DIM].astype(BF16)
        k_ref[:, mid:hi] = krot
    v_ref[...] = yv.astype(BF16)


def mla_up(z, qn, kvn, wq, wk, wv, cos, sin, *, seq, tm):
    t = z.shape[0]
    r = MLA_Q_RANK
    nq = MLA_HEADS * MLA_QK_PAD
    nv = MLA_HEADS * MLA_V_DIM
    kpe_blk = (EVEN_Z - LANES) // LANES
    pos_blocks = seq // tm
    return pl.pallas_call(
        _mla_up_kernel,
        grid=(t // tm,),
        in_specs=[
            pl.BlockSpec((tm, r), lambda i: (i, 0)),
            pl.BlockSpec((tm, r), lambda i: (i, 1)),
            pl.BlockSpec((tm, LANES), lambda i: (i, kpe_blk)),
            pl.BlockSpec((1, r), lambda i: (0, 0)),
            pl.BlockSpec((1, r), lambda i: (0, 0)),
            pl.BlockSpec((r, nq), lambda i: (0, 0)),
            pl.BlockSpec((r, nv), lambda i: (0, 0)),
            pl.BlockSpec((r, nv), lambda i: (0, 0)),
            pl.BlockSpec((tm, LANES), lambda i: (i % pos_blocks, 0)),
            pl.BlockSpec((tm, LANES), lambda i: (i % pos_blocks, 0)),
        ],
        out_specs=[
            pl.BlockSpec((tm, nq), lambda i: (i, 0)),
            pl.BlockSpec((tm, nq), lambda i: (i, 0)),
            pl.BlockSpec((tm, nv), lambda i: (i, 0)),
        ],
        out_shape=[
            jax.ShapeDtypeStruct((t, nq), BF16),
            jax.ShapeDtypeStruct((t, nq), BF16),
            jax.ShapeDtypeStruct((t, nv), BF16),
        ],
        compiler_params=_params(("parallel",)),
        name="mla_up",
    )(z, z, z, qn, kvn, wq, wk, wv, cos, sin)


def _softmax_steps_t(sts, vts, carries):
    stats = []
    for st, (m, l, _) in zip(sts, carries):
        m_new = jnp.maximum(m, jnp.max(st, axis=0, keepdims=True))
        p = jnp.exp(st - m_new)
        alpha = jnp.exp(m - m_new)
        stats.append((m_new, alpha * l + jnp.sum(p, axis=0, keepdims=True), alpha, p.astype(BF16)))
    return tuple((m_new, l_new, alpha * acc + jnp.dot(vt, p, preferred_element_type=F32))
                 for (m_new, l_new, alpha, p), vt, (_, _, acc) in zip(stats, vts, carries))


def _softmax_init(dv, nq):
    return (jnp.full((1, nq), -jnp.inf, F32), jnp.zeros((1, nq), F32), jnp.zeros((dv, nq), F32))


def _transpose_bf16(x):
    return x.astype(F32).T.astype(BF16)


def _mla_attn_kernel(q_ref, k_ref, v_ref, o_ref, vt_ref, *, blk, heads):
    qi = pl.program_id(2)
    dq, dv = MLA_QK_PAD, MLA_V_DIM

    @pl.when(qi == 0)
    def _():
        def fill(n, c):
            start = pl.multiple_of(n * blk, blk)
            for g in range(heads):
                vt_ref[g * dv:(g + 1) * dv, pl.ds(start, blk)] = _transpose_bf16(
                    v_ref[pl.ds(start, blk), g * dv:(g + 1) * dv])
            return c

        lax.fori_loop(0, k_ref.shape[0] // blk, fill, 0)

    qts = [_transpose_bf16(q_ref[:, g * dq:(g + 1) * dq]) for g in range(heads)]
    key = lax.broadcasted_iota(jnp.int32, (blk, blk), 0)
    qry = lax.broadcasted_iota(jnp.int32, (blk, blk), 1)

    def tile(n, carries, diagonal):
        start = pl.multiple_of(n * blk, blk)
        sts = [jnp.dot(k_ref[pl.ds(start, blk), g * dq:(g + 1) * dq], qts[g], preferred_element_type=F32)
               for g in range(heads)]
        if diagonal:
            sts = [jnp.where(key <= qry, st, -jnp.inf) for st in sts]
        vts = [vt_ref[g * dv:(g + 1) * dv, pl.ds(start, blk)] for g in range(heads)]
        return _softmax_steps_t(sts, vts, carries)

    carries = tile(qi, tuple(_softmax_init(dv, blk) for _ in range(heads)), True)
    carries = lax.fori_loop(0, qi, lambda n, c: tile(n, c, False), carries)
    for g in range(heads):
        _, l, acc = carries[g]
        o_ref[:, g * dv:(g + 1) * dv] = (acc / l).T.astype(o_ref.dtype)


def mla_attention(q, k, v, *, batch, seq, blk, heads):
    dq = MLA_QK_PAD * heads
    dv = MLA_V_DIM * heads
    return pl.pallas_call(
        functools.partial(_mla_attn_kernel, blk=blk, heads=heads),
        grid=(batch, MLA_HEADS // heads, seq // blk),
        in_specs=[
            pl.BlockSpec((None, blk, dq), lambda b, h, i: (b, i, h)),
            pl.BlockSpec((None, seq, dq), lambda b, h, i: (b, 0, h)),
            pl.BlockSpec((None, seq, dv), lambda b, h, i: (b, 0, h)),
        ],
        out_specs=pl.BlockSpec((None, blk, dv), lambda b, h, i: (b, i, h)),
        out_shape=jax.ShapeDtypeStruct((batch, seq, MLA_HEADS * MLA_V_DIM), BF16),
        scratch_shapes=[pltpu.VMEM((dv, seq), BF16)],
        compiler_params=_params(("parallel", "parallel", "arbitrary")),
        name="mla_attention",
    )(q, k, v)


MOBA_NB_PAD = 16


def _moba_kernel(slopes_ref, q_ref, k_ref, v_ref, o_ref, vt_ref, kmean_ref, sel_ref, *, nb, heads):
    blk = MOBA_BLOCK
    dh = MOBA_HEAD_DIM
    nbp = kmean_ref.shape[1]
    hg = pl.program_id(1)
    i = pl.program_id(2)

    @pl.when(i == 0)
    def _():
        kmean_ref[...] = jnp.zeros_like(kmean_ref)

        def fill(n, c):
            start = pl.multiple_of(n * blk, blk)
            for g in range(heads):
                kn = k_ref[pl.ds(start, blk), g * dh:(g + 1) * dh].astype(F32)
                kmean_ref[g, pl.ds(n, 1), :] = jnp.mean(kn, axis=0, keepdims=True)
                vt_ref[g * dh:(g + 1) * dh, pl.ds(start, blk)] = _transpose_bf16(
                    v_ref[pl.ds(start, blk), g * dh:(g + 1) * dh])
            return c

        lax.fori_loop(0, nb, fill, 0)

    blk_id = lax.broadcasted_iota(jnp.int32, (nbp, blk), 0)
    past = blk_id < i
    key = lax.broadcasted_iota(jnp.int32, (blk, blk), 0)
    qry = lax.broadcasted_iota(jnp.int32, (blk, blk), 1)
    keyf = key.astype(F32)

    qts, colbias, slopes = [], [], []
    for g in range(heads):
        qt = _transpose_bf16(q_ref[:, g * dh:(g + 1) * dh])
        km = kmean_ref[g]
        km_hi = km.astype(BF16)
        km_lo = (km - km_hi.astype(F32)).astype(BF16)
        gate = (jnp.dot(km_hi, qt, preferred_element_type=F32)
                + jnp.dot(km_lo, qt, preferred_element_type=F32))
        gm = jnp.where(past, gate, -jnp.inf)
        rank = jnp.zeros((nbp, blk), jnp.int32)
        for n in range(nb):
            gn = gm[n:n + 1, :]
            beats = (gn > gm) | ((gn == gm) & (blk_id > n))
            rank = rank + beats.astype(jnp.int32)
        sel_ref[g] = jnp.where(past & (rank < min(MOBA_TOPK, nb)), 0.0, -jnp.inf)
        slope = slopes_ref[hg * heads + g]
        qts.append(qt)
        slopes.append(slope)
        colbias.append(keyf * slope)

    def tile(n, carries, own):
        start = pl.multiple_of(n * blk, blk)
        sts = [jnp.dot(k_ref[pl.ds(start, blk), g * dh:(g + 1) * dh], qts[g], preferred_element_type=F32)
               for g in range(heads)]
        for g in range(heads):
            st = sts[g] + colbias[g]
            if own:
                st = jnp.where(key <= qry, st, -jnp.inf)
            else:
                shift = jnp.full((1, blk), (n - i) * blk, jnp.int32).astype(F32) * slopes[g]
                st = st + (sel_ref[g, pl.ds(n, 1), :] + shift)
            sts[g] = st
        vts = [vt_ref[g * dh:(g + 1) * dh, pl.ds(start, blk)] for g in range(heads)]
        return _softmax_steps_t(sts, vts, carries)

    carries = tile(i, tuple(_softmax_init(dh, blk) for _ in range(heads)), True)
    carries = lax.fori_loop(0, i, lambda n, c: tile(n, c, False), carries)
    for g in range(heads):
        _, l, acc = carries[g]
        o_ref[:, g * dh:(g + 1) * dh] = (acc / l).T.astype(o_ref.dtype)


def moba_attention(z, slopes, *, batch, seq, heads):
    blk = MOBA_BLOCK
    w = MOBA_HEAD_DIM * heads
    nb = seq // blk
    qo, ko, vo = 1024 // w, 2048 // w, 3072 // w
    return pl.pallas_call(
        functools.partial(_moba_kernel, nb=nb, heads=heads),
        grid=(batch, MOBA_HEADS // heads, nb),
        in_specs=[
            pl.BlockSpec(memory_space=pltpu.SMEM),
            pl.BlockSpec((None, blk, w), lambda b, h, i: (b, i, qo + h)),
            pl.BlockSpec((None, seq, w), lambda b, h, i: (b, 0, ko + h)),
            pl.BlockSpec((None, seq, w), lambda b, h, i: (b, 0, vo + h)),
        ],
        out_specs=pl.BlockSpec((None, blk, w), lambda b, h, i: (b, i, h)),
        out_shape=jax.ShapeDtypeStruct((batch, seq, MOBA_HEADS * MOBA_HEAD_DIM), BF16),
        scratch_shapes=[
            pltpu.VMEM((w, seq), BF16),
            pltpu.VMEM((heads, max(MOBA_NB_PAD, nb), MOBA_HEAD_DIM), F32),
            pltpu.VMEM((heads, max(MOBA_NB_PAD, nb), blk), F32),
        ],
        compiler_params=_params(("parallel", "parallel", "arbitrary")),
        name="moba_attention",
    )(slopes, z, z, z)


def _band_mask(j, max_dist):
    row = lax.broadcasted_iota(jnp.int32, (BAND, 2 * BAND), 0)
    col = lax.broadcasted_iota(jnp.int32, (BAND, 2 * BAND), 1)
    dist = BAND + row - col
    mask = (dist >= 0) & (dist <= max_dist) & ((col >= BAND) | (j > 0))
    return mask, dist.astype(F32)


def _dil_kernel(q_ref, kp_ref, ko_ref, vp_ref, vo_ref, o_ref, lse_ref, *, group):
    j = pl.program_id(2)
    rate = DIL_RATES[group]
    mask, distf = _band_mask(j, DIL_WINDOWS[group] // rate)
    dh = DIL_HEAD_DIM
    for hh in range(DIL_HEADS_PER_GROUP):
        head = group * DIL_HEADS_PER_GROUP + hh
        slope = 2.0 ** (-8.0 * (head + 1) / (N_DIL * DIL_HEADS_PER_GROUP)) * rate
        sl = slice(hh * dh, (hh + 1) * dh)
        kw = jnp.concatenate([kp_ref[:, sl], ko_ref[:, sl]], axis=0)
        vw = jnp.concatenate([vp_ref[:, sl], vo_ref[:, sl]], axis=0)
        s = _qk(q_ref[:, sl], kw) - slope * distf
        s = jnp.where(mask, s, -jnp.inf)
        m = jnp.max(s, axis=-1, keepdims=True)
        p = jnp.exp(s - m)
        l = jnp.sum(p, axis=-1, keepdims=True)
        acc = jnp.dot(p.astype(BF16), vw, preferred_element_type=F32)
        o_ref[:, sl] = acc / l
        lse_ref[:, sl] = jnp.broadcast_to(m + jnp.log(l), (BAND, dh))


def dilated_group(z, *, group, batch, seq):
    rate = DIL_RATES[group]
    length = seq // rate
    w = DIL_HEADS_PER_GROUP * DIL_HEAD_DIM
    zb = ODD_Z // w
    zv = z.reshape(batch, length, rate * ODD_Z)
    qo, ko, vo = 2 + group, 5 + group, 8 + group

    def spec(off, prev):
        if prev:
            return pl.BlockSpec((None, BAND, w), lambda b, r, j: (b, jnp.maximum(j - 1, 0), r * zb + off))
        return pl.BlockSpec((None, BAND, w), lambda b, r, j: (b, j, r * zb + off))

    out_spec = pl.BlockSpec((None, BAND, w), lambda b, r, j: (b, j, r))
    o, lse = pl.pallas_call(
        functools.partial(_dil_kernel, group=group),
        grid=(batch, rate, length // BAND),
        in_specs=[spec(qo, False), spec(ko, True), spec(ko, False), spec(vo, True), spec(vo, False)],
        out_specs=[out_spec, out_spec],
        out_shape=[jax.ShapeDtypeStruct((batch, length, rate * w), F32)] * 2,
        compiler_params=_params(("parallel", "parallel", "arbitrary")),
        name=f"dilated_group{group}",
    )(zv, zv, zv, zv, zv)
    return o.reshape(batch * seq, w), lse.reshape(batch * seq, w)


def _dil_merge_kernel(o0_ref, l0_ref, o1_ref, l1_ref, o2_ref, l2_ref, out_ref):
    l0, l1, l2 = l0_ref[...], l1_ref[...], l2_ref[...]
    mx = jnp.maximum(jnp.maximum(l0, l1), l2)
    w0, w1, w2 = jnp.exp(l0 - mx), jnp.exp(l1 - mx), jnp.exp(l2 - mx)
    num = o0_ref[...] * w0 + o1_ref[...] * w1 + o2_ref[...] * w2
    out_ref[...] = (num / (w0 + w1 + w2)).astype(out_ref.dtype)


def dilated_merge(parts, *, tm):
    t, w = parts[0].shape
    spec = pl.BlockSpec((tm, w), lambda i: (i, 0))
    return pl.pallas_call(
        _dil_merge_kernel,
        grid=(t // tm,),
        in_specs=[spec] * 6,
        out_specs=spec,
        out_shape=jax.ShapeDtypeStruct((t, w), BF16),
        compiler_params=_params(("parallel",)),
        name="dilated_merge",
    )(*parts)


def _swa_kernel(sinks_ref, q_ref, kp_ref, ko_ref, vp_ref, vo_ref, o_ref):
    j = pl.program_id(1)
    mask, distf = _band_mask(j, SWA_WINDOW - 1)
    heads_per_kv = SWA_Q_HEADS // SWA_KV_HEADS
    half = LANES // 2

    kw = jnp.concatenate([kp_ref[...], ko_ref[...]], axis=0).astype(F32)
    vw = jnp.concatenate([vp_ref[...], vo_ref[...]], axis=0).astype(F32)
    low = lax.broadcasted_iota(jnp.int32, (2 * BAND, LANES), 1) < half

    def halves(x):
        xr = pltpu.roll(x, half, 1)
        even = (jnp.where(low, x, 0.0).astype(BF16), jnp.where(low, xr, 0.0).astype(BF16))
        odd = (jnp.where(low, 0.0, xr).astype(BF16), jnp.where(low, 0.0, x).astype(BF16))
        return even, odd

    k_even, k_odd = halves(kw)
    v_even, v_odd = halves(vw)

    for pair in range(SWA_Q_HEADS // 2):
        kvh = (2 * pair) // heads_per_kv
        slab = q_ref[:, pair * LANES:(pair + 1) * LANES]
        out = None
        for par, (kk, vv) in enumerate(((k_even[kvh], v_even[kvh]), (k_odd[kvh], v_odd[kvh]))):
            head = 2 * pair + par
            slope = 2.0 ** (-8.0 * (head + 1) / SWA_Q_HEADS)
            s = _qk(slab, kk) - slope * distf
            s = jnp.where(mask, s, -jnp.inf)
            m = jnp.max(s, axis=-1, keepdims=True)
            p = jnp.exp(s - m)
            l = jnp.sum(p, axis=-1, keepdims=True)
            acc = jnp.dot(p.astype(BF16), vv, preferred_element_type=F32)
            sk = sinks_ref[head]
            mx = jnp.maximum(m, sk)
            a = jnp.exp(m - mx)
            o = acc * a / (l * a + jnp.exp(sk - mx))
            out = o if out is None else out + o
        o_ref[:, pair * LANES:(pair + 1) * LANES] = out.astype(o_ref.dtype)


def swa_attention(z, sinks, *, batch, seq):
    wq = SWA_Q_HEADS * SWA_HEAD_DIM
    ko, vo = 5632 // LANES, 5760 // LANES

    def kvspec(off, prev):
        if prev:
            return pl.BlockSpec((None, BAND, LANES), lambda b, j: (b, jnp.maximum(j - 1, 0), off))
        return pl.BlockSpec((None, BAND, LANES), lambda b, j: (b, j, off))

    return pl.pallas_call(
        _swa_kernel,
        grid=(batch, seq // BAND),
        in_specs=[
            pl.BlockSpec(memory_space=pltpu.SMEM),
            pl.BlockSpec((None, BAND, wq), lambda b, j: (b, j, 0)),
            kvspec(ko, True), kvspec(ko, False), kvspec(vo, True), kvspec(vo, False),
        ],
        out_specs=pl.BlockSpec((None, BAND, wq), lambda b, j: (b, j, 0)),
        out_shape=jax.ShapeDtypeStruct((batch, seq, wq), BF16),
        compiler_params=_params(("parallel", "arbitrary")),
        name="swa_attention",
    )(sinks, z, z, z, z, z)


def _rope_pad(w):
    half = MLA_ROPE_DIM // 2
    z = jnp.zeros(w.shape[:-1] + (half,), w.dtype)
    return jnp.concatenate([w[..., :half], z, w[..., half:], z], axis=-1)


def _even_weights(w_in, w_uq, w_ukv, w_out):
    hd = MOBA_HEADS * MOBA_HEAD_DIM
    o = 0
    q_lat = w_in[:, o:o + MLA_Q_RANK]; o += MLA_Q_RANK
    kv_lat = w_in[:, o:o + MLA_KV_RANK]; o += MLA_KV_RANK
    k_pe = w_in[:, o:o + MLA_ROPE_DIM]; o += MLA_ROPE_DIM
    qb = w_in[:, o:o + hd] * (MOBA_HEAD_DIM ** -0.5); o += hd
    kb = w_in[:, o:o + hd]; o += hd
    vb = w_in[:, o:o + hd]
    w_in_p = jnp.concatenate([q_lat, kv_lat, qb, kb, vb, _rope_pad(k_pe)], axis=-1).astype(BF16)

    r = w_uq.shape[0]
    uq = w_uq.reshape(r, MLA_HEADS, MLA_NOPE_DIM + MLA_ROPE_DIM) * ((MLA_NOPE_DIM + MLA_ROPE_DIM) ** -0.5)
    wq = jnp.concatenate([uq[..., :MLA_NOPE_DIM], _rope_pad(uq[..., MLA_NOPE_DIM:])], axis=-1)
    wq = wq.reshape(r, MLA_HEADS * MLA_QK_PAD).astype(BF16)
    ukv = w_ukv.reshape(w_ukv.shape[0], MLA_HEADS, MLA_NOPE_DIM + MLA_V_DIM)
    wk = ukv[..., :MLA_NOPE_DIM].reshape(-1, MLA_HEADS * MLA_NOPE_DIM).astype(BF16)
    wv = ukv[..., MLA_NOPE_DIM:].reshape(-1, MLA_HEADS * MLA_V_DIM).astype(BF16)
    na = MLA_HEADS * MLA_V_DIM
    return w_in_p, wq, wk, wv, w_out[:na].astype(BF16), w_out[na:].astype(BF16)


def _odd_weights(w_in, w_out):
    cd = N_DIL * DIL_HEADS_PER_GROUP * DIL_HEAD_DIM
    qd_w = SWA_Q_HEADS * SWA_HEAD_DIM
    kd_w = SWA_KV_HEADS * SWA_HEAD_DIM
    o = 0
    qc = w_in[:, o:o + cd] * (DIL_HEAD_DIM ** -0.5); o += cd
    kc = w_in[:, o:o + cd]; o += cd
    vc = w_in[:, o:o + cd]; o += cd
    qd = w_in[:, o:o + qd_w] * (SWA_HEAD_DIM ** -0.5); o += qd_w
    kd = w_in[:, o:o + kd_w]; o += kd_w
    vd = w_in[:, o:o + kd_w]
    used = qd_w + 3 * cd + 2 * kd_w
    pad = jnp.zeros((w_in.shape[0], ODD_Z - used), w_in.dtype)
    w_in_p = jnp.concatenate([qd, qc, kc, vc, kd, vd, pad], axis=-1).astype(BF16)
    nc = DIL_HEADS_PER_GROUP * DIL_HEAD_DIM
    return w_in_p, w_out[:nc].astype(BF16), w_out[nc:].astype(BF16)


def _rope_tables(seq):
    half = MLA_ROPE_DIM // 2
    inv_freq = ROPE_THETA ** (-jnp.arange(half, dtype=F32) / half)
    ang = jnp.arange(seq, dtype=jnp.int32).astype(F32)[:, None] * inv_freq[None, :]
    c, s = jnp.cos(ang), jnp.sin(ang)
    z = jnp.zeros_like(c)
    return jnp.concatenate([c, z, c, z], axis=-1), jnp.concatenate([-s, z, s, z], axis=-1)


def kernel(x, attn_norm, mlp_norm, w_up, w_down, ev_w_in, ev_q_norm, ev_w_uq, ev_kv_norm, ev_w_ukv, ev_w_out,
           od_w_in, od_sinks, od_w_out, final_norm):
    batch, seq, d = x.shape
    depth = attn_norm.shape[0]
    t = batch * seq
    xt = x.reshape(t, d)
    cos, sin = _rope_tables(seq)
    moba_slopes = jnp.exp2(-8.0 * jnp.arange(1, MOBA_HEADS + 1, dtype=F32) / MOBA_HEADS)
    fg = final_norm.reshape(1, d)

    for layer in range(depth):
        i = layer // 2
        g = attn_norm[layer].reshape(1, d)
        if layer % 2 == 0:
            w_in_p, wq, wk, wv, wo_a, wo_b = _even_weights(ev_w_in[i], ev_w_uq[i], ev_w_ukv[i], ev_w_out[i])
            z = norm_matmul(xt, g, w_in_p, tm=512, tn=1408)
            q, k, v = mla_up(z, ev_q_norm[i].reshape(1, -1), ev_kv_norm[i].reshape(1, -1), wq, wk, wv, cos, sin,
                             seq=seq, tm=256)
            a = mla_attention(q.reshape(batch, seq, -1), k.reshape(batch, seq, -1), v.reshape(batch, seq, -1),
                              batch=batch, seq=seq, blk=256, heads=4)
            b = moba_attention(z.reshape(batch, seq, EVEN_Z), moba_slopes, batch=batch, seq=seq, heads=4)
            xt = proj_residual(xt, a.reshape(t, -1), b.reshape(t, -1), wo_a, wo_b, tm=512)
        else:
            w_in_p, wo_c, wo_d = _odd_weights(od_w_in[i], od_w_out[i])
            z = norm_matmul(xt, g, w_in_p, tm=512, tn=1536)
            z3 = z.reshape(batch, seq, ODD_Z)
            parts = []
            for grp in range(N_DIL):
                parts.extend(dilated_group(z3, group=grp, batch=batch, seq=seq))
            c = dilated_merge(parts, tm=1024)
            dd = swa_attention(z3, od_sinks[i], batch=batch, seq=seq)
            xt = proj_residual(xt, c, dd.reshape(t, -1), wo_c, wo_d, tm=512)
        xt = mlp_residual(xt, mlp_norm[layer].reshape(1, d), w_up[layer].astype(BF16), w_down[layer].astype(BF16), fg,
                          tm=512, tf=1024, final_norm=(layer == depth - 1))
    return xt.reshape(batch, seq, d)
```

```python
import functools

import jax
import jax.numpy as jnp
from jax import lax
from jax.experimental import pallas as pl
from jax.experimental.pallas import tpu as pltpu

F32 = jnp.float32
BF16 = jnp.bfloat16

NORM_EPS = 1e-6
D_FF_MULT = 4

MLA_HEADS = 8
MLA_Q_RANK = 512
MLA_KV_RANK = 512
MLA_NOPE_DIM = 128
MLA_ROPE_DIM = 64
MLA_V_DIM = 128
ROPE_THETA = 10000.0

MOBA_HEADS = 8
MOBA_HEAD_DIM = 128
MOBA_BLOCK = 256
MOBA_TOPK = 3

DIL_WINDOWS = (128, 512, 2048)
DIL_RATES = (1, 4, 16)
N_DIL = 3
DIL_HEADS_PER_GROUP = 4
DIL_HEAD_DIM = 128

SWA_Q_HEADS = 16
SWA_KV_HEADS = 2
SWA_HEAD_DIM = 64
SWA_WINDOW = 128

BAND = 128
DIL_GROUP_HEADS_PER_STEP = (4, 4, 2)

LANES = 128
MLA_QK_PAD = 256
EVEN_Z = 4224
ODD_Z = 6144
VMEM_LIMIT = 56 * 1024 * 1024
LOG2E = 1.4426950408889634


def _params(sem):
    return pltpu.CompilerParams(dimension_semantics=sem, vmem_limit_bytes=VMEM_LIMIT)


def _rms(x, g):
    ms = jnp.mean(x * x, axis=-1, keepdims=True)
    return x * lax.rsqrt(ms + NORM_EPS) * g


def _qk(q, k):
    return lax.dot_general(q, k, (((1,), (1,)), ((), ())), preferred_element_type=F32)


def _norm_matmul_kernel(x_ref, g_ref, w_ref, o_ref, h_ref):
    @pl.when(pl.program_id(1) == 0)
    def _():
        h_ref[...] = _rms(x_ref[...], g_ref[...]).astype(BF16)

    o_ref[...] = jnp.dot(h_ref[...], w_ref[...], preferred_element_type=F32).astype(o_ref.dtype)


def norm_matmul(x, g, w, *, tm, tn):
    m, k = x.shape
    n = w.shape[1]
    return pl.pallas_call(
        _norm_matmul_kernel,
        grid=(m // tm, n // tn),
        in_specs=[
            pl.BlockSpec((tm, k), lambda i, j: (i, 0)),
            pl.BlockSpec((1, k), lambda i, j: (0, 0)),
            pl.BlockSpec((k, tn), lambda i, j: (0, j)),
        ],
        out_specs=pl.BlockSpec((tm, tn), lambda i, j: (i, j)),
        out_shape=jax.ShapeDtypeStruct((m, n), BF16),
        scratch_shapes=[pltpu.VMEM((tm, k), BF16)],
        compiler_params=_params(("parallel", "arbitrary")),
        name="norm_matmul",
    )(x, g, w)


def _mlp_kernel(x_ref, g_ref, wu_ref, wd_ref, fg_ref, o_ref, h_ref, acc_ref, *, final_norm):
    f = pl.program_id(1)

    @pl.when(f == 0)
    def _():
        h_ref[...] = _rms(x_ref[...], g_ref[...]).astype(BF16)
        acc_ref[...] = jnp.zeros_like(acc_ref)

    u = jnp.dot(h_ref[...], wu_ref[...], preferred_element_type=F32)
    u = jnp.maximum(u, 0.0)
    u = u * u
    acc_ref[...] += jnp.dot(u.astype(BF16), wd_ref[...], preferred_element_type=F32)

    @pl.when(f == pl.num_programs(1) - 1)
    def _():
        y = x_ref[...] + acc_ref[...]
        if final_norm:
            y = _rms(y, fg_ref[...])
        o_ref[...] = y


def mlp_residual(x, g, w_up, w_down, final_g, *, tm, tf, final_norm):
    m, d = x.shape
    ff = w_up.shape[1]
    return pl.pallas_call(
        functools.partial(_mlp_kernel, final_norm=final_norm),
        grid=(m // tm, ff // tf),
        in_specs=[
            pl.BlockSpec((tm, d), lambda i, f: (i, 0)),
            pl.BlockSpec((1, d), lambda i, f: (0, 0)),
            pl.BlockSpec((d, tf), lambda i, f: (0, f)),
            pl.BlockSpec((tf, d), lambda i, f: (f, 0)),
            pl.BlockSpec((1, d), lambda i, f: (0, 0)),
        ],
        out_specs=pl.BlockSpec((tm, d), lambda i, f: (i, 0)),
        out_shape=jax.ShapeDtypeStruct((m, d), F32),
        scratch_shapes=[pltpu.VMEM((tm, d), BF16), pltpu.VMEM((tm, d), F32)],
        compiler_params=_params(("parallel", "arbitrary")),
        name="mlp_residual",
    )(x, g, w_up, w_down, final_g)


def _proj_residual_kernel(x_ref, a_ref, b_ref, wa_ref, wb_ref, o_ref):
    y = jnp.dot(a_ref[...], wa_ref[...], preferred_element_type=F32)
    y = y + jnp.dot(b_ref[...], wb_ref[...], preferred_element_type=F32)
    o_ref[...] = x_ref[...] + y


def proj_residual(x, a, b, wa, wb, *, tm):
    m, d = x.shape
    ka, kb = a.shape[1], b.shape[1]
    return pl.pallas_call(
        _proj_residual_kernel,
        grid=(m // tm,),
        in_specs=[
            pl.BlockSpec((tm, d), lambda i: (i, 0)),
            pl.BlockSpec((tm, ka), lambda i: (i, 0)),
            pl.BlockSpec((tm, kb), lambda i: (i, 0)),
            pl.BlockSpec((ka, d), lambda i: (0, 0)),
            pl.BlockSpec((kb, d), lambda i: (0, 0)),
        ],
        out_specs=pl.BlockSpec((tm, d), lambda i: (i, 0)),
        out_shape=jax.ShapeDtypeStruct((m, d), F32),
        compiler_params=_params(("parallel",)),
        name="proj_residual",
    )(x, a, b, wa, wb)


def _mla_up_kernel(ql_ref, kvl_ref, kpe_ref, qn_ref, kvn_ref, wq_ref, wk_ref, wv_ref, cos_ref, sin_ref,
                   q_ref, k_ref, v_ref):
    cos = cos_ref[...]
    sin = sin_ref[...]

    def rope(t):
        return t * cos + pltpu.roll(t, LANES // 2, 1) * sin

    hq = _rms(ql_ref[...].astype(F32), qn_ref[...]).astype(BF16)
    yq = jnp.dot(hq, wq_ref[...], preferred_element_type=F32)
    hkv = _rms(kvl_ref[...].astype(F32), kvn_ref[...]).astype(BF16)
    yk = jnp.dot(hkv, wk_ref[...], preferred_element_type=F32)
    yv = jnp.dot(hkv, wv_ref[...], preferred_element_type=F32)
    krot = rope(kpe_ref[...].astype(F32)).astype(BF16)
    for h in range(MLA_HEADS):
        lo = h * MLA_QK_PAD
        mid = lo + MLA_NOPE_DIM
        hi = lo + MLA_QK_PAD
        q_ref[:, lo:mid] = yq[:, lo:mid].astype(BF16)
        q_ref[:, mid:hi] = rope(yq[:, mid:hi]).astype(BF16)
        k_ref[:, lo:mid] = yk[:, h * MLA_NOPE_DIM:(h + 1) * MLA_NOPE_DIM].astype(BF16)
        k_ref[:, mid:hi] = krot
    v_ref[...] = yv.astype(BF16)


def mla_up(z, qn, kvn, wq, wk, wv, cos, sin, *, seq, tm):
    t = z.shape[0]
    r = MLA_Q_RANK
    nq = MLA_HEADS * MLA_QK_PAD
    nv = MLA_HEADS * MLA_V_DIM
    kpe_blk = (EVEN_Z - LANES) // LANES
    pos_blocks = seq // tm
    return pl.pallas_call(
        _mla_up_kernel,
        grid=(t // tm,),
        in_specs=[
            pl.BlockSpec((tm, r), lambda i: (i, 0)),
            pl.BlockSpec((tm, r), lambda i: (i, 1)),
            pl.BlockSpec((tm, LANES), lambda i: (i, kpe_blk)),
            pl.BlockSpec((1, r), lambda i: (0, 0)),
            pl.BlockSpec((1, r), lambda i: (0, 0)),
            pl.BlockSpec((r, nq), lambda i: (0, 0)),
            pl.BlockSpec((r, nv), lambda i: (0, 0)),
            pl.BlockSpec((r, nv), lambda i: (0, 0)),
            pl.BlockSpec((tm, LANES), lambda i: (i % pos_blocks, 0)),
            pl.BlockSpec((tm, LANES), lambda i: (i % pos_blocks, 0)),
        ],
        out_specs=[
            pl.BlockSpec((tm, nq), lambda i: (i, 0)),
            pl.BlockSpec((tm, nq), lambda i: (i, 0)),
            pl.BlockSpec((tm, nv), lambda i: (i, 0)),
        ],
        out_shape=[
            jax.ShapeDtypeStruct((t, nq), BF16),
            jax.ShapeDtypeStruct((t, nq), BF16),
            jax.ShapeDtypeStruct((t, nv), BF16),
        ],
        compiler_params=_params(("parallel",)),
        name="mla_up",
    )(z, z, z, qn, kvn, wq, wk, wv, cos, sin)


def _softmax_steps_t(sts, vts, carries):
    stats = []
    for st, (m, l, _) in zip(sts, carries):
        m_new = jnp.maximum(m, jnp.max(st, axis=0, keepdims=True))
        p = jnp.exp2(st - m_new)
        alpha = jnp.exp2(m - m_new)
        stats.append((m_new, alpha * l + jnp.sum(p, axis=0, keepdims=True), alpha, p.astype(BF16)))
    return tuple((m_new, l_new, alpha * acc + jnp.dot(vt, p, preferred_element_type=F32))
                 for (m_new, l_new, alpha, p), vt, (_, _, acc) in zip(stats, vts, carries))


def _softmax_init(dv, nq):
    return (jnp.full((1, nq), -jnp.inf, F32), jnp.zeros((1, nq), F32), jnp.zeros((dv, nq), F32))


def _transpose_bf16(x):
    return x.astype(F32).T.astype(BF16)


def _mla_attn_kernel(q_ref, k_ref, v_ref, o_ref, vt_ref, *, blk, heads):
    qi = pl.program_id(2)
    dq, dv = MLA_QK_PAD, MLA_V_DIM

    @pl.when(qi == 0)
    def _():
        def fill(n, c):
            start = pl.multiple_of(n * blk, blk)
            for g in range(heads):
                vt_ref[g * dv:(g + 1) * dv, pl.ds(start, blk)] = _transpose_bf16(
                    v_ref[pl.ds(start, blk), g * dv:(g + 1) * dv])
            return c

        lax.fori_loop(0, k_ref.shape[0] // blk, fill, 0)

    qts = [_transpose_bf16(q_ref[:, g * dq:(g + 1) * dq]) for g in range(heads)]
    key = lax.broadcasted_iota(jnp.int32, (blk, blk), 0)
    qry = lax.broadcasted_iota(jnp.int32, (blk, blk), 1)

    def scores(n):
        start = pl.multiple_of(n * blk, blk)
        return tuple(jnp.dot(k_ref[pl.ds(start, blk), g * dq:(g + 1) * dq], qts[g], preferred_element_type=F32)
                     for g in range(heads))

    def values(n):
        start = pl.multiple_of(n * blk, blk)
        return [vt_ref[g * dv:(g + 1) * dv, pl.ds(start, blk)] for g in range(heads)]

    def body(n, state):
        sts, carries = state
        nxt = scores(n + 1)
        return nxt, _softmax_steps_t(sts, values(n), carries)

    init = tuple(_softmax_init(dv, blk) for _ in range(heads))
    sts, carries = lax.fori_loop(0, qi, body, (scores(0), init))
    sts = [jnp.where(key <= qry, st, -jnp.inf) for st in sts]
    carries = _softmax_steps_t(sts, values(qi), carries)
    for g in range(heads):
        _, l, acc = carries[g]
        o_ref[:, g * dv:(g + 1) * dv] = (acc / l).T.astype(o_ref.dtype)


def mla_attention(q, k, v, *, batch, seq, blk, heads):
    dq = MLA_QK_PAD * heads
    dv = MLA_V_DIM * heads
    return pl.pallas_call(
        functools.partial(_mla_attn_kernel, blk=blk, heads=heads),
        grid=(batch, MLA_HEADS // heads, seq // blk),
        in_specs=[
            pl.BlockSpec((None, blk, dq), lambda b, h, i: (b, i, h)),
            pl.BlockSpec((None, seq, dq), lambda b, h, i: (b, 0, h)),
            pl.BlockSpec((None, seq, dv), lambda b, h, i: (b, 0, h)),
        ],
        out_specs=pl.BlockSpec((None, blk, dv), lambda b, h, i: (b, i, h)),
        out_shape=jax.ShapeDtypeStruct((batch, seq, MLA_HEADS * MLA_V_DIM), BF16),
        scratch_shapes=[pltpu.VMEM((dv, seq), BF16)],
        compiler_params=_params(("parallel", "parallel", "arbitrary")),
        name="mla_attention",
    )(q, k, v)


MOBA_NB_PAD = 16
MOBA_MASKED = -1e30


def _moba_kernel(slopes_ref, q_ref, k_ref, v_ref, o_ref, vt_ref, kmean_ref, sel_ref, *, nb, heads):
    blk = MOBA_BLOCK
    dh = MOBA_HEAD_DIM
    nbp = kmean_ref.shape[1]
    hg = pl.program_id(1)
    i = pl.program_id(2)

    @pl.when(i == 0)
    def _():
        kmean_ref[...] = jnp.zeros_like(kmean_ref)

        def fill(n, c):
            start = pl.multiple_of(n * blk, blk)
            for g in range(heads):
                kn = k_ref[pl.ds(start, blk), g * dh:(g + 1) * dh].astype(F32)
                kmean_ref[g, pl.ds(n, 1), :] = jnp.mean(kn, axis=0, keepdims=True)
                vt_ref[g * dh:(g + 1) * dh, pl.ds(start, blk)] = _transpose_bf16(
                    v_ref[pl.ds(start, blk), g * dh:(g + 1) * dh])
            return c

        lax.fori_loop(0, nb, fill, 0)

    blk_id = lax.broadcasted_iota(jnp.int32, (nbp, blk), 0)
    past = blk_id < i
    key = lax.broadcasted_iota(jnp.int32, (blk, blk), 0)
    qry = lax.broadcasted_iota(jnp.int32, (blk, blk), 1)
    keyf = key.astype(F32)

    qts, colbias, slopes = [], [], []
    for g in range(heads):
        qt = _transpose_bf16(q_ref[:, g * dh:(g + 1) * dh])
        km = kmean_ref[g]
        km_hi = km.astype(BF16)
        km_lo = (km - km_hi.astype(F32)).astype(BF16)
        gate = (jnp.dot(km_hi, qt, preferred_element_type=F32)
                + jnp.dot(km_lo, qt, preferred_element_type=F32))
        gm = jnp.where(past, gate, -jnp.inf)
        rank = jnp.zeros((nbp, blk), jnp.int32)
        for n in range(nb):
            gn = gm[n:n + 1, :]
            beats = (gn > gm) | ((gn == gm) & (blk_id > n))
            rank = rank + beats.astype(jnp.int32)
        sel_ref[g] = jnp.where(past & (rank < min(MOBA_TOPK, nb)), 0.0, MOBA_MASKED)
        slope = slopes_ref[hg * heads + g]
        qts.append(qt)
        slopes.append(slope)
        colbias.append(keyf * slope)

    def scores(n):
        start = pl.multiple_of(n * blk, blk)
        return tuple(jnp.dot(k_ref[pl.ds(start, blk), g * dh:(g + 1) * dh], qts[g], preferred_element_type=F32)
                     for g in range(heads))

    def values(n):
        start = pl.multiple_of(n * blk, blk)
        return [vt_ref[g * dh:(g + 1) * dh, pl.ds(start, blk)] for g in range(heads)]

    def body(n, state):
        sts, carries = state
        nxt = scores(n + 1)
        gated = []
        for g in range(heads):
            shift = jnp.full((1, blk), (n - i) * blk, jnp.int32).astype(F32) * slopes[g]
            gated.append(sts[g] + colbias[g] + (sel_ref[g, pl.ds(n, 1), :] + shift))
        return nxt, _softmax_steps_t(gated, values(n), carries)

    init = tuple(_softmax_init(dh, blk) for _ in range(heads))
    sts, carries = lax.fori_loop(0, i, body, (scores(0), init))
    own = [jnp.where(key <= qry, sts[g] + colbias[g], -jnp.inf) for g in range(heads)]
    carries = _softmax_steps_t(own, values(i), carries)
    for g in range(heads):
        _, l, acc = carries[g]
        o_ref[:, g * dh:(g + 1) * dh] = (acc / l).T.astype(o_ref.dtype)


def moba_attention(z, slopes, *, batch, seq, heads):
    blk = MOBA_BLOCK
    w = MOBA_HEAD_DIM * heads
    nb = seq // blk
    qo, ko, vo = 1024 // w, 2048 // w, 3072 // w
    return pl.pallas_call(
        functools.partial(_moba_kernel, nb=nb, heads=heads),
        grid=(batch, MOBA_HEADS // heads, nb),
        in_specs=[
            pl.BlockSpec(memory_space=pltpu.SMEM),
            pl.BlockSpec((None, blk, w), lambda b, h, i: (b, i, qo + h)),
            pl.BlockSpec((None, seq, w), lambda b, h, i: (b, 0, ko + h)),
            pl.BlockSpec((None, seq, w), lambda b, h, i: (b, 0, vo + h)),
        ],
        out_specs=pl.BlockSpec((None, blk, w), lambda b, h, i: (b, i, h)),
        out_shape=jax.ShapeDtypeStruct((batch, seq, MOBA_HEADS * MOBA_HEAD_DIM), BF16),
        scratch_shapes=[
            pltpu.VMEM((w, seq), BF16),
            pltpu.VMEM((heads, max(MOBA_NB_PAD, nb), MOBA_HEAD_DIM), F32),
            pltpu.VMEM((heads, max(MOBA_NB_PAD, nb), blk), F32),
        ],
        compiler_params=_params(("parallel", "parallel", "arbitrary")),
        name="moba_attention",
    )(slopes, z, z, z)


def _band_mask(j, max_dist):
    row = lax.broadcasted_iota(jnp.int32, (BAND, 2 * BAND), 0)
    col = lax.broadcasted_iota(jnp.int32, (BAND, 2 * BAND), 1)
    dist = BAND + row - col
    mask = (dist >= 0) & (dist <= max_dist) & ((col >= BAND) | (j > 0))
    return mask, dist.astype(F32)


def _dil_kernel(slopes_ref, q_ref, k_ref, v_ref, o_ref, lse_ref, stage_ref, qd_ref, kd_ref, vd_ref, *, group):
    hb = pl.program_id(1)
    j = pl.program_id(2)
    rate = DIL_RATES[group]
    dh = DIL_HEAD_DIM
    nh = q_ref.shape[-1] // dh
    slot = j % 2
    prev = 1 - slot

    @pl.when(j == 0)
    def _():
        kd_ref[1] = jnp.zeros(kd_ref.shape[1:], BF16)
        vd_ref[1] = jnp.zeros(vd_ref.shape[1:], BF16)

    def split(src_ref, store):
        if rate == 1:
            store(0, src_ref[...])
            return
        for hh in range(nh):
            stage_ref[0, hh] = src_ref[:, hh * dh:(hh + 1) * dh].astype(F32)
        for r in range(rate):
            parts = [stage_ref[0, hh, pl.ds(r, BAND, stride=rate), :].astype(BF16) for hh in range(nh)]
            store(r, jnp.concatenate(parts, axis=1))

    def store_q(r, x):
        qd_ref[r] = x

    def store_k(r, x):
        kd_ref[slot, r] = x

    def store_v(r, x):
        vd_ref[slot, r] = x

    split(q_ref, store_q)
    split(k_ref, store_k)
    split(v_ref, store_v)

    mask, distf = _band_mask(j, DIL_WINDOWS[group] // rate)
    biases = []
    for hh in range(nh):
        slope = slopes_ref[group * DIL_HEADS_PER_GROUP + hb * nh + hh] * rate
        biases.append(jnp.where(mask, -slope * distf, -jnp.inf))

    def sub(r, c):
        q = qd_ref[r]
        kw = jnp.concatenate([kd_ref[prev, r], kd_ref[slot, r]], axis=0)
        vw = jnp.concatenate([vd_ref[prev, r], vd_ref[slot, r]], axis=0)
        sls = [slice(hh * dh, (hh + 1) * dh) for hh in range(nh)]
        ss = [_qk(q[:, sl], kw[:, sl]) + biases[hh] for hh, sl in enumerate(sls)]
        stats = []
        for s in ss:
            m = jnp.max(s, axis=-1, keepdims=True)
            p = jnp.exp(s - m)
            stats.append((m, jnp.sum(p, axis=-1, keepdims=True), p.astype(BF16)))
        for hh, ((m, l, p), sl) in enumerate(zip(stats, sls)):
            out = jnp.dot(p, vw[:, sl], preferred_element_type=F32) / l
            lse = jnp.broadcast_to(m + jnp.log(l), (BAND, dh))
            if rate == 1:
                o_ref[:, sl] = out
                lse_ref[:, sl] = lse
            else:
                stage_ref[0, hh, pl.ds(r, BAND, stride=rate), :] = out
                stage_ref[1, hh, pl.ds(r, BAND, stride=rate), :] = lse
        return c

    lax.fori_loop(0, rate, sub, 0)
    if rate > 1:
        for hh in range(nh):
            o_ref[:, hh * dh:(hh + 1) * dh] = stage_ref[0, hh]
            lse_ref[:, hh * dh:(hh + 1) * dh] = stage_ref[1, hh]


def dilated_group(z, slopes, *, group, batch, seq, heads):
    rate = DIL_RATES[group]
    tile = BAND * rate
    gw = DIL_HEADS_PER_GROUP * DIL_HEAD_DIM
    w = heads * DIL_HEAD_DIM
    nhb = DIL_HEADS_PER_GROUP // heads
    qo, ko, vo = ((2 + group) * gw // w, (5 + group) * gw // w, (8 + group) * gw // w)

    def spec(off):
        return pl.BlockSpec((None, tile, w), lambda b, h, j: (b, j, off + h))

    out_spec = pl.BlockSpec((None, tile, w), lambda b, h, j: (b, j, h))
    o, lse = pl.pallas_call(
        functools.partial(_dil_kernel, group=group),
        grid=(batch, nhb, seq // tile),
        in_specs=[pl.BlockSpec(memory_space=pltpu.SMEM), spec(qo), spec(ko), spec(vo)],
        out_specs=[out_spec, out_spec],
        out_shape=[jax.ShapeDtypeStruct((batch, seq, gw), F32)] * 2,
        scratch_shapes=[
            pltpu.VMEM((2, heads, tile, DIL_HEAD_DIM), F32),
            pltpu.VMEM((rate, BAND, w), BF16),
            pltpu.VMEM((2, rate, BAND, w), BF16),
            pltpu.VMEM((2, rate, BAND, w), BF16),
        ],
        compiler_params=_params(("parallel", "parallel", "arbitrary")),
        name=f"dilated_group{group}",
    )(slopes, z, z, z)
    return o.reshape(batch * seq, gw), lse.reshape(batch * seq, gw)


def _dil_merge_kernel(o0_ref, l0_ref, o1_ref, l1_ref, o2_ref, l2_ref, out_ref):
    l0, l1, l2 = l0_ref[...], l1_ref[...], l2_ref[...]
    mx = jnp.maximum(jnp.maximum(l0, l1), l2)
    w0, w1, w2 = jnp.exp(l0 - mx), jnp.exp(l1 - mx), jnp.exp(l2 - mx)
    num = o0_ref[...] * w0 + o1_ref[...] * w1 + o2_ref[...] * w2
    out_ref[...] = (num / (w0 + w1 + w2)).astype(out_ref.dtype)


def dilated_merge(parts, *, tm):
    t, w = parts[0].shape
    spec = pl.BlockSpec((tm, w), lambda i: (i, 0))
    return pl.pallas_call(
        _dil_merge_kernel,
        grid=(t // tm,),
        in_specs=[spec] * 6,
        out_specs=spec,
        out_shape=jax.ShapeDtypeStruct((t, w), BF16),
        compiler_params=_params(("parallel",)),
        name="dilated_merge",
    )(*parts)


def _swa_kernel(sinks_ref, q_ref, kp_ref, ko_ref, vp_ref, vo_ref, o_ref):
    j = pl.program_id(1)
    mask, distf = _band_mask(j, SWA_WINDOW - 1)
    heads_per_kv = SWA_Q_HEADS // SWA_KV_HEADS
    half = LANES // 2

    kw = jnp.concatenate([kp_ref[...], ko_ref[...]], axis=0).astype(F32)
    vw = jnp.concatenate([vp_ref[...], vo_ref[...]], axis=0).astype(F32)
    low = lax.broadcasted_iota(jnp.int32, (2 * BAND, LANES), 1) < half

    def halves(x):
        xr = pltpu.roll(x, half, 1)
        even = (jnp.where(low, x, 0.0).astype(BF16), jnp.where(low, xr, 0.0).astype(BF16))
        odd = (jnp.where(low, 0.0, xr).astype(BF16), jnp.where(low, 0.0, x).astype(BF16))
        return even, odd

    k_even, k_odd = halves(kw)
    v_even, v_odd = halves(vw)

    for pair in range(SWA_Q_HEADS // 2):
        kvh = (2 * pair) // heads_per_kv
        slab = q_ref[:, pair * LANES:(pair + 1) * LANES]
        out = None
        for par, (kk, vv) in enumerate(((k_even[kvh], v_even[kvh]), (k_odd[kvh], v_odd[kvh]))):
            head = 2 * pair + par
            slope = 2.0 ** (-8.0 * (head + 1) / SWA_Q_HEADS)
            s = _qk(slab, kk) - slope * distf
            s = jnp.where(mask, s, -jnp.inf)
            m = jnp.max(s, axis=-1, keepdims=True)
            p = jnp.exp(s - m)
            l = jnp.sum(p, axis=-1, keepdims=True)
            acc = jnp.dot(p.astype(BF16), vv, preferred_element_type=F32)
            sk = sinks_ref[head]
            mx = jnp.maximum(m, sk)
            a = jnp.exp(m - mx)
            o = acc * a / (l * a + jnp.exp(sk - mx))
            out = o if out is None else out + o
        o_ref[:, pair * LANES:(pair + 1) * LANES] = out.astype(o_ref.dtype)


def swa_attention(z, sinks, *, batch, seq):
    wq = SWA_Q_HEADS * SWA_HEAD_DIM
    ko, vo = 5632 // LANES, 5760 // LANES

    def kvspec(off, prev):
        if prev:
            return pl.BlockSpec((None, BAND, LANES), lambda b, j: (b, jnp.maximum(j - 1, 0), off))
        return pl.BlockSpec((None, BAND, LANES), lambda b, j: (b, j, off))

    return pl.pallas_call(
        _swa_kernel,
        grid=(batch, seq // BAND),
        in_specs=[
            pl.BlockSpec(memory_space=pltpu.SMEM),
            pl.BlockSpec((None, BAND, wq), lambda b, j: (b, j, 0)),
            kvspec(ko, True), kvspec(ko, False), kvspec(vo, True), kvspec(vo, False),
        ],
        out_specs=pl.BlockSpec((None, BAND, wq), lambda b, j: (b, j, 0)),
        out_shape=jax.ShapeDtypeStruct((batch, seq, wq), BF16),
        compiler_params=_params(("parallel", "arbitrary")),
        name="swa_attention",
    )(sinks, z, z, z, z, z)


def _rope_pad(w):
    half = MLA_ROPE_DIM // 2
    z = jnp.zeros(w.shape[:-1] + (half,), w.dtype)
    return jnp.concatenate([w[..., :half], z, w[..., half:], z], axis=-1)


def _even_weights(w_in, w_uq, w_ukv, w_out):
    hd = MOBA_HEADS * MOBA_HEAD_DIM
    o = 0
    q_lat = w_in[:, o:o + MLA_Q_RANK]; o += MLA_Q_RANK
    kv_lat = w_in[:, o:o + MLA_KV_RANK]; o += MLA_KV_RANK
    k_pe = w_in[:, o:o + MLA_ROPE_DIM]; o += MLA_ROPE_DIM
    qb = w_in[:, o:o + hd] * (MOBA_HEAD_DIM ** -0.5 * LOG2E); o += hd
    kb = w_in[:, o:o + hd]; o += hd
    vb = w_in[:, o:o + hd]
    w_in_p = jnp.concatenate([q_lat, kv_lat, qb, kb, vb, _rope_pad(k_pe)], axis=-1).astype(BF16)

    r = w_uq.shape[0]
    uq = w_uq.reshape(r, MLA_HEADS, MLA_NOPE_DIM + MLA_ROPE_DIM) * ((MLA_NOPE_DIM + MLA_ROPE_DIM) ** -0.5 * LOG2E)
    wq = jnp.concatenate([uq[..., :MLA_NOPE_DIM], _rope_pad(uq[..., MLA_NOPE_DIM:])], axis=-1)
    wq = wq.reshape(r, MLA_HEADS * MLA_QK_PAD).astype(BF16)
    ukv = w_ukv.reshape(w_ukv.shape[0], MLA_HEADS, MLA_NOPE_DIM + MLA_V_DIM)
    wk = ukv[..., :MLA_NOPE_DIM].reshape(-1, MLA_HEADS * MLA_NOPE_DIM).astype(BF16)
    wv = ukv[..., MLA_NOPE_DIM:].reshape(-1, MLA_HEADS * MLA_V_DIM).astype(BF16)
    na = MLA_HEADS * MLA_V_DIM
    return w_in_p, wq, wk, wv, w_out[:na].astype(BF16), w_out[na:].astype(BF16)


def _odd_weights(w_in, w_out):
    cd = N_DIL * DIL_HEADS_PER_GROUP * DIL_HEAD_DIM
    qd_w = SWA_Q_HEADS * SWA_HEAD_DIM
    kd_w = SWA_KV_HEADS * SWA_HEAD_DIM
    o = 0
    qc = w_in[:, o:o + cd] * (DIL_HEAD_DIM ** -0.5); o += cd
    kc = w_in[:, o:o + cd]; o += cd
    vc = w_in[:, o:o + cd]; o += cd
    qd = w_in[:, o:o + qd_w] * (SWA_HEAD_DIM ** -0.5); o += qd_w
    kd = w_in[:, o:o + kd_w]; o += kd_w
    vd = w_in[:, o:o + kd_w]
    used = qd_w + 3 * cd + 2 * kd_w
    pad = jnp.zeros((w_in.shape[0], ODD_Z - used), w_in.dtype)
    w_in_p = jnp.concatenate([qd, qc, kc, vc, kd, vd, pad], axis=-1).astype(BF16)
    nc = DIL_HEADS_PER_GROUP * DIL_HEAD_DIM
    return w_in_p, w_out[:nc].astype(BF16), w_out[nc:].astype(BF16)


def _rope_tables(seq):
    half = MLA_ROPE_DIM // 2
    inv_freq = ROPE_THETA ** (-jnp.arange(half, dtype=F32) / half)
    ang = jnp.arange(seq, dtype=jnp.int32).astype(F32)[:, None] * inv_freq[None, :]
    c, s = jnp.cos(ang), jnp.sin(ang)
    z = jnp.zeros_like(c)
    return jnp.concatenate([c, z, c, z], axis=-1), jnp.concatenate([-s, z, s, z], axis=-1)


def kernel(x, attn_norm, mlp_norm, w_up, w_down, ev_w_in, ev_q_norm, ev_w_uq, ev_kv_norm, ev_w_ukv, ev_w_out,
           od_w_in, od_sinks, od_w_out, final_norm):
    batch, seq, d = x.shape
    depth = attn_norm.shape[0]
    t = batch * seq
    xt = x.reshape(t, d)
    cos, sin = _rope_tables(seq)
    moba_slopes = jnp.exp2(-8.0 * jnp.arange(1, MOBA_HEADS + 1, dtype=F32) / MOBA_HEADS) * LOG2E
    n_dil_heads = N_DIL * DIL_HEADS_PER_GROUP
    dil_slopes = jnp.exp2(-8.0 * jnp.arange(1, n_dil_heads + 1, dtype=F32) / n_dil_heads)
    fg = final_norm.reshape(1, d)

    for layer in range(depth):
        i = layer // 2
        g = attn_norm[layer].reshape(1, d)
        if layer % 2 == 0:
            w_in_p, wq, wk, wv, wo_a, wo_b = _even_weights(ev_w_in[i], ev_w_uq[i], ev_w_ukv[i], ev_w_out[i])
            z = norm_matmul(xt, g, w_in_p, tm=512, tn=1408)
            q, k, v = mla_up(z, ev_q_norm[i].reshape(1, -1), ev_kv_norm[i].reshape(1, -1), wq, wk, wv, cos, sin,
                             seq=seq, tm=256)
            a = mla_attention(q.reshape(batch, seq, -1), k.reshape(batch, seq, -1), v.reshape(batch, seq, -1),
                              batch=batch, seq=seq, blk=256, heads=4)
            b = moba_attention(z.reshape(batch, seq, EVEN_Z), moba_slopes, batch=batch, seq=seq, heads=4)
            xt = proj_residual(xt, a.reshape(t, -1), b.reshape(t, -1), wo_a, wo_b, tm=512)
        else:
            w_in_p, wo_c, wo_d = _odd_weights(od_w_in[i], od_w_out[i])
            z = norm_matmul(xt, g, w_in_p, tm=512, tn=1536)
            z3 = z.reshape(batch, seq, ODD_Z)
            parts = []
            for grp in range(N_DIL):
                parts.extend(dilated_group(z3, dil_slopes, group=grp, batch=batch, seq=seq,
                                           heads=DIL_GROUP_HEADS_PER_STEP[grp]))
            c = dilated_merge(parts, tm=1024)
            dd = swa_attention(z3, od_sinks[i], batch=batch, seq=seq)
            xt = proj_residual(xt, c, dd.reshape(t, -1), wo_c, wo_d, tm=512)
        xt = mlp_residual(xt, mlp_norm[layer].reshape(1, d), w_up[layer].astype(BF16), w_down[layer].astype(BF16), fg,
                          tm=512, tf=1024, final_norm=(layer == depth - 1))
    return xt.reshape(batch, seq, d)
```

```python
import functools

import jax
import jax.numpy as jnp
from jax import lax
from jax.experimental import pallas as pl
from jax.experimental.pallas import tpu as pltpu

F32 = jnp.float32
BF16 = jnp.bfloat16

NORM_EPS = 1e-6
D_FF_MULT = 4

MLA_HEADS = 8
MLA_Q_RANK = 512
MLA_KV_RANK = 512
MLA_NOPE_DIM = 128
MLA_ROPE_DIM = 64
MLA_V_DIM = 128
ROPE_THETA = 10000.0

MOBA_HEADS = 8
MOBA_HEAD_DIM = 128
MOBA_BLOCK = 256
MOBA_TOPK = 3

DIL_WINDOWS = (128, 512, 2048)
DIL_RATES = (1, 4, 16)
N_DIL = 3
DIL_HEADS_PER_GROUP = 4
DIL_HEAD_DIM = 128

SWA_Q_HEADS = 16
SWA_KV_HEADS = 2
SWA_HEAD_DIM = 64
SWA_WINDOW = 128

BAND = 128
DIL_GROUP_HEADS_PER_STEP = (4, 4, 2)

LANES = 128
MLA_QK_PAD = 256
EVEN_Z = 4224
ODD_Z = 6144
VMEM_LIMIT = 56 * 1024 * 1024
LOG2E = 1.4426950408889634


def _params(sem):
    return pltpu.CompilerParams(dimension_semantics=sem, vmem_limit_bytes=VMEM_LIMIT)


def _rms(x, g):
    ms = jnp.mean(x * x, axis=-1, keepdims=True)
    return x * lax.rsqrt(ms + NORM_EPS) * g


def _qk(q, k):
    return lax.dot_general(q, k, (((1,), (1,)), ((), ())), preferred_element_type=F32)


def _norm_matmul_kernel(x_ref, g_ref, w_ref, o_ref):
    h = _rms(x_ref[...], g_ref[...]).astype(BF16)
    o_ref[...] = jnp.dot(h, w_ref[...], preferred_element_type=F32).astype(o_ref.dtype)


def norm_matmul(x, g, w, *, tm):
    m, k = x.shape
    n = w.shape[1]
    return pl.pallas_call(
        _norm_matmul_kernel,
        grid=(m // tm,),
        in_specs=[
            pl.BlockSpec((tm, k), lambda i: (i, 0)),
            pl.BlockSpec((1, k), lambda i: (0, 0)),
            pl.BlockSpec((k, n), lambda i: (0, 0), pipeline_mode=pl.Buffered(1)),
        ],
        out_specs=pl.BlockSpec((tm, n), lambda i: (i, 0)),
        out_shape=jax.ShapeDtypeStruct((m, n), BF16),
        compiler_params=_params(("parallel",)),
        name="norm_matmul",
    )(x, g, w)


def _mlp_kernel(x_ref, g_ref, wu_ref, wd_ref, fg_ref, o_ref, h_ref, acc_ref, *, final_norm):
    f = pl.program_id(1)

    @pl.when(f == 0)
    def _():
        h_ref[...] = _rms(x_ref[...], g_ref[...]).astype(BF16)
        acc_ref[...] = jnp.zeros_like(acc_ref)

    u = jnp.dot(h_ref[...], wu_ref[...], preferred_element_type=F32)
    u = jnp.maximum(u, 0.0)
    u = u * u
    acc_ref[...] += jnp.dot(u.astype(BF16), wd_ref[...], preferred_element_type=F32)

    @pl.when(f == pl.num_programs(1) - 1)
    def _():
        y = x_ref[...] + acc_ref[...]
        if final_norm:
            y = _rms(y, fg_ref[...])
        o_ref[...] = y


def mlp_residual(x, g, w_up, w_down, final_g, *, tm, tf, final_norm):
    m, d = x.shape
    ff = w_up.shape[1]
    return pl.pallas_call(
        functools.partial(_mlp_kernel, final_norm=final_norm),
        grid=(m // tm, ff // tf),
        in_specs=[
            pl.BlockSpec((tm, d), lambda i, f: (i, 0)),
            pl.BlockSpec((1, d), lambda i, f: (0, 0)),
            pl.BlockSpec((d, tf), lambda i, f: (0, f)),
            pl.BlockSpec((tf, d), lambda i, f: (f, 0)),
            pl.BlockSpec((1, d), lambda i, f: (0, 0)),
        ],
        out_specs=pl.BlockSpec((tm, d), lambda i, f: (i, 0)),
        out_shape=jax.ShapeDtypeStruct((m, d), F32),
        scratch_shapes=[pltpu.VMEM((tm, d), BF16), pltpu.VMEM((tm, d), F32)],
        compiler_params=_params(("parallel", "arbitrary")),
        name="mlp_residual",
    )(x, g, w_up, w_down, final_g)


def _proj_residual_kernel(x_ref, a_ref, b_ref, wa_ref, wb_ref, o_ref):
    y = jnp.dot(a_ref[...], wa_ref[...], preferred_element_type=F32)
    y = y + jnp.dot(b_ref[...], wb_ref[...], preferred_element_type=F32)
    o_ref[...] = x_ref[...] + y


def proj_residual(x, a, b, wa, wb, *, tm):
    m, d = x.shape
    ka, kb = a.shape[1], b.shape[1]
    return pl.pallas_call(
        _proj_residual_kernel,
        grid=(m // tm,),
        in_specs=[
            pl.BlockSpec((tm, d), lambda i: (i, 0)),
            pl.BlockSpec((tm, ka), lambda i: (i, 0)),
            pl.BlockSpec((tm, kb), lambda i: (i, 0)),
            pl.BlockSpec((ka, d), lambda i: (0, 0)),
            pl.BlockSpec((kb, d), lambda i: (0, 0)),
        ],
        out_specs=pl.BlockSpec((tm, d), lambda i: (i, 0)),
        out_shape=jax.ShapeDtypeStruct((m, d), F32),
        compiler_params=_params(("parallel",)),
        name="proj_residual",
    )(x, a, b, wa, wb)


def _mla_up_kernel(ql_ref, kvl_ref, kpe_ref, qn_ref, kvn_ref, wq_ref, wk_ref, wv_ref, cos_ref, sin_ref,
                   q_ref, k_ref, v_ref):
    cos = cos_ref[...]
    sin = sin_ref[...]

    def rope(t):
        return t * cos + pltpu.roll(t, LANES // 2, 1) * sin

    hq = _rms(ql_ref[...].astype(F32), qn_ref[...]).astype(BF16)
    yq = jnp.dot(hq, wq_ref[...], preferred_element_type=F32)
    hkv = _rms(kvl_ref[...].astype(F32), kvn_ref[...]).astype(BF16)
    yk = jnp.dot(hkv, wk_ref[...], preferred_element_type=F32)
    yv = jnp.dot(hkv, wv_ref[...], preferred_element_type=F32)
    krot = rope(kpe_ref[...].astype(F32)).astype(BF16)
    for h in range(MLA_HEADS):
        lo = h * MLA_QK_PAD
        mid = lo + MLA_NOPE_DIM
        hi = lo + MLA_QK_PAD
        q_ref[:, lo:mid] = yq[:, lo:mid].astype(BF16)
        q_ref[:, mid:hi] = rope(yq[:, mid:hi]).astype(BF16)
        k_ref[:, lo:mid] = yk[:, h * MLA_NOPE_DIM:(h + 1) * MLA_NOPE_DIM].astype(BF16)
        k_ref[:, mid:hi] = krot
    v_ref[...] = yv.astype(BF16)


def mla_up(z, qn, kvn, wq, wk, wv, cos, sin, *, seq, tm):
    t = z.shape[0]
    r = MLA_Q_RANK
    nq = MLA_HEADS * MLA_QK_PAD
    nv = MLA_HEADS * MLA_V_DIM
    kpe_blk = (EVEN_Z - LANES) // LANES
    pos_blocks = seq // tm
    return pl.pallas_call(
        _mla_up_kernel,
        grid=(t // tm,),
        in_specs=[
            pl.BlockSpec((tm, r), lambda i: (i, 0)),
            pl.BlockSpec((tm, r), lambda i: (i, 1)),
            pl.BlockSpec((tm, LANES), lambda i: (i, kpe_blk)),
            pl.BlockSpec((1, r), lambda i: (0, 0)),
            pl.BlockSpec((1, r), lambda i: (0, 0)),
            pl.BlockSpec((r, nq), lambda i: (0, 0)),
            pl.BlockSpec((r, nv), lambda i: (0, 0)),
            pl.BlockSpec((r, nv), lambda i: (0, 0)),
            pl.BlockSpec((tm, LANES), lambda i: (i % pos_blocks, 0)),
            pl.BlockSpec((tm, LANES), lambda i: (i % pos_blocks, 0)),
        ],
        out_specs=[
            pl.BlockSpec((tm, nq), lambda i: (i, 0)),
            pl.BlockSpec((tm, nq), lambda i: (i, 0)),
            pl.BlockSpec((tm, nv), lambda i: (i, 0)),
        ],
        out_shape=[
            jax.ShapeDtypeStruct((t, nq), BF16),
            jax.ShapeDtypeStruct((t, nq), BF16),
            jax.ShapeDtypeStruct((t, nv), BF16),
        ],
        compiler_params=_params(("parallel",)),
        name="mla_up",
    )(z, z, z, qn, kvn, wq, wk, wv, cos, sin)


def _softmax_steps_t(sts, vts, carries):
    stats = []
    for st, (m, l, _) in zip(sts, carries):
        m_new = jnp.maximum(m, jnp.max(st, axis=0, keepdims=True))
        p = jnp.exp2(st - m_new)
        alpha = jnp.exp2(m - m_new)
        stats.append((m_new, alpha * l + jnp.sum(p, axis=0, keepdims=True), alpha, p.astype(BF16)))
    return tuple((m_new, l_new, alpha * acc + jnp.dot(vt, p, preferred_element_type=F32))
                 for (m_new, l_new, alpha, p), vt, (_, _, acc) in zip(stats, vts, carries))


def _softmax_init(dv, nq):
    return (jnp.full((1, nq), -jnp.inf, F32), jnp.zeros((1, nq), F32), jnp.zeros((dv, nq), F32))


def _transpose_bf16(x):
    return x.astype(F32).T.astype(BF16)


def _mla_attn_kernel(q_ref, k_ref, v_ref, o_ref, vt_ref, *, blk, heads):
    qi = pl.program_id(2)
    dq, dv = MLA_QK_PAD, MLA_V_DIM

    @pl.when(qi == 0)
    def _():
        def fill(n, c):
            start = pl.multiple_of(n * blk, blk)
            for g in range(heads):
                vt_ref[g * dv:(g + 1) * dv, pl.ds(start, blk)] = _transpose_bf16(
                    v_ref[pl.ds(start, blk), g * dv:(g + 1) * dv])
            return c

        lax.fori_loop(0, k_ref.shape[0] // blk, fill, 0)

    qts = [_transpose_bf16(q_ref[:, g * dq:(g + 1) * dq]) for g in range(heads)]
    tk = 2 * blk

    def tile(start, carries, mask):
        sts = [jnp.dot(k_ref[pl.ds(start, tk), g * dq:(g + 1) * dq], qts[g], preferred_element_type=F32)
               for g in range(heads)]
        if mask is not None:
            sts = [jnp.where(mask, st, -jnp.inf) for st in sts]
        vts = [vt_ref[g * dv:(g + 1) * dv, pl.ds(start, tk)] for g in range(heads)]
        return _softmax_steps_t(sts, vts, carries)

    init = tuple(_softmax_init(dv, blk) for _ in range(heads))
    npairs = qi // 2
    carries = lax.fori_loop(0, npairs, lambda n, c: tile(pl.multiple_of(n * tk, tk), c, None), init)
    start = pl.multiple_of(jnp.maximum(qi - 1, 0) * blk, blk)
    key_pos = start + lax.broadcasted_iota(jnp.int32, (tk, blk), 0)
    qry_pos = qi * blk + lax.broadcasted_iota(jnp.int32, (tk, blk), 1)
    carries = tile(start, carries, (key_pos <= qry_pos) & (key_pos >= npairs * tk))
    for g in range(heads):
        _, l, acc = carries[g]
        o_ref[:, g * dv:(g + 1) * dv] = (acc / l).T.astype(o_ref.dtype)


def mla_attention(q, k, v, *, batch, seq, blk, heads):
    dq = MLA_QK_PAD * heads
    dv = MLA_V_DIM * heads
    return pl.pallas_call(
        functools.partial(_mla_attn_kernel, blk=blk, heads=heads),
        grid=(batch, MLA_HEADS // heads, seq // blk),
        in_specs=[
            pl.BlockSpec((None, blk, dq), lambda b, h, i: (b, i, h)),
            pl.BlockSpec((None, seq, dq), lambda b, h, i: (b, 0, h)),
            pl.BlockSpec((None, seq, dv), lambda b, h, i: (b, 0, h)),
        ],
        out_specs=pl.BlockSpec((None, blk, dv), lambda b, h, i: (b, i, h)),
        out_shape=jax.ShapeDtypeStruct((batch, seq, MLA_HEADS * MLA_V_DIM), BF16),
        scratch_shapes=[pltpu.VMEM((dv, seq), BF16)],
        compiler_params=_params(("parallel", "parallel", "arbitrary")),
        name="mla_attention",
    )(q, k, v)


MOBA_NB_PAD = 16
MOBA_MASKED = -1e30


def _moba_kernel(slopes_ref, q_ref, k_ref, v_ref, o_ref, vt_ref, kmean_ref, sel_ref, *, nb, heads):
    blk = MOBA_BLOCK
    dh = MOBA_HEAD_DIM
    nbp = kmean_ref.shape[1]
    hg = pl.program_id(1)
    i = pl.program_id(2)

    @pl.when(i == 0)
    def _():
        kmean_ref[...] = jnp.zeros_like(kmean_ref)

        def fill(n, c):
            start = pl.multiple_of(n * blk, blk)
            for g in range(heads):
                kn = k_ref[pl.ds(start, blk), g * dh:(g + 1) * dh].astype(F32)
                kmean_ref[g, pl.ds(n, 1), :] = jnp.mean(kn, axis=0, keepdims=True)
                vt_ref[g * dh:(g + 1) * dh, pl.ds(start, blk)] = _transpose_bf16(
                    v_ref[pl.ds(start, blk), g * dh:(g + 1) * dh])
            return c

        lax.fori_loop(0, nb, fill, 0)

    blk_id = lax.broadcasted_iota(jnp.int32, (nbp, blk), 0)
    past = blk_id < i
    tk = 2 * blk
    key = lax.broadcasted_iota(jnp.int32, (tk, blk), 0)
    qry = lax.broadcasted_iota(jnp.int32, (tk, blk), 1)
    keyf = key.astype(F32)

    qts, colbias, slopes = [], [], []
    for g in range(heads):
        qt = _transpose_bf16(q_ref[:, g * dh:(g + 1) * dh])
        km = kmean_ref[g]
        km_hi = km.astype(BF16)
        km_lo = (km - km_hi.astype(F32)).astype(BF16)
        gate = (jnp.dot(km_hi, qt, preferred_element_type=F32)
                + jnp.dot(km_lo, qt, preferred_element_type=F32))
        gm = jnp.where(past, gate, -jnp.inf)
        rank = jnp.zeros((nbp, blk), jnp.int32)
        for n in range(nb):
            gn = gm[n:n + 1, :]
            beats = (gn > gm) | ((gn == gm) & (blk_id > n))
            rank = rank + beats.astype(jnp.int32)
        sel_ref[g] = jnp.where(past & (rank < min(MOBA_TOPK, nb)), 0.0, MOBA_MASKED)
        slope = slopes_ref[hg * heads + g]
        qts.append(qt)
        slopes.append(slope)
        colbias.append(keyf * slope)

    def tile(b0, carries, gates, mask):
        start = pl.multiple_of(b0 * blk, blk)
        sts = [jnp.dot(k_ref[pl.ds(start, tk), g * dh:(g + 1) * dh], qts[g], preferred_element_type=F32)
               for g in range(heads)]
        biased = []
        for g in range(heads):
            shift = jnp.full((1, blk), (b0 - i) * blk, jnp.int32).astype(F32) * slopes[g]
            st = sts[g] + colbias[g]
            st = jnp.concatenate([st[:blk] + (gates[g][0] + shift), st[blk:] + (gates[g][1] + shift)], axis=0)
            biased.append(st if mask is None else jnp.where(mask, st, -jnp.inf))
        vts = [vt_ref[g * dh:(g + 1) * dh, pl.ds(start, tk)] for g in range(heads)]
        return _softmax_steps_t(biased, vts, carries)

    def body(n, carries):
        gates = [(sel_ref[g, pl.ds(2 * n, 1), :], sel_ref[g, pl.ds(2 * n + 1, 1), :]) for g in range(heads)]
        return tile(2 * n, carries, gates, None)

    init = tuple(_softmax_init(dh, blk) for _ in range(heads))
    npairs = i // 2
    carries = lax.fori_loop(0, npairs, body, init)
    b0 = jnp.maximum(i - 1, 0)
    zero = jnp.zeros((1, blk), F32)
    gates = [(jnp.where(i % 2 == 1, sel_ref[g, pl.ds(b0, 1), :], zero), zero) for g in range(heads)]
    mask = (key + b0 * blk <= qry + i * blk) & (key + b0 * blk >= npairs * tk)
    carries = tile(b0, carries, gates, mask)
    for g in range(heads):
        _, l, acc = carries[g]
        o_ref[:, g * dh:(g + 1) * dh] = (acc / l).T.astype(o_ref.dtype)


def moba_attention(z, slopes, *, batch, seq, heads):
    blk = MOBA_BLOCK
    w = MOBA_HEAD_DIM * heads
    nb = seq // blk
    qo, ko, vo = 1024 // w, 2048 // w, 3072 // w
    return pl.pallas_call(
        functools.partial(_moba_kernel, nb=nb, heads=heads),
        grid=(batch, MOBA_HEADS // heads, nb),
        in_specs=[
            pl.BlockSpec(memory_space=pltpu.SMEM),
            pl.BlockSpec((None, blk, w), lambda b, h, i: (b, i, qo + h)),
            pl.BlockSpec((None, seq, w), lambda b, h, i: (b, 0, ko + h)),
            pl.BlockSpec((None, seq, w), lambda b, h, i: (b, 0, vo + h)),
        ],
        out_specs=pl.BlockSpec((None, blk, w), lambda b, h, i: (b, i, h)),
        out_shape=jax.ShapeDtypeStruct((batch, seq, MOBA_HEADS * MOBA_HEAD_DIM), BF16),
        scratch_shapes=[
            pltpu.VMEM((w, seq), BF16),
            pltpu.VMEM((heads, max(MOBA_NB_PAD, nb), MOBA_HEAD_DIM), F32),
            pltpu.VMEM((heads, max(MOBA_NB_PAD, nb), blk), F32),
        ],
        compiler_params=_params(("parallel", "parallel", "arbitrary")),
        name="moba_attention",
    )(slopes, z, z, z)


def _band_mask(j, max_dist):
    row = lax.broadcasted_iota(jnp.int32, (BAND, 2 * BAND), 0)
    col = lax.broadcasted_iota(jnp.int32, (BAND, 2 * BAND), 1)
    dist = BAND + row - col
    mask = (dist >= 0) & (dist <= max_dist) & ((col >= BAND) | (j > 0))
    return mask, dist.astype(F32)


def _dil_kernel(slopes_ref, q_ref, k_ref, v_ref, o_ref, lse_ref, stage_ref, qd_ref, kd_ref, vd_ref, *, group):
    hb = pl.program_id(1)
    j = pl.program_id(2)
    rate = DIL_RATES[group]
    dh = DIL_HEAD_DIM
    nh = q_ref.shape[-1] // dh
    slot = j % 2
    prev = 1 - slot

    @pl.when(j == 0)
    def _():
        kd_ref[1] = jnp.zeros(kd_ref.shape[1:], BF16)
        vd_ref[1] = jnp.zeros(vd_ref.shape[1:], BF16)

    def split(src_ref, store):
        if rate == 1:
            store(0, src_ref[...])
            return
        for hh in range(nh):
            stage_ref[0, hh] = src_ref[:, hh * dh:(hh + 1) * dh].astype(F32)
        for r in range(rate):
            parts = [stage_ref[0, hh, pl.ds(r, BAND, stride=rate), :].astype(BF16) for hh in range(nh)]
            store(r, jnp.concatenate(parts, axis=1))

    def store_q(r, x):
        qd_ref[r] = x

    def store_k(r, x):
        kd_ref[slot, r] = x

    def store_v(r, x):
        vd_ref[slot, r] = x

    split(q_ref, store_q)
    split(k_ref, store_k)
    split(v_ref, store_v)

    mask, distf = _band_mask(j, DIL_WINDOWS[group] // rate)
    biases = []
    for hh in range(nh):
        slope = slopes_ref[group * DIL_HEADS_PER_GROUP + hb * nh + hh] * rate
        biases.append(jnp.where(mask, -slope * distf, -jnp.inf))

    def sub(r, c):
        q = qd_ref[r]
        kw = jnp.concatenate([kd_ref[prev, r], kd_ref[slot, r]], axis=0)
        vw = jnp.concatenate([vd_ref[prev, r], vd_ref[slot, r]], axis=0)
        sls = [slice(hh * dh, (hh + 1) * dh) for hh in range(nh)]
        ss = [_qk(q[:, sl], kw[:, sl]) + biases[hh] for hh, sl in enumerate(sls)]
        stats = []
        for s in ss:
            m = jnp.max(s, axis=-1, keepdims=True)
            p = jnp.exp(s - m)
            stats.append((m, jnp.sum(p, axis=-1, keepdims=True), p.astype(BF16)))
        for hh, ((m, l, p), sl) in enumerate(zip(stats, sls)):
            out = jnp.dot(p, vw[:, sl], preferred_element_type=F32) / l
            lse = jnp.broadcast_to(m + jnp.log(l), (BAND, dh))
            if rate == 1:
                o_ref[:, sl] = out
                lse_ref[:, sl] = lse
            else:
                stage_ref[0, hh, pl.ds(r, BAND, stride=rate), :] = out
                stage_ref[1, hh, pl.ds(r, BAND, stride=rate), :] = lse
        return c

    lax.fori_loop(0, rate, sub, 0)
    if rate > 1:
        for hh in range(nh):
            o_ref[:, hh * dh:(hh + 1) * dh] = stage_ref[0, hh]
            lse_ref[:, hh * dh:(hh + 1) * dh] = stage_ref[1, hh]


def dilated_group(z, slopes, *, group, batch, seq, heads):
    rate = DIL_RATES[group]
    tile = BAND * rate
    gw = DIL_HEADS_PER_GROUP * DIL_HEAD_DIM
    w = heads * DIL_HEAD_DIM
    nhb = DIL_HEADS_PER_GROUP // heads
    qo, ko, vo = ((2 + group) * gw // w, (5 + group) * gw // w, (8 + group) * gw // w)

    def spec(off):
        return pl.BlockSpec((None, tile, w), lambda b, h, j: (b, j, off + h))

    out_spec = pl.BlockSpec((None, tile, w), lambda b, h, j: (b, j, h))
    o, lse = pl.pallas_call(
        functools.partial(_dil_kernel, group=group),
        grid=(batch, nhb, seq // tile),
        in_specs=[pl.BlockSpec(memory_space=pltpu.SMEM), spec(qo), spec(ko), spec(vo)],
        out_specs=[out_spec, out_spec],
        out_shape=[jax.ShapeDtypeStruct((batch, seq, gw), F32)] * 2,
        scratch_shapes=[
            pltpu.VMEM((2, heads, tile, DIL_HEAD_DIM), F32),
            pltpu.VMEM((rate, BAND, w), BF16),
            pltpu.VMEM((2, rate, BAND, w), BF16),
            pltpu.VMEM((2, rate, BAND, w), BF16),
        ],
        compiler_params=_params(("parallel", "parallel", "arbitrary")),
        name=f"dilated_group{group}",
    )(slopes, z, z, z)
    return o.reshape(batch * seq, gw), lse.reshape(batch * seq, gw)


def _dil_merge_kernel(o0_ref, l0_ref, o1_ref, l1_ref, o2_ref, l2_ref, out_ref):
    l0, l1, l2 = l0_ref[...], l1_ref[...], l2_ref[...]
    mx = jnp.maximum(jnp.maximum(l0, l1), l2)
    w0, w1, w2 = jnp.exp(l0 - mx), jnp.exp(l1 - mx), jnp.exp(l2 - mx)
    num = o0_ref[...] * w0 + o1_ref[...] * w1 + o2_ref[...] * w2
    out_ref[...] = (num / (w0 + w1 + w2)).astype(out_ref.dtype)


def dilated_merge(parts, *, tm):
    t, w = parts[0].shape
    spec = pl.BlockSpec((tm, w), lambda i: (i, 0))
    return pl.pallas_call(
        _dil_merge_kernel,
        grid=(t // tm,),
        in_specs=[spec] * 6,
        out_specs=spec,
        out_shape=jax.ShapeDtypeStruct((t, w), BF16),
        compiler_params=_params(("parallel",)),
        name="dilated_merge",
    )(*parts)


def _swa_kernel(sinks_ref, q_ref, kp_ref, ko_ref, vp_ref, vo_ref, o_ref):
    j = pl.program_id(1)
    mask, distf = _band_mask(j, SWA_WINDOW - 1)
    heads_per_kv = SWA_Q_HEADS // SWA_KV_HEADS
    half = LANES // 2

    kw = jnp.concatenate([kp_ref[...], ko_ref[...]], axis=0).astype(F32)
    vw = jnp.concatenate([vp_ref[...], vo_ref[...]], axis=0).astype(F32)
    low = lax.broadcasted_iota(jnp.int32, (2 * BAND, LANES), 1) < half

    def halves(x):
        xr = pltpu.roll(x, half, 1)
        even = (jnp.where(low, x, 0.0).astype(BF16), jnp.where(low, xr, 0.0).astype(BF16))
        odd = (jnp.where(low, 0.0, xr).astype(BF16), jnp.where(low, 0.0, x).astype(BF16))
        return even, odd

    k_even, k_odd = halves(kw)
    v_even, v_odd = halves(vw)

    for pair in range(SWA_Q_HEADS // 2):
        kvh = (2 * pair) // heads_per_kv
        slab = q_ref[:, pair * LANES:(pair + 1) * LANES]
        out = None
        for par, (kk, vv) in enumerate(((k_even[kvh], v_even[kvh]), (k_odd[kvh], v_odd[kvh]))):
            head = 2 * pair + par
            slope = 2.0 ** (-8.0 * (head + 1) / SWA_Q_HEADS)
            s = _qk(slab, kk) - slope * distf
            s = jnp.where(mask, s, -jnp.inf)
            m = jnp.max(s, axis=-1, keepdims=True)
            p = jnp.exp(s - m)
            l = jnp.sum(p, axis=-1, keepdims=True)
            acc = jnp.dot(p.astype(BF16), vv, preferred_element_type=F32)
            sk = sinks_ref[head]
            mx = jnp.maximum(m, sk)
            a = jnp.exp(m - mx)
            o = acc * a / (l * a + jnp.exp(sk - mx))
            out = o if out is None else out + o
        o_ref[:, pair * LANES:(pair + 1) * LANES] = out.astype(o_ref.dtype)


def swa_attention(z, sinks, *, batch, seq):
    wq = SWA_Q_HEADS * SWA_HEAD_DIM
    ko, vo = 5632 // LANES, 5760 // LANES

    def kvspec(off, prev):
        if prev:
            return pl.BlockSpec((None, BAND, LANES), lambda b, j: (b, jnp.maximum(j - 1, 0), off))
        return pl.BlockSpec((None, BAND, LANES), lambda b, j: (b, j, off))

    return pl.pallas_call(
        _swa_kernel,
        grid=(batch, seq // BAND),
        in_specs=[
            pl.BlockSpec(memory_space=pltpu.SMEM),
            pl.BlockSpec((None, BAND, wq), lambda b, j: (b, j, 0)),
            kvspec(ko, True), kvspec(ko, False), kvspec(vo, True), kvspec(vo, False),
        ],
        out_specs=pl.BlockSpec((None, BAND, wq), lambda b, j: (b, j, 0)),
        out_shape=jax.ShapeDtypeStruct((batch, seq, wq), BF16),
        compiler_params=_params(("parallel", "arbitrary")),
        name="swa_attention",
    )(sinks, z, z, z, z, z)


def _rope_pad(w):
    half = MLA_ROPE_DIM // 2
    z = jnp.zeros(w.shape[:-1] + (half,), w.dtype)
    return jnp.concatenate([w[..., :half], z, w[..., half:], z], axis=-1)


def _even_weights(w_in, w_uq, w_ukv, w_out):
    hd = MOBA_HEADS * MOBA_HEAD_DIM
    o = 0
    q_lat = w_in[:, o:o + MLA_Q_RANK]; o += MLA_Q_RANK
    kv_lat = w_in[:, o:o + MLA_KV_RANK]; o += MLA_KV_RANK
    k_pe = w_in[:, o:o + MLA_ROPE_DIM]; o += MLA_ROPE_DIM
    qb = w_in[:, o:o + hd] * (MOBA_HEAD_DIM ** -0.5 * LOG2E); o += hd
    kb = w_in[:, o:o + hd]; o += hd
    vb = w_in[:, o:o + hd]
    w_in_p = jnp.concatenate([q_lat, kv_lat, qb, kb, vb, _rope_pad(k_pe)], axis=-1).astype(BF16)

    r = w_uq.shape[0]
    uq = w_uq.reshape(r, MLA_HEADS, MLA_NOPE_DIM + MLA_ROPE_DIM) * ((MLA_NOPE_DIM + MLA_ROPE_DIM) ** -0.5 * LOG2E)
    wq = jnp.concatenate([uq[..., :MLA_NOPE_DIM], _rope_pad(uq[..., MLA_NOPE_DIM:])], axis=-1)
    wq = wq.reshape(r, MLA_HEADS * MLA_QK_PAD).astype(BF16)
    ukv = w_ukv.reshape(w_ukv.shape[0], MLA_HEADS, MLA_NOPE_DIM + MLA_V_DIM)
    wk = ukv[..., :MLA_NOPE_DIM].reshape(-1, MLA_HEADS * MLA_NOPE_DIM).astype(BF16)
    wv = ukv[..., MLA_NOPE_DIM:].reshape(-1, MLA_HEADS * MLA_V_DIM).astype(BF16)
    na = MLA_HEADS * MLA_V_DIM
    return w_in_p, wq, wk, wv, w_out[:na].astype(BF16), w_out[na:].astype(BF16)


def _odd_weights(w_in, w_out):
    cd = N_DIL * DIL_HEADS_PER_GROUP * DIL_HEAD_DIM
    qd_w = SWA_Q_HEADS * SWA_HEAD_DIM
    kd_w = SWA_KV_HEADS * SWA_HEAD_DIM
    o = 0
    qc = w_in[:, o:o + cd] * (DIL_HEAD_DIM ** -0.5); o += cd
    kc = w_in[:, o:o + cd]; o += cd
    vc = w_in[:, o:o + cd]; o += cd
    qd = w_in[:, o:o + qd_w] * (SWA_HEAD_DIM ** -0.5); o += qd_w
    kd = w_in[:, o:o + kd_w]; o += kd_w
    vd = w_in[:, o:o + kd_w]
    used = qd_w + 3 * cd + 2 * kd_w
    pad = jnp.zeros((w_in.shape[0], ODD_Z - used), w_in.dtype)
    w_in_p = jnp.concatenate([qd, qc, kc, vc, kd, vd, pad], axis=-1).astype(BF16)
    nc = DIL_HEADS_PER_GROUP * DIL_HEAD_DIM
    return w_in_p, w_out[:nc].astype(BF16), w_out[nc:].astype(BF16)


def _rope_tables(seq):
    half = MLA_ROPE_DIM // 2
    inv_freq = ROPE_THETA ** (-jnp.arange(half, dtype=F32) / half)
    ang = jnp.arange(seq, dtype=jnp.int32).astype(F32)[:, None] * inv_freq[None, :]
    c, s = jnp.cos(ang), jnp.sin(ang)
    z = jnp.zeros_like(c)
    return jnp.concatenate([c, z, c, z], axis=-1), jnp.concatenate([-s, z, s, z], axis=-1)


def kernel(x, attn_norm, mlp_norm, w_up, w_down, ev_w_in, ev_q_norm, ev_w_uq, ev_kv_norm, ev_w_ukv, ev_w_out,
           od_w_in, od_sinks, od_w_out, final_norm):
    batch, seq, d = x.shape
    depth = attn_norm.shape[0]
    t = batch * seq
    xt = x.reshape(t, d)
    cos, sin = _rope_tables(seq)
    moba_slopes = jnp.exp2(-8.0 * jnp.arange(1, MOBA_HEADS + 1, dtype=F32) / MOBA_HEADS) * LOG2E
    n_dil_heads = N_DIL * DIL_HEADS_PER_GROUP
    dil_slopes = jnp.exp2(-8.0 * jnp.arange(1, n_dil_heads + 1, dtype=F32) / n_dil_heads)
    fg = final_norm.reshape(1, d)

    for layer in range(depth):
        i = layer // 2
        g = attn_norm[layer].reshape(1, d)
        if layer % 2 == 0:
            w_in_p, wq, wk, wv, wo_a, wo_b = _even_weights(ev_w_in[i], ev_w_uq[i], ev_w_ukv[i], ev_w_out[i])
            z = norm_matmul(xt, g, w_in_p, tm=256)
            q, k, v = mla_up(z, ev_q_norm[i].reshape(1, -1), ev_kv_norm[i].reshape(1, -1), wq, wk, wv, cos, sin,
                             seq=seq, tm=256)
            a = mla_attention(q.reshape(batch, seq, -1), k.reshape(batch, seq, -1), v.reshape(batch, seq, -1),
                              batch=batch, seq=seq, blk=256, heads=4)
            b = moba_attention(z.reshape(batch, seq, EVEN_Z), moba_slopes, batch=batch, seq=seq, heads=4)
            xt = proj_residual(xt, a.reshape(t, -1), b.reshape(t, -1), wo_a, wo_b, tm=512)
        else:
            w_in_p, wo_c, wo_d = _odd_weights(od_w_in[i], od_w_out[i])
            z = norm_matmul(xt, g, w_in_p, tm=256)
            z3 = z.reshape(batch, seq, ODD_Z)
            parts = []
            for grp in range(N_DIL):
                parts.extend(dilated_group(z3, dil_slopes, group=grp, batch=batch, seq=seq,
                                           heads=DIL_GROUP_HEADS_PER_STEP[grp]))
            c = dilated_merge(parts, tm=1024)
            dd = swa_attention(z3, od_sinks[i], batch=batch, seq=seq)
            xt = proj_residual(xt, c, dd.reshape(t, -1), wo_c, wo_d, tm=512)
        xt = mlp_residual(xt, mlp_norm[layer].reshape(1, d), w_up[layer].astype(BF16), w_down[layer].astype(BF16), fg,
                          tm=512, tf=1024, final_norm=(layer == depth - 1))
    return xt.reshape(batch, seq, d)
```

```python
import functools

import jax
import jax.numpy as jnp
from jax import lax
from jax.experimental import pallas as pl
from jax.experimental.pallas import tpu as pltpu

F32 = jnp.float32
BF16 = jnp.bfloat16

NORM_EPS = 1e-6
D_FF_MULT = 4

MLA_HEADS = 8
MLA_Q_RANK = 512
MLA_KV_RANK = 512
MLA_NOPE_DIM = 128
MLA_ROPE_DIM = 64
MLA_V_DIM = 128
ROPE_THETA = 10000.0

MOBA_HEADS = 8
MOBA_HEAD_DIM = 128
MOBA_BLOCK = 256
MOBA_TOPK = 3

DIL_WINDOWS = (128, 512, 2048)
DIL_RATES = (1, 4, 16)
N_DIL = 3
DIL_HEADS_PER_GROUP = 4
DIL_HEAD_DIM = 128

SWA_Q_HEADS = 16
SWA_KV_HEADS = 2
SWA_HEAD_DIM = 64
SWA_WINDOW = 128

BAND = 128
DIL_GROUP_HEADS_PER_STEP = (4, 4, 2)

LANES = 128
MLA_QK_PAD = 256
EVEN_Z = 4224
ODD_Z = 6144
VMEM_LIMIT = 56 * 1024 * 1024
LOG2E = 1.4426950408889634


def _params(sem):
    return pltpu.CompilerParams(dimension_semantics=sem, vmem_limit_bytes=VMEM_LIMIT)


def _rms(x, g):
    ms = jnp.mean(x * x, axis=-1, keepdims=True)
    return x * lax.rsqrt(ms + NORM_EPS) * g


def _qk(q, k):
    return lax.dot_general(q, k, (((1,), (1,)), ((), ())), preferred_element_type=F32)


def _norm_matmul_kernel(x_ref, g_ref, w_ref, o_ref):
    h = _rms(x_ref[...], g_ref[...]).astype(BF16)
    o_ref[...] = jnp.dot(h, w_ref[...], preferred_element_type=F32).astype(o_ref.dtype)


def norm_matmul(x, g, w, *, tm):
    m, k = x.shape
    n = w.shape[1]
    return pl.pallas_call(
        _norm_matmul_kernel,
        grid=(m // tm,),
        in_specs=[
            pl.BlockSpec((tm, k), lambda i: (i, 0)),
            pl.BlockSpec((1, k), lambda i: (0, 0)),
            pl.BlockSpec((k, n), lambda i: (0, 0), pipeline_mode=pl.Buffered(1)),
        ],
        out_specs=pl.BlockSpec((tm, n), lambda i: (i, 0)),
        out_shape=jax.ShapeDtypeStruct((m, n), BF16),
        compiler_params=_params(("parallel",)),
        name="norm_matmul",
    )(x, g, w)


def _mlp_kernel(x_ref, g_ref, wu_ref, wd_ref, fg_ref, o_ref, h_ref, *, final_norm):
    f = pl.program_id(1)

    @pl.when(f == 0)
    def _():
        x = x_ref[...]
        h_ref[...] = _rms(x, g_ref[...]).astype(BF16)
        o_ref[...] = x

    u = jnp.dot(h_ref[...], wu_ref[...], preferred_element_type=F32)
    u = jnp.maximum(u, 0.0)
    u = u * u
    o_ref[...] += jnp.dot(u.astype(BF16), wd_ref[...], preferred_element_type=F32)

    if final_norm:
        @pl.when(f == pl.num_programs(1) - 1)
        def _():
            o_ref[...] = _rms(o_ref[...], fg_ref[...])


def mlp_residual(x, g, w_up, w_down, final_g, *, tm, tf, final_norm):
    m, d = x.shape
    ff = w_up.shape[1]
    return pl.pallas_call(
        functools.partial(_mlp_kernel, final_norm=final_norm),
        grid=(m // tm, ff // tf),
        in_specs=[
            pl.BlockSpec((tm, d), lambda i, f: (i, 0), pipeline_mode=pl.Buffered(1)),
            pl.BlockSpec((1, d), lambda i, f: (0, 0)),
            pl.BlockSpec((d, tf), lambda i, f: (0, f)),
            pl.BlockSpec((tf, d), lambda i, f: (f, 0)),
            pl.BlockSpec((1, d), lambda i, f: (0, 0)),
        ],
        out_specs=pl.BlockSpec((tm, d), lambda i, f: (i, 0)),
        out_shape=jax.ShapeDtypeStruct((m, d), F32),
        scratch_shapes=[pltpu.VMEM((tm, d), BF16)],
        compiler_params=_params(("parallel", "arbitrary")),
        name="mlp_residual",
    )(x, g, w_up, w_down, final_g)


def _proj_residual_kernel(x_ref, a_ref, b_ref, wa_ref, wb_ref, o_ref):
    y = jnp.dot(a_ref[...], wa_ref[...], preferred_element_type=F32)
    y = y + jnp.dot(b_ref[...], wb_ref[...], preferred_element_type=F32)
    o_ref[...] = x_ref[...] + y


def proj_residual(x, a, b, wa, wb, *, tm):
    m, d = x.shape
    ka, kb = a.shape[1], b.shape[1]
    return pl.pallas_call(
        _proj_residual_kernel,
        grid=(m // tm,),
        in_specs=[
            pl.BlockSpec((tm, d), lambda i: (i, 0)),
            pl.BlockSpec((tm, ka), lambda i: (i, 0)),
            pl.BlockSpec((tm, kb), lambda i: (i, 0)),
            pl.BlockSpec((ka, d), lambda i: (0, 0)),
            pl.BlockSpec((kb, d), lambda i: (0, 0)),
        ],
        out_specs=pl.BlockSpec((tm, d), lambda i: (i, 0)),
        out_shape=jax.ShapeDtypeStruct((m, d), F32),
        compiler_params=_params(("parallel",)),
        name="proj_residual",
    )(x, a, b, wa, wb)


def _mla_up_kernel(ql_ref, kvl_ref, kpe_ref, qn_ref, kvn_ref, wq_ref, wk_ref, wv_ref, cos_ref, sin_ref,
                   q_ref, k_ref, v_ref):
    cos = cos_ref[...]
    sin = sin_ref[...]

    def rope(t):
        return t * cos + pltpu.roll(t, LANES // 2, 1) * sin

    hq = _rms(ql_ref[...].astype(F32), qn_ref[...]).astype(BF16)
    yq = jnp.dot(hq, wq_ref[...], preferred_element_type=F32)
    hkv = _rms(kvl_ref[...].astype(F32), kvn_ref[...]).astype(BF16)
    yk = jnp.dot(hkv, wk_ref[...], preferred_element_type=F32)
    yv = jnp.dot(hkv, wv_ref[...], preferred_element_type=F32)
    krot = rope(kpe_ref[...].astype(F32)).astype(BF16)
    for h in range(MLA_HEADS):
        lo = h * MLA_QK_PAD
        mid = lo + MLA_NOPE_DIM
        hi = lo + MLA_QK_PAD
        q_ref[:, lo:mid] = yq[:, lo:mid].astype(BF16)
        q_ref[:, mid:hi] = rope(yq[:, mid:hi]).astype(BF16)
        k_ref[:, lo:mid] = yk[:, h * MLA_NOPE_DIM:(h + 1) * MLA_NOPE_DIM].astype(BF16)
        k_ref[:, mid:hi] = krot
    v_ref[...] = yv.astype(BF16)


def mla_up(z, qn, kvn, wq, wk, wv, cos, sin, *, seq, tm):
    t = z.shape[0]
    r = MLA_Q_RANK
    nq = MLA_HEADS * MLA_QK_PAD
    nv = MLA_HEADS * MLA_V_DIM
    kpe_blk = (EVEN_Z - LANES) // LANES
    pos_blocks = seq // tm
    return pl.pallas_call(
        _mla_up_kernel,
        grid=(t // tm,),
        in_specs=[
            pl.BlockSpec((tm, r), lambda i: (i, 0)),
            pl.BlockSpec((tm, r), lambda i: (i, 1)),
            pl.BlockSpec((tm, LANES), lambda i: (i, kpe_blk)),
            pl.BlockSpec((1, r), lambda i: (0, 0)),
            pl.BlockSpec((1, r), lambda i: (0, 0)),
            pl.BlockSpec((r, nq), lambda i: (0, 0)),
            pl.BlockSpec((r, nv), lambda i: (0, 0)),
            pl.BlockSpec((r, nv), lambda i: (0, 0)),
            pl.BlockSpec((tm, LANES), lambda i: (i % pos_blocks, 0)),
            pl.BlockSpec((tm, LANES), lambda i: (i % pos_blocks, 0)),
        ],
        out_specs=[
            pl.BlockSpec((tm, nq), lambda i: (i, 0)),
            pl.BlockSpec((tm, nq), lambda i: (i, 0)),
            pl.BlockSpec((tm, nv), lambda i: (i, 0)),
        ],
        out_shape=[
            jax.ShapeDtypeStruct((t, nq), BF16),
            jax.ShapeDtypeStruct((t, nq), BF16),
            jax.ShapeDtypeStruct((t, nv), BF16),
        ],
        compiler_params=_params(("parallel",)),
        name="mla_up",
    )(z, z, z, qn, kvn, wq, wk, wv, cos, sin)


def _softmax_steps_t(sts, vts, carries):
    stats = []
    for st, (m, l, _) in zip(sts, carries):
        m_new = jnp.maximum(m, jnp.max(st, axis=0, keepdims=True))
        p = jnp.exp2(st - m_new)
        alpha = jnp.exp2(m - m_new)
        stats.append((m_new, alpha * l + jnp.sum(p, axis=0, keepdims=True), alpha, p.astype(BF16)))
    return tuple((m_new, l_new, alpha * acc + jnp.dot(vt, p, preferred_element_type=F32))
                 for (m_new, l_new, alpha, p), vt, (_, _, acc) in zip(stats, vts, carries))


def _softmax_init(dv, nq):
    return (jnp.full((1, nq), -jnp.inf, F32), jnp.zeros((1, nq), F32), jnp.zeros((dv, nq), F32))


def _transpose_bf16(x):
    return x.astype(F32).T.astype(BF16)


def _mla_attn_kernel(q_ref, k_ref, v_ref, o_ref, vt_ref, *, blk, heads):
    qi = pl.program_id(2)
    dq, dv = MLA_QK_PAD, MLA_V_DIM

    @pl.when(qi == 0)
    def _():
        def fill(n, c):
            start = pl.multiple_of(n * blk, blk)
            for g in range(heads):
                vt_ref[g * dv:(g + 1) * dv, pl.ds(start, blk)] = _transpose_bf16(
                    v_ref[pl.ds(start, blk), g * dv:(g + 1) * dv])
            return c

        lax.fori_loop(0, k_ref.shape[0] // blk, fill, 0)

    qts = [_transpose_bf16(q_ref[:, g * dq:(g + 1) * dq]) for g in range(heads)]
    tk = 2 * blk

    def tile(start, carries, mask):
        sts = [jnp.dot(k_ref[pl.ds(start, tk), g * dq:(g + 1) * dq], qts[g], preferred_element_type=F32)
               for g in range(heads)]
        if mask is not None:
            sts = [jnp.where(mask, st, -jnp.inf) for st in sts]
        vts = [vt_ref[g * dv:(g + 1) * dv, pl.ds(start, tk)] for g in range(heads)]
        return _softmax_steps_t(sts, vts, carries)

    init = tuple(_softmax_init(dv, blk) for _ in range(heads))
    npairs = qi // 2
    carries = lax.fori_loop(0, npairs, lambda n, c: tile(pl.multiple_of(n * tk, tk), c, None), init)
    start = pl.multiple_of(jnp.maximum(qi - 1, 0) * blk, blk)
    key_pos = start + lax.broadcasted_iota(jnp.int32, (tk, blk), 0)
    qry_pos = qi * blk + lax.broadcasted_iota(jnp.int32, (tk, blk), 1)
    carries = tile(start, carries, (key_pos <= qry_pos) & (key_pos >= npairs * tk))
    for g in range(heads):
        _, l, acc = carries[g]
        o_ref[:, g * dv:(g + 1) * dv] = (acc / l).T.astype(o_ref.dtype)


def mla_attention(q, k, v, *, batch, seq, blk, heads):
    dq = MLA_QK_PAD * heads
    dv = MLA_V_DIM * heads
    return pl.pallas_call(
        functools.partial(_mla_attn_kernel, blk=blk, heads=heads),
        grid=(batch, MLA_HEADS // heads, seq // blk),
        in_specs=[
            pl.BlockSpec((None, blk, dq), lambda b, h, i: (b, i, h)),
            pl.BlockSpec((None, seq, dq), lambda b, h, i: (b, 0, h), pipeline_mode=pl.Buffered(1)),
            pl.BlockSpec((None, seq, dv), lambda b, h, i: (b, 0, h), pipeline_mode=pl.Buffered(1)),
        ],
        out_specs=pl.BlockSpec((None, blk, dv), lambda b, h, i: (b, i, h)),
        out_shape=jax.ShapeDtypeStruct((batch, seq, MLA_HEADS * MLA_V_DIM), BF16),
        scratch_shapes=[pltpu.VMEM((dv, seq), BF16)],
        compiler_params=_params(("parallel", "parallel", "arbitrary")),
        name="mla_attention",
    )(q, k, v)


MOBA_NB_PAD = 16
MOBA_MASKED = -1e30


def _moba_kernel(slopes_ref, q_ref, k_ref, v_ref, o_ref, vt_ref, kmean_ref, sel_ref, *, nb, heads):
    blk = MOBA_BLOCK
    dh = MOBA_HEAD_DIM
    nbp = kmean_ref.shape[1]
    hg = pl.program_id(1)
    i = pl.program_id(2)

    @pl.when(i == 0)
    def _():
        kmean_ref[...] = jnp.zeros_like(kmean_ref)

        def fill(n, c):
            start = pl.multiple_of(n * blk, blk)
            for g in range(heads):
                kn = k_ref[pl.ds(start, blk), g * dh:(g + 1) * dh].astype(F32)
                kmean_ref[g, pl.ds(n, 1), :] = jnp.mean(kn, axis=0, keepdims=True)
                vt_ref[g * dh:(g + 1) * dh, pl.ds(start, blk)] = _transpose_bf16(
                    v_ref[pl.ds(start, blk), g * dh:(g + 1) * dh])
            return c

        lax.fori_loop(0, nb, fill, 0)

    blk_id = lax.broadcasted_iota(jnp.int32, (nbp, blk), 0)
    past = blk_id < i
    tk = 2 * blk
    key = lax.broadcasted_iota(jnp.int32, (tk, blk), 0)
    qry = lax.broadcasted_iota(jnp.int32, (tk, blk), 1)
    keyf = key.astype(F32)

    qts, colbias, slopes = [], [], []
    for g in range(heads):
        qt = _transpose_bf16(q_ref[:, g * dh:(g + 1) * dh])
        km = kmean_ref[g]
        km_hi = km.astype(BF16)
        km_lo = (km - km_hi.astype(F32)).astype(BF16)
        gate = (jnp.dot(km_hi, qt, preferred_element_type=F32)
                + jnp.dot(km_lo, qt, preferred_element_type=F32))
        gm = jnp.where(past, gate, -jnp.inf)
        rank = jnp.zeros((nbp, blk), jnp.int32)
        for n in range(nb):
            gn = gm[n:n + 1, :]
            beats = (gn > gm) | ((gn == gm) & (blk_id > n))
            rank = rank + beats.astype(jnp.int32)
        sel_ref[g] = jnp.where(past & (rank < min(MOBA_TOPK, nb)), 0.0, MOBA_MASKED)
        slope = slopes_ref[hg * heads + g]
        qts.append(qt)
        slopes.append(slope)
        colbias.append(keyf * slope)

    def tile(b0, carries, gates, mask):
        start = pl.multiple_of(b0 * blk, blk)
        sts = [jnp.dot(k_ref[pl.ds(start, tk), g * dh:(g + 1) * dh], qts[g], preferred_element_type=F32)
               for g in range(heads)]
        biased = []
        for g in range(heads):
            shift = jnp.full((1, blk), (b0 - i) * blk, jnp.int32).astype(F32) * slopes[g]
            st = sts[g] + colbias[g]
            st = jnp.concatenate([st[:blk] + (gates[g][0] + shift), st[blk:] + (gates[g][1] + shift)], axis=0)
            biased.append(st if mask is None else jnp.where(mask, st, -jnp.inf))
        vts = [vt_ref[g * dh:(g + 1) * dh, pl.ds(start, tk)] for g in range(heads)]
        return _softmax_steps_t(biased, vts, carries)

    def body(n, carries):
        gates = [(sel_ref[g, pl.ds(2 * n, 1), :], sel_ref[g, pl.ds(2 * n + 1, 1), :]) for g in range(heads)]
        return tile(2 * n, carries, gates, None)

    init = tuple(_softmax_init(dh, blk) for _ in range(heads))
    npairs = i // 2
    carries = lax.fori_loop(0, npairs, body, init)
    b0 = jnp.maximum(i - 1, 0)
    zero = jnp.zeros((1, blk), F32)
    gates = [(jnp.where(i % 2 == 1, sel_ref[g, pl.ds(b0, 1), :], zero), zero) for g in range(heads)]
    mask = (key + b0 * blk <= qry + i * blk) & (key + b0 * blk >= npairs * tk)
    carries = tile(b0, carries, gates, mask)
    for g in range(heads):
        _, l, acc = carries[g]
        o_ref[:, g * dh:(g + 1) * dh] = (acc / l).T.astype(o_ref.dtype)


def moba_attention(z, slopes, *, batch, seq, heads):
    blk = MOBA_BLOCK
    w = MOBA_HEAD_DIM * heads
    nb = seq // blk
    qo, ko, vo = 1024 // w, 2048 // w, 3072 // w
    return pl.pallas_call(
        functools.partial(_moba_kernel, nb=nb, heads=heads),
        grid=(batch, MOBA_HEADS // heads, nb),
        in_specs=[
            pl.BlockSpec(memory_space=pltpu.SMEM),
            pl.BlockSpec((None, blk, w), lambda b, h, i: (b, i, qo + h)),
            pl.BlockSpec((None, seq, w), lambda b, h, i: (b, 0, ko + h)),
            pl.BlockSpec((None, seq, w), lambda b, h, i: (b, 0, vo + h)),
        ],
        out_specs=pl.BlockSpec((None, blk, w), lambda b, h, i: (b, i, h)),
        out_shape=jax.ShapeDtypeStruct((batch, seq, MOBA_HEADS * MOBA_HEAD_DIM), BF16),
        scratch_shapes=[
            pltpu.VMEM((w, seq), BF16),
            pltpu.VMEM((heads, max(MOBA_NB_PAD, nb), MOBA_HEAD_DIM), F32),
            pltpu.VMEM((heads, max(MOBA_NB_PAD, nb), blk), F32),
        ],
        compiler_params=_params(("parallel", "parallel", "arbitrary")),
        name="moba_attention",
    )(slopes, z, z, z)


def _band_mask(j, max_dist):
    row = lax.broadcasted_iota(jnp.int32, (BAND, 2 * BAND), 0)
    col = lax.broadcasted_iota(jnp.int32, (BAND, 2 * BAND), 1)
    dist = BAND + row - col
    mask = (dist >= 0) & (dist <= max_dist) & ((col >= BAND) | (j > 0))
    return mask, dist.astype(F32)


def _dil_kernel(slopes_ref, q_ref, k_ref, v_ref, o_ref, lse_ref, stage_ref, qd_ref, kd_ref, vd_ref, *, group):
    hb = pl.program_id(1)
    j = pl.program_id(2)
    rate = DIL_RATES[group]
    dh = DIL_HEAD_DIM
    nh = q_ref.shape[-1] // dh
    slot = j % 2
    prev = 1 - slot

    @pl.when(j == 0)
    def _():
        kd_ref[1] = jnp.zeros(kd_ref.shape[1:], BF16)
        vd_ref[1] = jnp.zeros(vd_ref.shape[1:], BF16)

    def split(src_ref, store):
        if rate == 1:
            store(0, src_ref[...])
            return
        for hh in range(nh):
            stage_ref[0, hh] = src_ref[:, hh * dh:(hh + 1) * dh].astype(F32)
        for r in range(rate):
            parts = [stage_ref[0, hh, pl.ds(r, BAND, stride=rate), :].astype(BF16) for hh in range(nh)]
            store(r, jnp.concatenate(parts, axis=1))

    def store_q(r, x):
        qd_ref[r] = x

    def store_k(r, x):
        kd_ref[slot, r] = x

    def store_v(r, x):
        vd_ref[slot, r] = x

    split(q_ref, store_q)
    split(k_ref, store_k)
    split(v_ref, store_v)

    mask, distf = _band_mask(j, DIL_WINDOWS[group] // rate)
    biases = []
    for hh in range(nh):
        slope = slopes_ref[group * DIL_HEADS_PER_GROUP + hb * nh + hh] * rate
        biases.append(jnp.where(mask, -slope * distf, -jnp.inf))

    def sub(r, c):
        q = qd_ref[r]
        kw = jnp.concatenate([kd_ref[prev, r], kd_ref[slot, r]], axis=0)
        vw = jnp.concatenate([vd_ref[prev, r], vd_ref[slot, r]], axis=0)
        sls = [slice(hh * dh, (hh + 1) * dh) for hh in range(nh)]
        ss = [_qk(q[:, sl], kw[:, sl]) + biases[hh] for hh, sl in enumerate(sls)]
        stats = []
        for s in ss:
            m = jnp.max(s, axis=-1, keepdims=True)
            p = jnp.exp(s - m)
            stats.append((m, jnp.sum(p, axis=-1, keepdims=True), p.astype(BF16)))
        for hh, ((m, l, p), sl) in enumerate(zip(stats, sls)):
            out = jnp.dot(p, vw[:, sl], preferred_element_type=F32) / l
            lse = jnp.broadcast_to(m + jnp.log(l), (BAND, dh))
            if rate == 1:
                o_ref[:, sl] = out
                lse_ref[:, sl] = lse
            else:
                stage_ref[0, hh, pl.ds(r, BAND, stride=rate), :] = out
                stage_ref[1, hh, pl.ds(r, BAND, stride=rate), :] = lse
        return c

    lax.fori_loop(0, rate, sub, 0)
    if rate > 1:
        for hh in range(nh):
            o_ref[:, hh * dh:(hh + 1) * dh] = stage_ref[0, hh]
            lse_ref[:, hh * dh:(hh + 1) * dh] = stage_ref[1, hh]


def dilated_group(z, slopes, *, group, batch, seq, heads):
    rate = DIL_RATES[group]
    tile = BAND * rate
    gw = DIL_HEADS_PER_GROUP * DIL_HEAD_DIM
    w = heads * DIL_HEAD_DIM
    nhb = DIL_HEADS_PER_GROUP // heads
    qo, ko, vo = ((2 + group) * gw // w, (5 + group) * gw // w, (8 + group) * gw // w)

    def spec(off):
        return pl.BlockSpec((None, tile, w), lambda b, h, j: (b, j, off + h))

    out_spec = pl.BlockSpec((None, tile, w), lambda b, h, j: (b, j, h))
    o, lse = pl.pallas_call(
        functools.partial(_dil_kernel, group=group),
        grid=(batch, nhb, seq // tile),
        in_specs=[pl.BlockSpec(memory_space=pltpu.SMEM), spec(qo), spec(ko), spec(vo)],
        out_specs=[out_spec, out_spec],
        out_shape=[jax.ShapeDtypeStruct((batch, seq, gw), F32)] * 2,
        scratch_shapes=[
            pltpu.VMEM((2, heads, tile, DIL_HEAD_DIM), F32),
            pltpu.VMEM((rate, BAND, w), BF16),
            pltpu.VMEM((2, rate, BAND, w), BF16),
            pltpu.VMEM((2, rate, BAND, w), BF16),
        ],
        compiler_params=_params(("parallel", "parallel", "arbitrary")),
        name=f"dilated_group{group}",
    )(slopes, z, z, z)
    return o.reshape(batch * seq, gw), lse.reshape(batch * seq, gw)


def _dil_merge_kernel(o0_ref, l0_ref, o1_ref, l1_ref, o2_ref, l2_ref, out_ref):
    l0, l1, l2 = l0_ref[...], l1_ref[...], l2_ref[...]
    mx = jnp.maximum(jnp.maximum(l0, l1), l2)
    w0, w1, w2 = jnp.exp(l0 - mx), jnp.exp(l1 - mx), jnp.exp(l2 - mx)
    num = o0_ref[...] * w0 + o1_ref[...] * w1 + o2_ref[...] * w2
    out_ref[...] = (num / (w0 + w1 + w2)).astype(out_ref.dtype)


def dilated_merge(parts, *, tm):
    t, w = parts[0].shape
    spec = pl.BlockSpec((tm, w), lambda i: (i, 0))
    return pl.pallas_call(
        _dil_merge_kernel,
        grid=(t // tm,),
        in_specs=[spec] * 6,
        out_specs=spec,
        out_shape=jax.ShapeDtypeStruct((t, w), BF16),
        compiler_params=_params(("parallel",)),
        name="dilated_merge",
    )(*parts)


def _swa_kernel(sinks_ref, q_ref, kp_ref, ko_ref, vp_ref, vo_ref, o_ref):
    j = pl.program_id(1)
    mask, distf = _band_mask(j, SWA_WINDOW - 1)
    heads_per_kv = SWA_Q_HEADS // SWA_KV_HEADS
    half = LANES // 2

    kw = jnp.concatenate([kp_ref[...], ko_ref[...]], axis=0).astype(F32)
    vw = jnp.concatenate([vp_ref[...], vo_ref[...]], axis=0).astype(F32)
    low = lax.broadcasted_iota(jnp.int32, (2 * BAND, LANES), 1) < half

    def halves(x):
        xr = pltpu.roll(x, half, 1)
        even = (jnp.where(low, x, 0.0).astype(BF16), jnp.where(low, xr, 0.0).astype(BF16))
        odd = (jnp.where(low, 0.0, xr).astype(BF16), jnp.where(low, 0.0, x).astype(BF16))
        return even, odd

    k_even, k_odd = halves(kw)
    v_even, v_odd = halves(vw)

    for pair in range(SWA_Q_HEADS // 2):
        kvh = (2 * pair) // heads_per_kv
        slab = q_ref[:, pair * LANES:(pair + 1) * LANES]
        out = None
        for par, (kk, vv) in enumerate(((k_even[kvh], v_even[kvh]), (k_odd[kvh], v_odd[kvh]))):
            head = 2 * pair + par
            slope = 2.0 ** (-8.0 * (head + 1) / SWA_Q_HEADS)
            s = _qk(slab, kk) - slope * distf
            s = jnp.where(mask, s, -jnp.inf)
            m = jnp.max(s, axis=-1, keepdims=True)
            p = jnp.exp(s - m)
            l = jnp.sum(p, axis=-1, keepdims=True)
            acc = jnp.dot(p.astype(BF16), vv, preferred_element_type=F32)
            sk = sinks_ref[head]
            mx = jnp.maximum(m, sk)
            a = jnp.exp(m - mx)
            o = acc * a / (l * a + jnp.exp(sk - mx))
            out = o if out is None else out + o
        o_ref[:, pair * LANES:(pair + 1) * LANES] = out.astype(o_ref.dtype)


def swa_attention(z, sinks, *, batch, seq):
    wq = SWA_Q_HEADS * SWA_HEAD_DIM
    ko, vo = 5632 // LANES, 5760 // LANES

    def kvspec(off, prev):
        if prev:
            return pl.BlockSpec((None, BAND, LANES), lambda b, j: (b, jnp.maximum(j - 1, 0), off))
        return pl.BlockSpec((None, BAND, LANES), lambda b, j: (b, j, off))

    return pl.pallas_call(
        _swa_kernel,
        grid=(batch, seq // BAND),
        in_specs=[
            pl.BlockSpec(memory_space=pltpu.SMEM),
            pl.BlockSpec((None, BAND, wq), lambda b, j: (b, j, 0)),
            kvspec(ko, True), kvspec(ko, False), kvspec(vo, True), kvspec(vo, False),
        ],
        out_specs=pl.BlockSpec((None, BAND, wq), lambda b, j: (b, j, 0)),
        out_shape=jax.ShapeDtypeStruct((batch, seq, wq), BF16),
        compiler_params=_params(("parallel", "arbitrary")),
        name="swa_attention",
    )(sinks, z, z, z, z, z)


def _rope_pad(w):
    half = MLA_ROPE_DIM // 2
    z = jnp.zeros(w.shape[:-1] + (half,), w.dtype)
    return jnp.concatenate([w[..., :half], z, w[..., half:], z], axis=-1)


def _even_weights(w_in, w_uq, w_ukv, w_out):
    hd = MOBA_HEADS * MOBA_HEAD_DIM
    o = 0
    q_lat = w_in[:, o:o + MLA_Q_RANK]; o += MLA_Q_RANK
    kv_lat = w_in[:, o:o + MLA_KV_RANK]; o += MLA_KV_RANK
    k_pe = w_in[:, o:o + MLA_ROPE_DIM]; o += MLA_ROPE_DIM
    qb = w_in[:, o:o + hd] * (MOBA_HEAD_DIM ** -0.5 * LOG2E); o += hd
    kb = w_in[:, o:o + hd]; o += hd
    vb = w_in[:, o:o + hd]
    w_in_p = jnp.concatenate([q_lat, kv_lat, qb, kb, vb, _rope_pad(k_pe)], axis=-1).astype(BF16)

    r = w_uq.shape[0]
    uq = w_uq.reshape(r, MLA_HEADS, MLA_NOPE_DIM + MLA_ROPE_DIM) * ((MLA_NOPE_DIM + MLA_ROPE_DIM) ** -0.5 * LOG2E)
    wq = jnp.concatenate([uq[..., :MLA_NOPE_DIM], _rope_pad(uq[..., MLA_NOPE_DIM:])], axis=-1)
    wq = wq.reshape(r, MLA_HEADS * MLA_QK_PAD).astype(BF16)
    ukv = w_ukv.reshape(w_ukv.shape[0], MLA_HEADS, MLA_NOPE_DIM + MLA_V_DIM)
    wk = ukv[..., :MLA_NOPE_DIM].reshape(-1, MLA_HEADS * MLA_NOPE_DIM).astype(BF16)
    wv = ukv[..., MLA_NOPE_DIM:].reshape(-1, MLA_HEADS * MLA_V_DIM).astype(BF16)
    na = MLA_HEADS * MLA_V_DIM
    return w_in_p, wq, wk, wv, w_out[:na].astype(BF16), w_out[na:].astype(BF16)


def _odd_weights(w_in, w_out):
    cd = N_DIL * DIL_HEADS_PER_GROUP * DIL_HEAD_DIM
    qd_w = SWA_Q_HEADS * SWA_HEAD_DIM
    kd_w = SWA_KV_HEADS * SWA_HEAD_DIM
    o = 0
    qc = w_in[:, o:o + cd] * (DIL_HEAD_DIM ** -0.5); o += cd
    kc = w_in[:, o:o + cd]; o += cd
    vc = w_in[:, o:o + cd]; o += cd
    qd = w_in[:, o:o + qd_w] * (SWA_HEAD_DIM ** -0.5); o += qd_w
    kd = w_in[:, o:o + kd_w]; o += kd_w
    vd = w_in[:, o:o + kd_w]
    used = qd_w + 3 * cd + 2 * kd_w
    pad = jnp.zeros((w_in.shape[0], ODD_Z - used), w_in.dtype)
    w_in_p = jnp.concatenate([qd, qc, kc, vc, kd, vd, pad], axis=-1).astype(BF16)
    nc = DIL_HEADS_PER_GROUP * DIL_HEAD_DIM
    return w_in_p, w_out[:nc].astype(BF16), w_out[nc:].astype(BF16)


def _rope_tables(seq):
    half = MLA_ROPE_DIM // 2
    inv_freq = ROPE_THETA ** (-jnp.arange(half, dtype=F32) / half)
    ang = jnp.arange(seq, dtype=jnp.int32).astype(F32)[:, None] * inv_freq[None, :]
    c, s = jnp.cos(ang), jnp.sin(ang)
    z = jnp.zeros_like(c)
    return jnp.concatenate([c, z, c, z], axis=-1), jnp.concatenate([-s, z, s, z], axis=-1)


def kernel(x, attn_norm, mlp_norm, w_up, w_down, ev_w_in, ev_q_norm, ev_w_uq, ev_kv_norm, ev_w_ukv, ev_w_out,
           od_w_in, od_sinks, od_w_out, final_norm):
    batch, seq, d = x.shape
    depth = attn_norm.shape[0]
    t = batch * seq
    xt = x.reshape(t, d)
    cos, sin = _rope_tables(seq)
    moba_slopes = jnp.exp2(-8.0 * jnp.arange(1, MOBA_HEADS + 1, dtype=F32) / MOBA_HEADS) * LOG2E
    n_dil_heads = N_DIL * DIL_HEADS_PER_GROUP
    dil_slopes = jnp.exp2(-8.0 * jnp.arange(1, n_dil_heads + 1, dtype=F32) / n_dil_heads)
    fg = final_norm.reshape(1, d)

    for layer in range(depth):
        i = layer // 2
        g = attn_norm[layer].reshape(1, d)
        if layer % 2 == 0:
            w_in_p, wq, wk, wv, wo_a, wo_b = _even_weights(ev_w_in[i], ev_w_uq[i], ev_w_ukv[i], ev_w_out[i])
            z = norm_matmul(xt, g, w_in_p, tm=256)
            q, k, v = mla_up(z, ev_q_norm[i].reshape(1, -1), ev_kv_norm[i].reshape(1, -1), wq, wk, wv, cos, sin,
                             seq=seq, tm=256)
            a = mla_attention(q.reshape(batch, seq, -1), k.reshape(batch, seq, -1), v.reshape(batch, seq, -1),
                              batch=batch, seq=seq, blk=256, heads=8)
            b = moba_attention(z.reshape(batch, seq, EVEN_Z), moba_slopes, batch=batch, seq=seq, heads=8)
            xt = proj_residual(xt, a.reshape(t, -1), b.reshape(t, -1), wo_a, wo_b, tm=512)
        else:
            w_in_p, wo_c, wo_d = _odd_weights(od_w_in[i], od_w_out[i])
            z = norm_matmul(xt, g, w_in_p, tm=256)
            z3 = z.reshape(batch, seq, ODD_Z)
            parts = []
            for grp in range(N_DIL):
                parts.extend(dilated_group(z3, dil_slopes, group=grp, batch=batch, seq=seq,
                                           heads=DIL_GROUP_HEADS_PER_STEP[grp]))
            c = dilated_merge(parts, tm=1024)
            dd = swa_attention(z3, od_sinks[i], batch=batch, seq=seq)
            xt = proj_residual(xt, c, dd.reshape(t, -1), wo_c, wo_d, tm=512)
        xt = mlp_residual(xt, mlp_norm[layer].reshape(1, d), w_up[layer].astype(BF16), w_down[layer].astype(BF16), fg,
                          tm=1024, tf=1024, final_norm=(layer == depth - 1))
    return xt.reshape(batch, seq, d)
```

```python
import functools

import jax
import jax.numpy as jnp
from jax import lax
from jax.experimental import pallas as pl
from jax.experimental.pallas import tpu as pltpu

F32 = jnp.float32
BF16 = jnp.bfloat16

NORM_EPS = 1e-6
D_FF_MULT = 4

MLA_HEADS = 8
MLA_Q_RANK = 512
MLA_KV_RANK = 512
MLA_NOPE_DIM = 128
MLA_ROPE_DIM = 64
MLA_V_DIM = 128
ROPE_THETA = 10000.0

MOBA_HEADS = 8
MOBA_HEAD_DIM = 128
MOBA_BLOCK = 256
MOBA_TOPK = 3

DIL_WINDOWS = (128, 512, 2048)
DIL_RATES = (1, 4, 16)
N_DIL = 3
DIL_HEADS_PER_GROUP = 4
DIL_HEAD_DIM = 128

SWA_Q_HEADS = 16
SWA_KV_HEADS = 2
SWA_HEAD_DIM = 64
SWA_WINDOW = 128

BAND = 128
DIL_GROUP_HEADS_PER_STEP = (4, 4, 2)

LANES = 128
MLA_QK_PAD = 256
EVEN_Z = 4224
ODD_Z = 5888
VMEM_LIMIT = 56 * 1024 * 1024
LOG2E = 1.4426950408889634


def _params(sem):
    return pltpu.CompilerParams(dimension_semantics=sem, vmem_limit_bytes=VMEM_LIMIT)


def _rms(x, g):
    ms = jnp.mean(x * x, axis=-1, keepdims=True)
    return x * lax.rsqrt(ms + NORM_EPS) * g


def _qk(q, k):
    return lax.dot_general(q, k, (((1,), (1,)), ((), ())), preferred_element_type=F32)


def _norm_matmul_kernel(x_ref, g_ref, w_ref, o_ref):
    h = _rms(x_ref[...], g_ref[...]).astype(BF16)
    o_ref[...] = jnp.dot(h, w_ref[...], preferred_element_type=F32).astype(o_ref.dtype)


def norm_matmul(x, g, w, *, tm):
    m, k = x.shape
    n = w.shape[1]
    return pl.pallas_call(
        _norm_matmul_kernel,
        grid=(m // tm,),
        in_specs=[
            pl.BlockSpec((tm, k), lambda i: (i, 0)),
            pl.BlockSpec((1, k), lambda i: (0, 0)),
            pl.BlockSpec((k, n), lambda i: (0, 0), pipeline_mode=pl.Buffered(1)),
        ],
        out_specs=pl.BlockSpec((tm, n), lambda i: (i, 0)),
        out_shape=jax.ShapeDtypeStruct((m, n), BF16),
        compiler_params=_params(("parallel",)),
        name="norm_matmul",
    )(x, g, w)


def _mlp_kernel(x_ref, g_ref, wu_ref, wd_ref, fg_ref, o_ref, h_ref, acc_ref, *, final_norm):
    f = pl.program_id(1)

    @pl.when(f == 0)
    def _():
        h_ref[...] = _rms(x_ref[...], g_ref[...]).astype(BF16)
        acc_ref[...] = jnp.zeros_like(acc_ref)

    u = jnp.dot(h_ref[...], wu_ref[...], preferred_element_type=F32)
    u = jnp.maximum(u, 0.0)
    u = u * u
    acc_ref[...] += jnp.dot(u.astype(BF16), wd_ref[...], preferred_element_type=F32)

    @pl.when(f == pl.num_programs(1) - 1)
    def _():
        y = x_ref[...] + acc_ref[...]
        if final_norm:
            y = _rms(y, fg_ref[...])
        o_ref[...] = y


def mlp_residual(x, g, w_up, w_down, final_g, *, tm, tf, final_norm):
    m, d = x.shape
    ff = w_up.shape[1]
    return pl.pallas_call(
        functools.partial(_mlp_kernel, final_norm=final_norm),
        grid=(m // tm, ff // tf),
        in_specs=[
            pl.BlockSpec((tm, d), lambda i, f: (i, 0)),
            pl.BlockSpec((1, d), lambda i, f: (0, 0)),
            pl.BlockSpec((d, tf), lambda i, f: (0, f)),
            pl.BlockSpec((tf, d), lambda i, f: (f, 0)),
            pl.BlockSpec((1, d), lambda i, f: (0, 0)),
        ],
        out_specs=pl.BlockSpec((tm, d), lambda i, f: (i, 0)),
        out_shape=jax.ShapeDtypeStruct((m, d), F32),
        scratch_shapes=[pltpu.VMEM((tm, d), BF16), pltpu.VMEM((tm, d), F32)],
        compiler_params=_params(("parallel", "arbitrary")),
        name="mlp_residual",
    )(x, g, w_up, w_down, final_g)


def _proj_residual_kernel(x_ref, a_ref, b_ref, wa_ref, wb_ref, o_ref):
    y = jnp.dot(a_ref[...], wa_ref[...], preferred_element_type=F32)
    y = y + jnp.dot(b_ref[...], wb_ref[...], preferred_element_type=F32)
    o_ref[...] = x_ref[...] + y


def proj_residual(x, a, b, wa, wb, *, tm):
    m, d = x.shape
    ka, kb = a.shape[1], b.shape[1]
    return pl.pallas_call(
        _proj_residual_kernel,
        grid=(m // tm,),
        in_specs=[
            pl.BlockSpec((tm, d), lambda i: (i, 0)),
            pl.BlockSpec((tm, ka), lambda i: (i, 0)),
            pl.BlockSpec((tm, kb), lambda i: (i, 0)),
            pl.BlockSpec((ka, d), lambda i: (0, 0)),
            pl.BlockSpec((kb, d), lambda i: (0, 0)),
        ],
        out_specs=pl.BlockSpec((tm, d), lambda i: (i, 0)),
        out_shape=jax.ShapeDtypeStruct((m, d), F32),
        compiler_params=_params(("parallel",)),
        name="proj_residual",
    )(x, a, b, wa, wb)


def _mla_up_kernel(ql_ref, kvl_ref, kpe_ref, qn_ref, kvn_ref, wq_ref, wk_ref, wv_ref, cos_ref, sin_ref,
                   q_ref, k_ref, v_ref):
    cos = cos_ref[...]
    sin = sin_ref[...]

    def rope(t):
        return t * cos + pltpu.roll(t, LANES // 2, 1) * sin

    hq = _rms(ql_ref[...].astype(F32), qn_ref[...]).astype(BF16)
    yq = jnp.dot(hq, wq_ref[...], preferred_element_type=F32)
    hkv = _rms(kvl_ref[...].astype(F32), kvn_ref[...]).astype(BF16)
    yk = jnp.dot(hkv, wk_ref[...], preferred_element_type=F32)
    yv = jnp.dot(hkv, wv_ref[...], preferred_element_type=F32)
    krot = rope(kpe_ref[...].astype(F32)).astype(BF16)
    for h in range(MLA_HEADS):
        lo = h * MLA_QK_PAD
        mid = lo + MLA_NOPE_DIM
        hi = lo + MLA_QK_PAD
        q_ref[:, lo:mid] = yq[:, lo:mid].astype(BF16)
        q_ref[:, mid:hi] = rope(yq[:, mid:hi]).astype(BF16)
        k_ref[:, lo:mid] = yk[:, h * MLA_NOPE_DIM:(h + 1) * MLA_NOPE_DIM].astype(BF16)
        k_ref[:, mid:hi] = krot
    v_ref[...] = yv.astype(BF16)


def mla_up(z, qn, kvn, wq, wk, wv, cos, sin, *, seq, tm):
    t = z.shape[0]
    r = MLA_Q_RANK
    nq = MLA_HEADS * MLA_QK_PAD
    nv = MLA_HEADS * MLA_V_DIM
    kpe_blk = (EVEN_Z - LANES) // LANES
    pos_blocks = seq // tm
    return pl.pallas_call(
        _mla_up_kernel,
        grid=(t // tm,),
        in_specs=[
            pl.BlockSpec((tm, r), lambda i: (i, 0)),
            pl.BlockSpec((tm, r), lambda i: (i, 1)),
            pl.BlockSpec((tm, LANES), lambda i: (i, kpe_blk)),
            pl.BlockSpec((1, r), lambda i: (0, 0)),
            pl.BlockSpec((1, r), lambda i: (0, 0)),
            pl.BlockSpec((r, nq), lambda i: (0, 0)),
            pl.BlockSpec((r, nv), lambda i: (0, 0)),
            pl.BlockSpec((r, nv), lambda i: (0, 0)),
            pl.BlockSpec((tm, LANES), lambda i: (i % pos_blocks, 0)),
            pl.BlockSpec((tm, LANES), lambda i: (i % pos_blocks, 0)),
        ],
        out_specs=[
            pl.BlockSpec((tm, nq), lambda i: (i, 0)),
            pl.BlockSpec((tm, nq), lambda i: (i, 0)),
            pl.BlockSpec((tm, nv), lambda i: (i, 0)),
        ],
        out_shape=[
            jax.ShapeDtypeStruct((t, nq), BF16),
            jax.ShapeDtypeStruct((t, nq), BF16),
            jax.ShapeDtypeStruct((t, nv), BF16),
        ],
        compiler_params=_params(("parallel",)),
        name="mla_up",
    )(z, z, z, qn, kvn, wq, wk, wv, cos, sin)


def _softmax_steps_t(sts, vts, carries):
    stats = []
    for st, (m, l, _) in zip(sts, carries):
        m_new = jnp.maximum(m, jnp.max(st, axis=0, keepdims=True))
        p = jnp.exp2(st - m_new)
        alpha = jnp.exp2(m - m_new)
        stats.append((m_new, alpha * l + jnp.sum(p, axis=0, keepdims=True), alpha, p.astype(BF16)))
    return tuple((m_new, l_new, alpha * acc + jnp.dot(vt, p, preferred_element_type=F32))
                 for (m_new, l_new, alpha, p), vt, (_, _, acc) in zip(stats, vts, carries))


def _softmax_init(dv, nq):
    return (jnp.full((1, nq), -jnp.inf, F32), jnp.zeros((1, nq), F32), jnp.zeros((dv, nq), F32))


def _transpose_bf16(x):
    return x.astype(F32).T.astype(BF16)


def _mla_attn_kernel(q_ref, k_ref, v_ref, o_ref, vt_ref, *, blk, heads):
    qi = pl.program_id(2)
    dq, dv = MLA_QK_PAD, MLA_V_DIM

    @pl.when(qi == 0)
    def _():
        def fill(n, c):
            start = pl.multiple_of(n * blk, blk)
            for g in range(heads):
                vt_ref[g * dv:(g + 1) * dv, pl.ds(start, blk)] = _transpose_bf16(
                    v_ref[pl.ds(start, blk), g * dv:(g + 1) * dv])
            return c

        lax.fori_loop(0, k_ref.shape[0] // blk, fill, 0)

    qts = [_transpose_bf16(q_ref[:, g * dq:(g + 1) * dq]) for g in range(heads)]
    tk = 2 * blk

    def tile(start, carries, mask):
        sts = [jnp.dot(k_ref[pl.ds(start, tk), g * dq:(g + 1) * dq], qts[g], preferred_element_type=F32)
               for g in range(heads)]
        if mask is not None:
            sts = [jnp.where(mask, st, -jnp.inf) for st in sts]
        vts = [vt_ref[g * dv:(g + 1) * dv, pl.ds(start, tk)] for g in range(heads)]
        return _softmax_steps_t(sts, vts, carries)

    init = tuple(_softmax_init(dv, blk) for _ in range(heads))
    npairs = qi // 2
    carries = lax.fori_loop(0, npairs, lambda n, c: tile(pl.multiple_of(n * tk, tk), c, None), init)
    start = pl.multiple_of(jnp.maximum(qi - 1, 0) * blk, blk)
    key_pos = start + lax.broadcasted_iota(jnp.int32, (tk, blk), 0)
    qry_pos = qi * blk + lax.broadcasted_iota(jnp.int32, (tk, blk), 1)
    carries = tile(start, carries, (key_pos <= qry_pos) & (key_pos >= npairs * tk))
    for g in range(heads):
        _, l, acc = carries[g]
        o_ref[:, g * dv:(g + 1) * dv] = (acc / l).T.astype(o_ref.dtype)


def mla_attention(q, k, v, *, batch, seq, blk, heads):
    dq = MLA_QK_PAD * heads
    dv = MLA_V_DIM * heads
    return pl.pallas_call(
        functools.partial(_mla_attn_kernel, blk=blk, heads=heads),
        grid=(batch, MLA_HEADS // heads, seq // blk),
        in_specs=[
            pl.BlockSpec((None, blk, dq), lambda b, h, i: (b, i, h)),
            pl.BlockSpec((None, seq, dq), lambda b, h, i: (b, 0, h), pipeline_mode=pl.Buffered(1)),
            pl.BlockSpec((None, seq, dv), lambda b, h, i: (b, 0, h), pipeline_mode=pl.Buffered(1)),
        ],
        out_specs=pl.BlockSpec((None, blk, dv), lambda b, h, i: (b, i, h)),
        out_shape=jax.ShapeDtypeStruct((batch, seq, MLA_HEADS * MLA_V_DIM), BF16),
        scratch_shapes=[pltpu.VMEM((dv, seq), BF16)],
        compiler_params=_params(("parallel", "parallel", "arbitrary")),
        name="mla_attention",
    )(q, k, v)


MOBA_NB_PAD = 16
MOBA_MASKED = -1e30


def _moba_kernel(slopes_ref, q_ref, k_ref, v_ref, o_ref, vt_ref, kmean_ref, sel_ref, *, nb, heads):
    blk = MOBA_BLOCK
    dh = MOBA_HEAD_DIM
    nbp = kmean_ref.shape[1]
    hg = pl.program_id(1)
    i = pl.program_id(2)

    @pl.when(i == 0)
    def _():
        kmean_ref[...] = jnp.zeros_like(kmean_ref)

        def fill(n, c):
            start = pl.multiple_of(n * blk, blk)
            for g in range(heads):
                kn = k_ref[pl.ds(start, blk), g * dh:(g + 1) * dh].astype(F32)
                kmean_ref[g, pl.ds(n, 1), :] = jnp.mean(kn, axis=0, keepdims=True)
                vt_ref[g * dh:(g + 1) * dh, pl.ds(start, blk)] = _transpose_bf16(
                    v_ref[pl.ds(start, blk), g * dh:(g + 1) * dh])
            return c

        lax.fori_loop(0, nb, fill, 0)

    blk_id = lax.broadcasted_iota(jnp.int32, (nbp, blk), 0)
    past = blk_id < i
    tk = 2 * blk
    key = lax.broadcasted_iota(jnp.int32, (tk, blk), 0)
    qry = lax.broadcasted_iota(jnp.int32, (tk, blk), 1)
    keyf = key.astype(F32)

    qts, colbias, slopes = [], [], []
    for g in range(heads):
        qt = _transpose_bf16(q_ref[:, g * dh:(g + 1) * dh])
        km = kmean_ref[g]
        km_hi = km.astype(BF16)
        km_lo = (km - km_hi.astype(F32)).astype(BF16)
        gate = (jnp.dot(km_hi, qt, preferred_element_type=F32)
                + jnp.dot(km_lo, qt, preferred_element_type=F32))
        gm = jnp.where(past, gate, -jnp.inf)
        rank = jnp.zeros((nbp, blk), jnp.int32)
        for n in range(nb):
            gn = gm[n:n + 1, :]
            beats = (gn > gm) | ((gn == gm) & (blk_id > n))
            rank = rank + beats.astype(jnp.int32)
        sel_ref[g] = jnp.where(past & (rank < min(MOBA_TOPK, nb)), 0.0, MOBA_MASKED)
        slope = slopes_ref[hg * heads + g]
        qts.append(qt)
        slopes.append(slope)
        colbias.append(keyf * slope)

    def tile(b0, carries, gates, mask):
        start = pl.multiple_of(b0 * blk, blk)
        sts = [jnp.dot(k_ref[pl.ds(start, tk), g * dh:(g + 1) * dh], qts[g], preferred_element_type=F32)
               for g in range(heads)]
        biased = []
        for g in range(heads):
            shift = jnp.full((1, blk), (b0 - i) * blk, jnp.int32).astype(F32) * slopes[g]
            st = sts[g] + colbias[g]
            st = jnp.concatenate([st[:blk] + (gates[g][0] + shift), st[blk:] + (gates[g][1] + shift)], axis=0)
            biased.append(st if mask is None else jnp.where(mask, st, -jnp.inf))
        vts = [vt_ref[g * dh:(g + 1) * dh, pl.ds(start, tk)] for g in range(heads)]
        return _softmax_steps_t(biased, vts, carries)

    def body(n, carries):
        gates = [(sel_ref[g, pl.ds(2 * n, 1), :], sel_ref[g, pl.ds(2 * n + 1, 1), :]) for g in range(heads)]
        return tile(2 * n, carries, gates, None)

    init = tuple(_softmax_init(dh, blk) for _ in range(heads))
    npairs = i // 2
    carries = lax.fori_loop(0, npairs, body, init)
    b0 = jnp.maximum(i - 1, 0)
    zero = jnp.zeros((1, blk), F32)
    gates = [(jnp.where(i % 2 == 1, sel_ref[g, pl.ds(b0, 1), :], zero), zero) for g in range(heads)]
    mask = (key + b0 * blk <= qry + i * blk) & (key + b0 * blk >= npairs * tk)
    carries = tile(b0, carries, gates, mask)
    for g in range(heads):
        _, l, acc = carries[g]
        o_ref[:, g * dh:(g + 1) * dh] = (acc / l).T.astype(o_ref.dtype)


def moba_attention(z, slopes, *, batch, seq, heads):
    blk = MOBA_BLOCK
    w = MOBA_HEAD_DIM * heads
    nb = seq // blk
    qo, ko, vo = 1024 // w, 2048 // w, 3072 // w
    return pl.pallas_call(
        functools.partial(_moba_kernel, nb=nb, heads=heads),
        grid=(batch, MOBA_HEADS // heads, nb),
        in_specs=[
            pl.BlockSpec(memory_space=pltpu.SMEM),
            pl.BlockSpec((None, blk, w), lambda b, h, i: (b, i, qo + h)),
            pl.BlockSpec((None, seq, w), lambda b, h, i: (b, 0, ko + h)),
            pl.BlockSpec((None, seq, w), lambda b, h, i: (b, 0, vo + h)),
        ],
        out_specs=pl.BlockSpec((None, blk, w), lambda b, h, i: (b, i, h)),
        out_shape=jax.ShapeDtypeStruct((batch, seq, MOBA_HEADS * MOBA_HEAD_DIM), BF16),
        scratch_shapes=[
            pltpu.VMEM((w, seq), BF16),
            pltpu.VMEM((heads, max(MOBA_NB_PAD, nb), MOBA_HEAD_DIM), F32),
            pltpu.VMEM((heads, max(MOBA_NB_PAD, nb), blk), F32),
        ],
        compiler_params=_params(("parallel", "parallel", "arbitrary")),
        name="moba_attention",
    )(slopes, z, z, z)


def _band_mask(j, max_dist):
    row = lax.broadcasted_iota(jnp.int32, (BAND, 2 * BAND), 0)
    col = lax.broadcasted_iota(jnp.int32, (BAND, 2 * BAND), 1)
    dist = BAND + row - col
    mask = (dist >= 0) & (dist <= max_dist) & ((col >= BAND) | (j > 0))
    return mask, dist.astype(F32)


def _dil_kernel(slopes_ref, q_ref, k_ref, v_ref, o_ref, lse_ref, stage_ref, qd_ref, kd_ref, vd_ref, *, group):
    hb = pl.program_id(1)
    j = pl.program_id(2)
    rate = DIL_RATES[group]
    dh = DIL_HEAD_DIM
    nh = q_ref.shape[-1] // dh
    slot = j % 2
    prev = 1 - slot

    @pl.when(j == 0)
    def _():
        kd_ref[1] = jnp.zeros(kd_ref.shape[1:], BF16)
        vd_ref[1] = jnp.zeros(vd_ref.shape[1:], BF16)

    def split(src_ref, store):
        if rate == 1:
            store(0, src_ref[...])
            return
        for hh in range(nh):
            stage_ref[0, hh] = src_ref[:, hh * dh:(hh + 1) * dh].astype(F32)
        for r in range(rate):
            parts = [stage_ref[0, hh, pl.ds(r, BAND, stride=rate), :].astype(BF16) for hh in range(nh)]
            store(r, jnp.concatenate(parts, axis=1))

    def store_q(r, x):
        qd_ref[r] = x

    def store_k(r, x):
        kd_ref[slot, r] = x

    def store_v(r, x):
        vd_ref[slot, r] = x

    split(q_ref, store_q)
    split(k_ref, store_k)
    split(v_ref, store_v)

    mask, distf = _band_mask(j, DIL_WINDOWS[group] // rate)
    biases = []
    for hh in range(nh):
        slope = slopes_ref[group * DIL_HEADS_PER_GROUP + hb * nh + hh] * rate
        biases.append(jnp.where(mask, -slope * distf, -jnp.inf))

    def sub(r, c):
        q = qd_ref[r]
        kw = jnp.concatenate([kd_ref[prev, r], kd_ref[slot, r]], axis=0)
        vw = jnp.concatenate([vd_ref[prev, r], vd_ref[slot, r]], axis=0)
        sls = [slice(hh * dh, (hh + 1) * dh) for hh in range(nh)]
        ss = [_qk(q[:, sl], kw[:, sl]) + biases[hh] for hh, sl in enumerate(sls)]
        stats = []
        for s in ss:
            m = jnp.max(s, axis=-1, keepdims=True)
            p = jnp.exp2(s - m)
            stats.append((m, jnp.sum(p, axis=-1, keepdims=True), p.astype(BF16)))
        for hh, ((m, l, p), sl) in enumerate(zip(stats, sls)):
            out = jnp.dot(p, vw[:, sl], preferred_element_type=F32) / l
            lse = jnp.broadcast_to(m + jnp.log2(l), (BAND, dh))
            if rate == 1:
                o_ref[:, sl] = out
                lse_ref[:, sl] = lse
            else:
                stage_ref[0, hh, pl.ds(r, BAND, stride=rate), :] = out
                stage_ref[1, hh, pl.ds(r, BAND, stride=rate), :] = lse
        return c

    lax.fori_loop(0, rate, sub, 0)
    if rate > 1:
        for hh in range(nh):
            o_ref[:, hh * dh:(hh + 1) * dh] = stage_ref[0, hh]
            lse_ref[:, hh * dh:(hh + 1) * dh] = stage_ref[1, hh]


def dilated_group(z, slopes, *, group, batch, seq, heads):
    rate = DIL_RATES[group]
    tile = BAND * rate
    gw = DIL_HEADS_PER_GROUP * DIL_HEAD_DIM
    w = heads * DIL_HEAD_DIM
    nhb = DIL_HEADS_PER_GROUP // heads
    qo, ko, vo = ((2 + group) * gw // w, (5 + group) * gw // w, (8 + group) * gw // w)

    def spec(off):
        return pl.BlockSpec((None, tile, w), lambda b, h, j: (b, j, off + h))

    out_spec = pl.BlockSpec((None, tile, w), lambda b, h, j: (b, j, h))
    o, lse = pl.pallas_call(
        functools.partial(_dil_kernel, group=group),
        grid=(batch, nhb, seq // tile),
        in_specs=[pl.BlockSpec(memory_space=pltpu.SMEM), spec(qo), spec(ko), spec(vo)],
        out_specs=[out_spec, out_spec],
        out_shape=[jax.ShapeDtypeStruct((batch, seq, gw), F32)] * 2,
        scratch_shapes=[
            pltpu.VMEM((2, heads, tile, DIL_HEAD_DIM), F32),
            pltpu.VMEM((rate, BAND, w), BF16),
            pltpu.VMEM((2, rate, BAND, w), BF16),
            pltpu.VMEM((2, rate, BAND, w), BF16),
        ],
        compiler_params=_params(("parallel", "parallel", "arbitrary")),
        name=f"dilated_group{group}",
    )(slopes, z, z, z)
    return o.reshape(batch * seq, gw), lse.reshape(batch * seq, gw)


def _dil_merge_kernel(o0_ref, l0_ref, o1_ref, l1_ref, o2_ref, l2_ref, out_ref):
    l0, l1, l2 = l0_ref[...], l1_ref[...], l2_ref[...]
    mx = jnp.maximum(jnp.maximum(l0, l1), l2)
    w0, w1, w2 = jnp.exp2(l0 - mx), jnp.exp2(l1 - mx), jnp.exp2(l2 - mx)
    num = o0_ref[...] * w0 + o1_ref[...] * w1 + o2_ref[...] * w2
    out_ref[...] = (num / (w0 + w1 + w2)).astype(out_ref.dtype)


def dilated_merge(parts, *, tm):
    t, w = parts[0].shape
    spec = pl.BlockSpec((tm, w), lambda i: (i, 0))
    return pl.pallas_call(
        _dil_merge_kernel,
        grid=(t // tm,),
        in_specs=[spec] * 6,
        out_specs=spec,
        out_shape=jax.ShapeDtypeStruct((t, w), BF16),
        compiler_params=_params(("parallel",)),
        name="dilated_merge",
    )(*parts)


def _swa_kernel(sinks_ref, q_ref, kp_ref, ko_ref, vp_ref, vo_ref, o_ref, bias_ref):
    j = pl.program_id(1)
    heads_per_kv = SWA_Q_HEADS // SWA_KV_HEADS
    half = LANES // 2

    @pl.when(j == 0)
    def _():
        for later in (0, 1):
            mask, distf = _band_mask(later, SWA_WINDOW - 1)
            for head in range(SWA_Q_HEADS):
                slope = 2.0 ** (-8.0 * (head + 1) / SWA_Q_HEADS) * LOG2E
                bias_ref[later, head] = jnp.where(mask, -slope * distf, -jnp.inf)

    later = jnp.minimum(j, 1)

    kw = jnp.concatenate([kp_ref[...], ko_ref[...]], axis=0).astype(F32)
    vw = jnp.concatenate([vp_ref[...], vo_ref[...]], axis=0).astype(F32)
    low = lax.broadcasted_iota(jnp.int32, (2 * BAND, LANES), 1) < half

    def halves(x):
        xr = pltpu.roll(x, half, 1)
        even = (jnp.where(low, x, 0.0).astype(BF16), jnp.where(low, xr, 0.0).astype(BF16))
        odd = (jnp.where(low, 0.0, xr).astype(BF16), jnp.where(low, 0.0, x).astype(BF16))
        return even, odd

    k_even, k_odd = halves(kw)
    v_even, v_odd = halves(vw)

    for kvh in range(SWA_KV_HEADS):
        heads = list(range(kvh * heads_per_kv, (kvh + 1) * heads_per_kv))
        ss = []
        for head in heads:
            slab = q_ref[:, (head // 2) * LANES:(head // 2 + 1) * LANES]
            kk = (k_even if head % 2 == 0 else k_odd)[kvh]
            ss.append(_qk(slab, kk) + bias_ref[later, head])
        stats = []
        for s in ss:
            m = jnp.max(s, axis=-1, keepdims=True)
            p = jnp.exp2(s - m)
            stats.append((m, jnp.sum(p, axis=-1, keepdims=True), p.astype(BF16)))
        outs = {}
        for head, (m, l, p) in zip(heads, stats):
            vv = (v_even if head % 2 == 0 else v_odd)[kvh]
            acc = jnp.dot(p, vv, preferred_element_type=F32)
            sk = sinks_ref[head] * LOG2E
            mx = jnp.maximum(m, sk)
            a = jnp.exp2(m - mx)
            outs[head] = acc * a / (l * a + jnp.exp2(sk - mx))
        for pair in range(heads[0] // 2, heads[-1] // 2 + 1):
            o_ref[:, pair * LANES:(pair + 1) * LANES] = (outs[2 * pair] + outs[2 * pair + 1]).astype(o_ref.dtype)


def swa_attention(z, sinks, *, batch, seq):
    wq = SWA_Q_HEADS * SWA_HEAD_DIM
    ko, vo = 5632 // LANES, 5760 // LANES

    def kvspec(off, prev):
        if prev:
            return pl.BlockSpec((None, BAND, LANES), lambda b, j: (b, jnp.maximum(j - 1, 0), off))
        return pl.BlockSpec((None, BAND, LANES), lambda b, j: (b, j, off))

    return pl.pallas_call(
        _swa_kernel,
        grid=(batch, seq // BAND),
        in_specs=[
            pl.BlockSpec(memory_space=pltpu.SMEM),
            pl.BlockSpec((None, BAND, wq), lambda b, j: (b, j, 0)),
            kvspec(ko, True), kvspec(ko, False), kvspec(vo, True), kvspec(vo, False),
        ],
        out_specs=pl.BlockSpec((None, BAND, wq), lambda b, j: (b, j, 0)),
        out_shape=jax.ShapeDtypeStruct((batch, seq, wq), BF16),
        scratch_shapes=[pltpu.VMEM((2, SWA_Q_HEADS, BAND, 2 * BAND), F32)],
        compiler_params=_params(("parallel", "arbitrary")),
        name="swa_attention",
    )(sinks, z, z, z, z, z)


def _rope_pad(w):
    half = MLA_ROPE_DIM // 2
    z = jnp.zeros(w.shape[:-1] + (half,), w.dtype)
    return jnp.concatenate([w[..., :half], z, w[..., half:], z], axis=-1)


def _even_weights(w_in, w_uq, w_ukv, w_out):
    hd = MOBA_HEADS * MOBA_HEAD_DIM
    o = 0
    q_lat = w_in[:, o:o + MLA_Q_RANK]; o += MLA_Q_RANK
    kv_lat = w_in[:, o:o + MLA_KV_RANK]; o += MLA_KV_RANK
    k_pe = w_in[:, o:o + MLA_ROPE_DIM]; o += MLA_ROPE_DIM
    qb = w_in[:, o:o + hd] * (MOBA_HEAD_DIM ** -0.5 * LOG2E); o += hd
    kb = w_in[:, o:o + hd]; o += hd
    vb = w_in[:, o:o + hd]
    w_in_p = jnp.concatenate([q_lat, kv_lat, qb, kb, vb, _rope_pad(k_pe)], axis=-1).astype(BF16)

    r = w_uq.shape[0]
    uq = w_uq.reshape(r, MLA_HEADS, MLA_NOPE_DIM + MLA_ROPE_DIM) * ((MLA_NOPE_DIM + MLA_ROPE_DIM) ** -0.5 * LOG2E)
    wq = jnp.concatenate([uq[..., :MLA_NOPE_DIM], _rope_pad(uq[..., MLA_NOPE_DIM:])], axis=-1)
    wq = wq.reshape(r, MLA_HEADS * MLA_QK_PAD).astype(BF16)
    ukv = w_ukv.reshape(w_ukv.shape[0], MLA_HEADS, MLA_NOPE_DIM + MLA_V_DIM)
    wk = ukv[..., :MLA_NOPE_DIM].reshape(-1, MLA_HEADS * MLA_NOPE_DIM).astype(BF16)
    wv = ukv[..., MLA_NOPE_DIM:].reshape(-1, MLA_HEADS * MLA_V_DIM).astype(BF16)
    na = MLA_HEADS * MLA_V_DIM
    return w_in_p, wq, wk, wv, w_out[:na].astype(BF16), w_out[na:].astype(BF16)


def _odd_weights(w_in, w_out):
    cd = N_DIL * DIL_HEADS_PER_GROUP * DIL_HEAD_DIM
    qd_w = SWA_Q_HEADS * SWA_HEAD_DIM
    kd_w = SWA_KV_HEADS * SWA_HEAD_DIM
    o = 0
    qc = w_in[:, o:o + cd] * (DIL_HEAD_DIM ** -0.5 * LOG2E); o += cd
    kc = w_in[:, o:o + cd]; o += cd
    vc = w_in[:, o:o + cd]; o += cd
    qd = w_in[:, o:o + qd_w] * (SWA_HEAD_DIM ** -0.5 * LOG2E); o += qd_w
    kd = w_in[:, o:o + kd_w]; o += kd_w
    vd = w_in[:, o:o + kd_w]
    w_in_p = jnp.concatenate([qd, qc, kc, vc, kd, vd], axis=-1).astype(BF16)
    nc = DIL_HEADS_PER_GROUP * DIL_HEAD_DIM
    return w_in_p, w_out[:nc].astype(BF16), w_out[nc:].astype(BF16)


def _rope_tables(seq):
    half = MLA_ROPE_DIM // 2
    inv_freq = ROPE_THETA ** (-jnp.arange(half, dtype=F32) / half)
    ang = jnp.arange(seq, dtype=jnp.int32).astype(F32)[:, None] * inv_freq[None, :]
    c, s = jnp.cos(ang), jnp.sin(ang)
    z = jnp.zeros_like(c)
    return jnp.concatenate([c, z, c, z], axis=-1), jnp.concatenate([-s, z, s, z], axis=-1)


def kernel(x, attn_norm, mlp_norm, w_up, w_down, ev_w_in, ev_q_norm, ev_w_uq, ev_kv_norm, ev_w_ukv, ev_w_out,
           od_w_in, od_sinks, od_w_out, final_norm):
    batch, seq, d = x.shape
    depth = attn_norm.shape[0]
    t = batch * seq
    xt = x.reshape(t, d)
    cos, sin = _rope_tables(seq)
    moba_slopes = jnp.exp2(-8.0 * jnp.arange(1, MOBA_HEADS + 1, dtype=F32) / MOBA_HEADS) * LOG2E
    n_dil_heads = N_DIL * DIL_HEADS_PER_GROUP
    dil_slopes = jnp.exp2(-8.0 * jnp.arange(1, n_dil_heads + 1, dtype=F32) / n_dil_heads) * LOG2E
    fg = final_norm.reshape(1, d)

    for layer in range(depth):
        i = layer // 2
        g = attn_norm[layer].reshape(1, d)
        if layer % 2 == 0:
            w_in_p, wq, wk, wv, wo_a, wo_b = _even_weights(ev_w_in[i], ev_w_uq[i], ev_w_ukv[i], ev_w_out[i])
            z = norm_matmul(xt, g, w_in_p, tm=256)
            q, k, v = mla_up(z, ev_q_norm[i].reshape(1, -1), ev_kv_norm[i].reshape(1, -1), wq, wk, wv, cos, sin,
                             seq=seq, tm=256)
            a = mla_attention(q.reshape(batch, seq, -1), k.reshape(batch, seq, -1), v.reshape(batch, seq, -1),
                              batch=batch, seq=seq, blk=256, heads=8)
            b = moba_attention(z.reshape(batch, seq, EVEN_Z), moba_slopes, batch=batch, seq=seq, heads=8)
            xt = proj_residual(xt, a.reshape(t, -1), b.reshape(t, -1), wo_a, wo_b, tm=512)
        else:
            w_in_p, wo_c, wo_d = _odd_weights(od_w_in[i], od_w_out[i])
            z = norm_matmul(xt, g, w_in_p, tm=256)
            z3 = z.reshape(batch, seq, ODD_Z)
            parts = []
            for grp in range(N_DIL):
                parts.extend(dilated_group(z3, dil_slopes, group=grp, batch=batch, seq=seq,
                                           heads=DIL_GROUP_HEADS_PER_STEP[grp]))
            c = dilated_merge(parts, tm=1024)
            dd = swa_attention(z3, od_sinks[i], batch=batch, seq=seq)
            xt = proj_residual(xt, c, dd.reshape(t, -1), wo_c, wo_d, tm=512)
        xt = mlp_residual(xt, mlp_norm[layer].reshape(1, d), w_up[layer].astype(BF16), w_down[layer].astype(BF16), fg,
                          tm=512, tf=1024, final_norm=(layer == depth - 1))
    return xt.reshape(batch, seq, d)
```

```python
import functools

import jax
import jax.numpy as jnp
from jax import lax
from jax.experimental import pallas as pl
from jax.experimental.pallas import tpu as pltpu

F32 = jnp.float32
BF16 = jnp.bfloat16

NORM_EPS = 1e-6
D_FF_MULT = 4

MLA_HEADS = 8
MLA_Q_RANK = 512
MLA_KV_RANK = 512
MLA_NOPE_DIM = 128
MLA_ROPE_DIM = 64
MLA_V_DIM = 128
ROPE_THETA = 10000.0

MOBA_HEADS = 8
MOBA_HEAD_DIM = 128
MOBA_BLOCK = 256
MOBA_TOPK = 3

DIL_WINDOWS = (128, 512, 2048)
DIL_RATES = (1, 4, 16)
N_DIL = 3
DIL_HEADS_PER_GROUP = 4
DIL_HEAD_DIM = 128

SWA_Q_HEADS = 16
SWA_KV_HEADS = 2
SWA_HEAD_DIM = 64
SWA_WINDOW = 128

BAND = 128
DIL_GROUP_HEADS_PER_STEP = (4, 4, 2)
DIL_GROUP_CHUNKS = (4, 1, 1)

LANES = 128
MLA_QK_PAD = 256
EVEN_Z = 4224
ODD_Z = 5888
VMEM_LIMIT = 56 * 1024 * 1024
LOG2E = 1.4426950408889634


def _params(sem):
    return pltpu.CompilerParams(dimension_semantics=sem, vmem_limit_bytes=VMEM_LIMIT)


def _rms(x, g):
    ms = jnp.mean(x * x, axis=-1, keepdims=True)
    return x * lax.rsqrt(ms + NORM_EPS) * g


def _qk(q, k):
    return lax.dot_general(q, k, (((1,), (1,)), ((), ())), preferred_element_type=F32)


def _norm_matmul_kernel(x_ref, g_ref, w_ref, o_ref):
    h = _rms(x_ref[...], g_ref[...]).astype(BF16)
    o_ref[...] = jnp.dot(h, w_ref[...], preferred_element_type=F32).astype(o_ref.dtype)


def norm_matmul(x, g, w, *, tm):
    m, k = x.shape
    n = w.shape[1]
    return pl.pallas_call(
        _norm_matmul_kernel,
        grid=(m // tm,),
        in_specs=[
            pl.BlockSpec((tm, k), lambda i: (i, 0)),
            pl.BlockSpec((1, k), lambda i: (0, 0)),
            pl.BlockSpec((k, n), lambda i: (0, 0), pipeline_mode=pl.Buffered(1)),
        ],
        out_specs=pl.BlockSpec((tm, n), lambda i: (i, 0)),
        out_shape=jax.ShapeDtypeStruct((m, n), BF16),
        compiler_params=_params(("parallel",)),
        name="norm_matmul",
    )(x, g, w)


def _mlp_kernel(x_ref, g_ref, wu_ref, wd_ref, fg_ref, o_ref, h_ref, acc_ref, *, final_norm):
    f = pl.program_id(1)

    @pl.when(f == 0)
    def _():
        h_ref[...] = _rms(x_ref[...], g_ref[...]).astype(BF16)
        acc_ref[...] = jnp.zeros_like(acc_ref)

    u = jnp.dot(h_ref[...], wu_ref[...], preferred_element_type=F32)
    u = jnp.maximum(u, 0.0)
    u = u * u
    acc_ref[...] += jnp.dot(u.astype(BF16), wd_ref[...], preferred_element_type=F32)

    @pl.when(f == pl.num_programs(1) - 1)
    def _():
        y = x_ref[...] + acc_ref[...]
        if final_norm:
            y = _rms(y, fg_ref[...])
        o_ref[...] = y


def mlp_residual(x, g, w_up, w_down, final_g, *, tm, tf, final_norm):
    m, d = x.shape
    ff = w_up.shape[1]
    return pl.pallas_call(
        functools.partial(_mlp_kernel, final_norm=final_norm),
        grid=(m // tm, ff // tf),
        in_specs=[
            pl.BlockSpec((tm, d), lambda i, f: (i, 0)),
            pl.BlockSpec((1, d), lambda i, f: (0, 0)),
            pl.BlockSpec((d, tf), lambda i, f: (0, f)),
            pl.BlockSpec((tf, d), lambda i, f: (f, 0)),
            pl.BlockSpec((1, d), lambda i, f: (0, 0)),
        ],
        out_specs=pl.BlockSpec((tm, d), lambda i, f: (i, 0)),
        out_shape=jax.ShapeDtypeStruct((m, d), F32),
        scratch_shapes=[pltpu.VMEM((tm, d), BF16), pltpu.VMEM((tm, d), F32)],
        compiler_params=_params(("parallel", "arbitrary")),
        name="mlp_residual",
    )(x, g, w_up, w_down, final_g)


def _proj_residual_kernel(x_ref, a_ref, b_ref, wa_ref, wb_ref, o_ref):
    y = jnp.dot(a_ref[...], wa_ref[...], preferred_element_type=F32)
    y = y + jnp.dot(b_ref[...], wb_ref[...], preferred_element_type=F32)
    o_ref[...] = x_ref[...] + y


def proj_residual(x, a, b, wa, wb, *, tm):
    m, d = x.shape
    ka, kb = a.shape[1], b.shape[1]
    return pl.pallas_call(
        _proj_residual_kernel,
        grid=(m // tm,),
        in_specs=[
            pl.BlockSpec((tm, d), lambda i: (i, 0)),
            pl.BlockSpec((tm, ka), lambda i: (i, 0)),
            pl.BlockSpec((tm, kb), lambda i: (i, 0)),
            pl.BlockSpec((ka, d), lambda i: (0, 0)),
            pl.BlockSpec((kb, d), lambda i: (0, 0)),
        ],
        out_specs=pl.BlockSpec((tm, d), lambda i: (i, 0)),
        out_shape=jax.ShapeDtypeStruct((m, d), F32),
        compiler_params=_params(("parallel",)),
        name="proj_residual",
    )(x, a, b, wa, wb)


def _mla_up_kernel(ql_ref, kvl_ref, kpe_ref, qn_ref, kvn_ref, wq_ref, wk_ref, wv_ref, cos_ref, sin_ref,
                   q_ref, k_ref, v_ref):
    cos = cos_ref[...]
    sin = sin_ref[...]

    def rope(t):
        return t * cos + pltpu.roll(t, LANES // 2, 1) * sin

    hq = _rms(ql_ref[...].astype(F32), qn_ref[...]).astype(BF16)
    yq = jnp.dot(hq, wq_ref[...], preferred_element_type=F32)
    hkv = _rms(kvl_ref[...].astype(F32), kvn_ref[...]).astype(BF16)
    yk = jnp.dot(hkv, wk_ref[...], preferred_element_type=F32)
    yv = jnp.dot(hkv, wv_ref[...], preferred_element_type=F32)
    krot = rope(kpe_ref[...].astype(F32)).astype(BF16)
    for h in range(MLA_HEADS):
        lo = h * MLA_QK_PAD
        mid = lo + MLA_NOPE_DIM
        hi = lo + MLA_QK_PAD
        q_ref[:, lo:mid] = yq[:, lo:mid].astype(BF16)
        q_ref[:, mid:hi] = rope(yq[:, mid:hi]).astype(BF16)
        k_ref[:, lo:mid] = yk[:, h * MLA_NOPE_DIM:(h + 1) * MLA_NOPE_DIM].astype(BF16)
        k_ref[:, mid:hi] = krot
    v_ref[...] = yv.astype(BF16)


def mla_up(z, qn, kvn, wq, wk, wv, cos, sin, *, seq, tm):
    t = z.shape[0]
    r = MLA_Q_RANK
    nq = MLA_HEADS * MLA_QK_PAD
    nv = MLA_HEADS * MLA_V_DIM
    kpe_blk = (EVEN_Z - LANES) // LANES
    pos_blocks = seq // tm
    return pl.pallas_call(
        _mla_up_kernel,
        grid=(t // tm,),
        in_specs=[
            pl.BlockSpec((tm, r), lambda i: (i, 0)),
            pl.BlockSpec((tm, r), lambda i: (i, 1)),
            pl.BlockSpec((tm, LANES), lambda i: (i, kpe_blk)),
            pl.BlockSpec((1, r), lambda i: (0, 0)),
            pl.BlockSpec((1, r), lambda i: (0, 0)),
            pl.BlockSpec((r, nq), lambda i: (0, 0)),
            pl.BlockSpec((r, nv), lambda i: (0, 0)),
            pl.BlockSpec((r, nv), lambda i: (0, 0)),
            pl.BlockSpec((tm, LANES), lambda i: (i % pos_blocks, 0)),
            pl.BlockSpec((tm, LANES), lambda i: (i % pos_blocks, 0)),
        ],
        out_specs=[
            pl.BlockSpec((tm, nq), lambda i: (i, 0)),
            pl.BlockSpec((tm, nq), lambda i: (i, 0)),
            pl.BlockSpec((tm, nv), lambda i: (i, 0)),
        ],
        out_shape=[
            jax.ShapeDtypeStruct((t, nq), BF16),
            jax.ShapeDtypeStruct((t, nq), BF16),
            jax.ShapeDtypeStruct((t, nv), BF16),
        ],
        compiler_params=_params(("parallel",)),
        name="mla_up",
    )(z, z, z, qn, kvn, wq, wk, wv, cos, sin)


def _softmax_steps_t(sts, vts, carries):
    stats = []
    for st, (m, l, _) in zip(sts, carries):
        m_new = jnp.maximum(m, jnp.max(st, axis=0, keepdims=True))
        p = jnp.exp2(st - m_new)
        alpha = jnp.exp2(m - m_new)
        stats.append((m_new, alpha * l + jnp.sum(p, axis=0, keepdims=True), alpha, p.astype(BF16)))
    return tuple((m_new, l_new, alpha * acc + jnp.dot(vt, p, preferred_element_type=F32))
                 for (m_new, l_new, alpha, p), vt, (_, _, acc) in zip(stats, vts, carries))


def _softmax_init(dv, nq):
    return (jnp.full((1, nq), -jnp.inf, F32), jnp.zeros((1, nq), F32), jnp.zeros((dv, nq), F32))


def _transpose_bf16(x):
    return x.astype(F32).T.astype(BF16)


def _mla_attn_kernel(q_ref, k_ref, v_ref, o_ref, vt_ref, *, blk, heads):
    qi = pl.program_id(2)
    dq, dv = MLA_QK_PAD, MLA_V_DIM

    @pl.when(qi == 0)
    def _():
        def fill(n, c):
            start = pl.multiple_of(n * blk, blk)
            for g in range(heads):
                vt_ref[g * dv:(g + 1) * dv, pl.ds(start, blk)] = _transpose_bf16(
                    v_ref[pl.ds(start, blk), g * dv:(g + 1) * dv])
            return c

        lax.fori_loop(0, k_ref.shape[0] // blk, fill, 0)

    qts = [_transpose_bf16(q_ref[:, g * dq:(g + 1) * dq]) for g in range(heads)]
    tk = 2 * blk

    def tile(start, carries, mask):
        sts = [jnp.dot(k_ref[pl.ds(start, tk), g * dq:(g + 1) * dq], qts[g], preferred_element_type=F32)
               for g in range(heads)]
        if mask is not None:
            sts = [jnp.where(mask, st, -jnp.inf) for st in sts]
        vts = [vt_ref[g * dv:(g + 1) * dv, pl.ds(start, tk)] for g in range(heads)]
        return _softmax_steps_t(sts, vts, carries)

    init = tuple(_softmax_init(dv, blk) for _ in range(heads))
    npairs = qi // 2
    carries = lax.fori_loop(0, npairs, lambda n, c: tile(pl.multiple_of(n * tk, tk), c, None), init)
    start = pl.multiple_of(jnp.maximum(qi - 1, 0) * blk, blk)
    key_pos = start + lax.broadcasted_iota(jnp.int32, (tk, blk), 0)
    qry_pos = qi * blk + lax.broadcasted_iota(jnp.int32, (tk, blk), 1)
    carries = tile(start, carries, (key_pos <= qry_pos) & (key_pos >= npairs * tk))
    for g in range(heads):
        _, l, acc = carries[g]
        o_ref[:, g * dv:(g + 1) * dv] = (acc / l).T.astype(o_ref.dtype)


def mla_attention(q, k, v, *, batch, seq, blk, heads):
    dq = MLA_QK_PAD * heads
    dv = MLA_V_DIM * heads
    return pl.pallas_call(
        functools.partial(_mla_attn_kernel, blk=blk, heads=heads),
        grid=(batch, MLA_HEADS // heads, seq // blk),
        in_specs=[
            pl.BlockSpec((None, blk, dq), lambda b, h, i: (b, i, h)),
            pl.BlockSpec((None, seq, dq), lambda b, h, i: (b, 0, h), pipeline_mode=pl.Buffered(1)),
            pl.BlockSpec((None, seq, dv), lambda b, h, i: (b, 0, h), pipeline_mode=pl.Buffered(1)),
        ],
        out_specs=pl.BlockSpec((None, blk, dv), lambda b, h, i: (b, i, h)),
        out_shape=jax.ShapeDtypeStruct((batch, seq, MLA_HEADS * MLA_V_DIM), BF16),
        scratch_shapes=[pltpu.VMEM((dv, seq), BF16)],
        compiler_params=_params(("parallel", "parallel", "arbitrary")),
        name="mla_attention",
    )(q, k, v)


MOBA_NB_PAD = 16
MOBA_MASKED = -1e30


def _moba_kernel(slopes_ref, q_ref, k_ref, v_ref, o_ref, vt_ref, kmean_ref, sel_ref, *, nb, heads):
    blk = MOBA_BLOCK
    dh = MOBA_HEAD_DIM
    nbp = kmean_ref.shape[1]
    hg = pl.program_id(1)
    i = pl.program_id(2)

    @pl.when(i == 0)
    def _():
        kmean_ref[...] = jnp.zeros_like(kmean_ref)

        def fill(n, c):
            start = pl.multiple_of(n * blk, blk)
            for g in range(heads):
                kn = k_ref[pl.ds(start, blk), g * dh:(g + 1) * dh].astype(F32)
                kmean_ref[g, pl.ds(n, 1), :] = jnp.mean(kn, axis=0, keepdims=True)
                vt_ref[g * dh:(g + 1) * dh, pl.ds(start, blk)] = _transpose_bf16(
                    v_ref[pl.ds(start, blk), g * dh:(g + 1) * dh])
            return c

        lax.fori_loop(0, nb, fill, 0)

    blk_id = lax.broadcasted_iota(jnp.int32, (nbp, blk), 0)
    past = blk_id < i
    tk = 2 * blk
    key = lax.broadcasted_iota(jnp.int32, (tk, blk), 0)
    qry = lax.broadcasted_iota(jnp.int32, (tk, blk), 1)
    keyf = key.astype(F32)

    qts, colbias, slopes = [], [], []
    for g in range(heads):
        qt = _transpose_bf16(q_ref[:, g * dh:(g + 1) * dh])
        km = kmean_ref[g]
        km_hi = km.astype(BF16)
        km_lo = (km - km_hi.astype(F32)).astype(BF16)
        gate = (jnp.dot(km_hi, qt, preferred_element_type=F32)
                + jnp.dot(km_lo, qt, preferred_element_type=F32))
        gm = jnp.where(past, gate, -jnp.inf)
        rank = jnp.zeros((nbp, blk), jnp.int32)
        for n in range(nb):
            gn = gm[n:n + 1, :]
            beats = (gn > gm) | ((gn == gm) & (blk_id > n))
            rank = rank + beats.astype(jnp.int32)
        sel_ref[g] = jnp.where(past & (rank < min(MOBA_TOPK, nb)), 0.0, MOBA_MASKED)
        slope = slopes_ref[hg * heads + g]
        qts.append(qt)
        slopes.append(slope)
        colbias.append(keyf * slope)

    def tile(b0, carries, gates, mask):
        start = pl.multiple_of(b0 * blk, blk)
        sts = [jnp.dot(k_ref[pl.ds(start, tk), g * dh:(g + 1) * dh], qts[g], preferred_element_type=F32)
               for g in range(heads)]
        biased = []
        for g in range(heads):
            shift = jnp.full((1, blk), (b0 - i) * blk, jnp.int32).astype(F32) * slopes[g]
            st = sts[g] + colbias[g]
            st = jnp.concatenate([st[:blk] + (gates[g][0] + shift), st[blk:] + (gates[g][1] + shift)], axis=0)
            biased.append(st if mask is None else jnp.where(mask, st, -jnp.inf))
        vts = [vt_ref[g * dh:(g + 1) * dh, pl.ds(start, tk)] for g in range(heads)]
        return _softmax_steps_t(biased, vts, carries)

    def body(n, carries):
        gates = [(sel_ref[g, pl.ds(2 * n, 1), :], sel_ref[g, pl.ds(2 * n + 1, 1), :]) for g in range(heads)]
        return tile(2 * n, carries, gates, None)

    init = tuple(_softmax_init(dh, blk) for _ in range(heads))
    npairs = i // 2
    carries = lax.fori_loop(0, npairs, body, init)
    b0 = jnp.maximum(i - 1, 0)
    zero = jnp.zeros((1, blk), F32)
    gates = [(jnp.where(i % 2 == 1, sel_ref[g, pl.ds(b0, 1), :], zero), zero) for g in range(heads)]
    mask = (key + b0 * blk <= qry + i * blk) & (key + b0 * blk >= npairs * tk)
    carries = tile(b0, carries, gates, mask)
    for g in range(heads):
        _, l, acc = carries[g]
        o_ref[:, g * dh:(g + 1) * dh] = (acc / l).T.astype(o_ref.dtype)


def moba_attention(z, slopes, *, batch, seq, heads):
    blk = MOBA_BLOCK
    w = MOBA_HEAD_DIM * heads
    nb = seq // blk
    qo, ko, vo = 1024 // w, 2048 // w, 3072 // w
    return pl.pallas_call(
        functools.partial(_moba_kernel, nb=nb, heads=heads),
        grid=(batch, MOBA_HEADS // heads, nb),
        in_specs=[
            pl.BlockSpec(memory_space=pltpu.SMEM),
            pl.BlockSpec((None, blk, w), lambda b, h, i: (b, i, qo + h)),
            pl.BlockSpec((None, seq, w), lambda b, h, i: (b, 0, ko + h)),
            pl.BlockSpec((None, seq, w), lambda b, h, i: (b, 0, vo + h)),
        ],
        out_specs=pl.BlockSpec((None, blk, w), lambda b, h, i: (b, i, h)),
        out_shape=jax.ShapeDtypeStruct((batch, seq, MOBA_HEADS * MOBA_HEAD_DIM), BF16),
        scratch_shapes=[
            pltpu.VMEM((w, seq), BF16),
            pltpu.VMEM((heads, max(MOBA_NB_PAD, nb), MOBA_HEAD_DIM), F32),
            pltpu.VMEM((heads, max(MOBA_NB_PAD, nb), blk), F32),
        ],
        compiler_params=_params(("parallel", "parallel", "arbitrary")),
        name="moba_attention",
    )(slopes, z, z, z)


def _band_mask(j, max_dist):
    row = lax.broadcasted_iota(jnp.int32, (BAND, 2 * BAND), 0)
    col = lax.broadcasted_iota(jnp.int32, (BAND, 2 * BAND), 1)
    dist = BAND + row - col
    mask = (dist >= 0) & (dist <= max_dist) & ((col >= BAND) | (j > 0))
    return mask, dist.astype(F32)


def _dil_kernel(slopes_ref, q_ref, k_ref, v_ref, o_ref, lse_ref, stage_ref, qd_ref, kd_ref, vd_ref, *, group):
    hb = pl.program_id(1)
    j = pl.program_id(2)
    rate = DIL_RATES[group]
    dh = DIL_HEAD_DIM
    nh = q_ref.shape[-1] // dh
    sup = BAND * rate
    chunks = q_ref.shape[0] // sup

    @pl.when(j == 0)
    def _():
        kd_ref[0] = jnp.zeros(kd_ref.shape[1:], BF16)
        vd_ref[0] = jnp.zeros(vd_ref.shape[1:], BF16)

    def split(src_ref, store):
        if rate == 1:
            for c in range(chunks):
                store(c, 0, src_ref[c * BAND:(c + 1) * BAND, :])
            return
        for hh in range(nh):
            stage_ref[0, hh] = src_ref[:, hh * dh:(hh + 1) * dh].astype(F32)
        for c in range(chunks):
            for r in range(rate):
                rows = pl.ds(c * sup + r, BAND, stride=rate)
                store(c, r, jnp.concatenate([stage_ref[0, hh, rows, :].astype(BF16) for hh in range(nh)], axis=1))

    def store_q(c, r, x):
        qd_ref[c, r] = x

    def store_k(c, r, x):
        kd_ref[1 + c, r] = x

    def store_v(c, r, x):
        vd_ref[1 + c, r] = x

    split(q_ref, store_q)
    split(k_ref, store_k)
    split(v_ref, store_v)

    def biases(first_flag):
        mask, distf = _band_mask(first_flag, DIL_WINDOWS[group] // rate)
        out = []
        for hh in range(nh):
            slope = slopes_ref[group * DIL_HEADS_PER_GROUP + hb * nh + hh] * rate
            out.append(jnp.where(mask, -slope * distf, -jnp.inf))
        return out

    sls = [slice(hh * dh, (hh + 1) * dh) for hh in range(nh)]
    for c in range(chunks):
        bias = biases(j if c == 0 else 1)

        def sub(r, carry, c=c, bias=bias):
            q = qd_ref[c, r]
            kw = jnp.concatenate([kd_ref[c, r], kd_ref[c + 1, r]], axis=0)
            vw = jnp.concatenate([vd_ref[c, r], vd_ref[c + 1, r]], axis=0)
            ss = [_qk(q[:, sl], kw[:, sl]) + bias[hh] for hh, sl in enumerate(sls)]
            stats = []
            for s in ss:
                m = jnp.max(s, axis=-1, keepdims=True)
                p = jnp.exp2(s - m)
                stats.append((m, jnp.sum(p, axis=-1, keepdims=True), p.astype(BF16)))
            for hh, ((m, l, p), sl) in enumerate(zip(stats, sls)):
                out = jnp.dot(p, vw[:, sl], preferred_element_type=F32) / l
                lse = jnp.broadcast_to(m + jnp.log2(l), (BAND, dh))
                if rate == 1:
                    o_ref[c * BAND:(c + 1) * BAND, sl] = out
                    lse_ref[c * BAND:(c + 1) * BAND, sl] = lse
                else:
                    rows = pl.ds(c * sup + r, BAND, stride=rate)
                    stage_ref[0, hh, rows, :] = out
                    stage_ref[1, hh, rows, :] = lse
            return carry

        if rate == 1:
            sub(0, 0)
        else:
            lax.fori_loop(0, rate, sub, 0)

    if rate > 1:
        for hh in range(nh):
            o_ref[:, sls[hh]] = stage_ref[0, hh]
            lse_ref[:, sls[hh]] = stage_ref[1, hh]
    kd_ref[0] = kd_ref[chunks]
    vd_ref[0] = vd_ref[chunks]


def dilated_group(z, slopes, *, group, batch, seq, heads, chunks):
    rate = DIL_RATES[group]
    tile = BAND * rate * chunks
    gw = DIL_HEADS_PER_GROUP * DIL_HEAD_DIM
    w = heads * DIL_HEAD_DIM
    nhb = DIL_HEADS_PER_GROUP // heads
    qo, ko, vo = (group * gw // w, (N_DIL + group) * gw // w, (2 * N_DIL + group) * gw // w)
    stage_rows = tile if rate > 1 else 8

    def spec(off):
        return pl.BlockSpec((None, tile, w), lambda b, h, j: (b, j, off + h))

    out_spec = pl.BlockSpec((None, tile, w), lambda b, h, j: (b, j, h))
    o, lse = pl.pallas_call(
        functools.partial(_dil_kernel, group=group),
        grid=(batch, nhb, seq // tile),
        in_specs=[pl.BlockSpec(memory_space=pltpu.SMEM), spec(qo), spec(ko), spec(vo)],
        out_specs=[out_spec, out_spec],
        out_shape=[jax.ShapeDtypeStruct((batch, seq, gw), F32)] * 2,
        scratch_shapes=[
            pltpu.VMEM((2, heads, stage_rows, DIL_HEAD_DIM), F32),
            pltpu.VMEM((chunks, rate, BAND, w), BF16),
            pltpu.VMEM((chunks + 1, rate, BAND, w), BF16),
            pltpu.VMEM((chunks + 1, rate, BAND, w), BF16),
        ],
        compiler_params=_params(("parallel", "parallel", "arbitrary")),
        name=f"dilated_group{group}",
    )(slopes, z, z, z)
    return o.reshape(batch * seq, gw), lse.reshape(batch * seq, gw)


def _odd_proj_kernel(x_ref, o0_ref, l0_ref, o1_ref, l1_ref, o2_ref, l2_ref, d_ref, wc_ref, wd_ref, out_ref):
    l0, l1, l2 = l0_ref[...], l1_ref[...], l2_ref[...]
    mx = jnp.maximum(jnp.maximum(l0, l1), l2)
    w0, w1, w2 = jnp.exp2(l0 - mx), jnp.exp2(l1 - mx), jnp.exp2(l2 - mx)
    num = o0_ref[...] * w0 + o1_ref[...] * w1 + o2_ref[...] * w2
    c = (num / (w0 + w1 + w2)).astype(BF16)
    y = jnp.dot(c, wc_ref[...], preferred_element_type=F32)
    y = y + jnp.dot(d_ref[...], wd_ref[...], preferred_element_type=F32)
    out_ref[...] = x_ref[...] + y


def odd_proj_residual(x, parts, dd, wc, wd, *, tm):
    m, d = x.shape
    kc, kd = wc.shape[0], wd.shape[0]
    part_spec = pl.BlockSpec((tm, kc), lambda i: (i, 0))
    return pl.pallas_call(
        _odd_proj_kernel,
        grid=(m // tm,),
        in_specs=[pl.BlockSpec((tm, d), lambda i: (i, 0))] + [part_spec] * 6 + [
            pl.BlockSpec((tm, kd), lambda i: (i, 0)),
            pl.BlockSpec((kc, d), lambda i: (0, 0), pipeline_mode=pl.Buffered(1)),
            pl.BlockSpec((kd, d), lambda i: (0, 0), pipeline_mode=pl.Buffered(1)),
        ],
        out_specs=pl.BlockSpec((tm, d), lambda i: (i, 0)),
        out_shape=jax.ShapeDtypeStruct((m, d), F32),
        compiler_params=_params(("parallel",)),
        name="odd_proj_residual",
    )(x, *parts, dd, wc, wd)


def _swa_kernel(sinks_ref, qa_ref, qb_ref, kp_ref, ko_ref, vp_ref, vo_ref, o_ref, bias_ref):
    j = pl.program_id(1)
    heads_per_kv = SWA_Q_HEADS // SWA_KV_HEADS
    half = LANES // 2

    @pl.when(j == 0)
    def _():
        for later in (0, 1):
            mask, distf = _band_mask(later, SWA_WINDOW - 1)
            for head in range(SWA_Q_HEADS):
                slope = 2.0 ** (-8.0 * (head + 1) / SWA_Q_HEADS) * LOG2E
                bias_ref[later, head] = jnp.where(mask, -slope * distf, -jnp.inf)

    later = jnp.minimum(j, 1)

    kw = jnp.concatenate([kp_ref[...], ko_ref[...]], axis=0).astype(F32)
    vw = jnp.concatenate([vp_ref[...], vo_ref[...]], axis=0).astype(F32)
    low = lax.broadcasted_iota(jnp.int32, (2 * BAND, LANES), 1) < half

    def halves(x):
        xr = pltpu.roll(x, half, 1)
        even = (jnp.where(low, x, 0.0).astype(BF16), jnp.where(low, xr, 0.0).astype(BF16))
        odd = (jnp.where(low, 0.0, xr).astype(BF16), jnp.where(low, 0.0, x).astype(BF16))
        return even, odd

    k_even, k_odd = halves(kw)
    v_even, v_odd = halves(vw)

    for kvh in range(SWA_KV_HEADS):
        heads = list(range(kvh * heads_per_kv, (kvh + 1) * heads_per_kv))
        ss = []
        q_ref = (qa_ref, qb_ref)[kvh]
        for head in heads:
            pair = (head - heads[0]) // 2
            slab = q_ref[:, pair * LANES:(pair + 1) * LANES]
            kk = (k_even if head % 2 == 0 else k_odd)[kvh]
            ss.append(_qk(slab, kk) + bias_ref[later, head])
        stats = []
        for s in ss:
            m = jnp.max(s, axis=-1, keepdims=True)
            p = jnp.exp2(s - m)
            stats.append((m, jnp.sum(p, axis=-1, keepdims=True), p.astype(BF16)))
        outs = {}
        for head, (m, l, p) in zip(heads, stats):
            vv = (v_even if head % 2 == 0 else v_odd)[kvh]
            acc = jnp.dot(p, vv, preferred_element_type=F32)
            sk = sinks_ref[head] * LOG2E
            mx = jnp.maximum(m, sk)
            a = jnp.exp2(m - mx)
            outs[head] = acc * a / (l * a + jnp.exp2(sk - mx))
        for pair in range(heads[0] // 2, heads[-1] // 2 + 1):
            o_ref[:, pair * LANES:(pair + 1) * LANES] = (outs[2 * pair] + outs[2 * pair + 1]).astype(o_ref.dtype)


def swa_attention(z, sinks, *, batch, seq):
    wq = SWA_Q_HEADS * SWA_HEAD_DIM
    wq_kv = wq // SWA_KV_HEADS
    q0 = 3 * N_DIL * DIL_HEADS_PER_GROUP * DIL_HEAD_DIM
    qo, ko, vo = q0 // wq_kv, (q0 + wq) // LANES, (q0 + wq) // LANES + 1

    def kvspec(off, prev):
        if prev:
            return pl.BlockSpec((None, BAND, LANES), lambda b, j: (b, jnp.maximum(j - 1, 0), off))
        return pl.BlockSpec((None, BAND, LANES), lambda b, j: (b, j, off))

    return pl.pallas_call(
        _swa_kernel,
        grid=(batch, seq // BAND),
        in_specs=[
            pl.BlockSpec(memory_space=pltpu.SMEM),
            pl.BlockSpec((None, BAND, wq_kv), lambda b, j: (b, j, qo)),
            pl.BlockSpec((None, BAND, wq_kv), lambda b, j: (b, j, qo + 1)),
            kvspec(ko, True), kvspec(ko, False), kvspec(vo, True), kvspec(vo, False),
        ],
        out_specs=pl.BlockSpec((None, BAND, wq), lambda b, j: (b, j, 0)),
        out_shape=jax.ShapeDtypeStruct((batch, seq, wq), BF16),
        scratch_shapes=[pltpu.VMEM((2, SWA_Q_HEADS, BAND, 2 * BAND), F32)],
        compiler_params=_params(("parallel", "arbitrary")),
        name="swa_attention",
    )(sinks, z, z, z, z, z, z)


def _rope_pad(w):
    half = MLA_ROPE_DIM // 2
    z = jnp.zeros(w.shape[:-1] + (half,), w.dtype)
    return jnp.concatenate([w[..., :half], z, w[..., half:], z], axis=-1)


def _even_weights(w_in, w_uq, w_ukv, w_out):
    hd = MOBA_HEADS * MOBA_HEAD_DIM
    o = 0
    q_lat = w_in[:, o:o + MLA_Q_RANK]; o += MLA_Q_RANK
    kv_lat = w_in[:, o:o + MLA_KV_RANK]; o += MLA_KV_RANK
    k_pe = w_in[:, o:o + MLA_ROPE_DIM]; o += MLA_ROPE_DIM
    qb = w_in[:, o:o + hd] * (MOBA_HEAD_DIM ** -0.5 * LOG2E); o += hd
    kb = w_in[:, o:o + hd]; o += hd
    vb = w_in[:, o:o + hd]
    w_in_p = jnp.concatenate([q_lat, kv_lat, qb, kb, vb, _rope_pad(k_pe)], axis=-1).astype(BF16)

    r = w_uq.shape[0]
    uq = w_uq.reshape(r, MLA_HEADS, MLA_NOPE_DIM + MLA_ROPE_DIM) * ((MLA_NOPE_DIM + MLA_ROPE_DIM) ** -0.5 * LOG2E)
    wq = jnp.concatenate([uq[..., :MLA_NOPE_DIM], _rope_pad(uq[..., MLA_NOPE_DIM:])], axis=-1)
    wq = wq.reshape(r, MLA_HEADS * MLA_QK_PAD).astype(BF16)
    ukv = w_ukv.reshape(w_ukv.shape[0], MLA_HEADS, MLA_NOPE_DIM + MLA_V_DIM)
    wk = ukv[..., :MLA_NOPE_DIM].reshape(-1, MLA_HEADS * MLA_NOPE_DIM).astype(BF16)
    wv = ukv[..., MLA_NOPE_DIM:].reshape(-1, MLA_HEADS * MLA_V_DIM).astype(BF16)
    na = MLA_HEADS * MLA_V_DIM
    return w_in_p, wq, wk, wv, w_out[:na].astype(BF16), w_out[na:].astype(BF16)


def _odd_weights(w_in, w_out):
    cd = N_DIL * DIL_HEADS_PER_GROUP * DIL_HEAD_DIM
    qd_w = SWA_Q_HEADS * SWA_HEAD_DIM
    kd_w = SWA_KV_HEADS * SWA_HEAD_DIM
    scale = jnp.concatenate([
        jnp.full((cd,), DIL_HEAD_DIM ** -0.5 * LOG2E, F32), jnp.ones((2 * cd,), F32),
        jnp.full((qd_w,), SWA_HEAD_DIM ** -0.5 * LOG2E, F32), jnp.ones((2 * kd_w,), F32)])
    w_in_p = (w_in * scale[None, :]).astype(BF16)
    nc = DIL_HEADS_PER_GROUP * DIL_HEAD_DIM
    return w_in_p, w_out[:nc].astype(BF16), w_out[nc:].astype(BF16)


def _rope_tables(seq):
    half = MLA_ROPE_DIM // 2
    inv_freq = ROPE_THETA ** (-jnp.arange(half, dtype=F32) / half)
    ang = jnp.arange(seq, dtype=jnp.int32).astype(F32)[:, None] * inv_freq[None, :]
    c, s = jnp.cos(ang), jnp.sin(ang)
    z = jnp.zeros_like(c)
    return jnp.concatenate([c, z, c, z], axis=-1), jnp.concatenate([-s, z, s, z], axis=-1)


def kernel(x, attn_norm, mlp_norm, w_up, w_down, ev_w_in, ev_q_norm, ev_w_uq, ev_kv_norm, ev_w_ukv, ev_w_out,
           od_w_in, od_sinks, od_w_out, final_norm):
    batch, seq, d = x.shape
    depth = attn_norm.shape[0]
    t = batch * seq
    xt = x.reshape(t, d)
    cos, sin = _rope_tables(seq)
    moba_slopes = jnp.exp2(-8.0 * jnp.arange(1, MOBA_HEADS + 1, dtype=F32) / MOBA_HEADS) * LOG2E
    n_dil_heads = N_DIL * DIL_HEADS_PER_GROUP
    dil_slopes = jnp.exp2(-8.0 * jnp.arange(1, n_dil_heads + 1, dtype=F32) / n_dil_heads) * LOG2E
    fg = final_norm.reshape(1, d)

    for layer in range(depth):
        i = layer // 2
        g = attn_norm[layer].reshape(1, d)
        if layer % 2 == 0:
            w_in_p, wq, wk, wv, wo_a, wo_b = _even_weights(ev_w_in[i], ev_w_uq[i], ev_w_ukv[i], ev_w_out[i])
            z = norm_matmul(xt, g, w_in_p, tm=256)
            q, k, v = mla_up(z, ev_q_norm[i].reshape(1, -1), ev_kv_norm[i].reshape(1, -1), wq, wk, wv, cos, sin,
                             seq=seq, tm=256)
            a = mla_attention(q.reshape(batch, seq, -1), k.reshape(batch, seq, -1), v.reshape(batch, seq, -1),
                              batch=batch, seq=seq, blk=256, heads=8)
            b = moba_attention(z.reshape(batch, seq, EVEN_Z), moba_slopes, batch=batch, seq=seq, heads=8)
            xt = proj_residual(xt, a.reshape(t, -1), b.reshape(t, -1), wo_a, wo_b, tm=512)
        else:
            w_in_p, wo_c, wo_d = _odd_weights(od_w_in[i], od_w_out[i])
            z = norm_matmul(xt, g, w_in_p, tm=256)
            z3 = z.reshape(batch, seq, ODD_Z)
            parts = []
            for grp in range(N_DIL):
                parts.extend(dilated_group(z3, dil_slopes, group=grp, batch=batch, seq=seq,
                                           heads=DIL_GROUP_HEADS_PER_STEP[grp], chunks=DIL_GROUP_CHUNKS[grp]))
            dd = swa_attention(z3, od_sinks[i], batch=batch, seq=seq)
            xt = odd_proj_residual(xt, parts, dd.reshape(t, -1), wo_c, wo_d, tm=512)
        xt = mlp_residual(xt, mlp_norm[layer].reshape(1, d), w_up[layer].astype(BF16), w_down[layer].astype(BF16), fg,
                          tm=512, tf=1024, final_norm=(layer == depth - 1))
    return xt.reshape(batch, seq, d)
```

```python
import functools

import jax
import jax.numpy as jnp
from jax import lax
from jax.experimental import pallas as pl
from jax.experimental.pallas import tpu as pltpu

F32 = jnp.float32
BF16 = jnp.bfloat16

NORM_EPS = 1e-6
D_FF_MULT = 4

MLA_HEADS = 8
MLA_Q_RANK = 512
MLA_KV_RANK = 512
MLA_NOPE_DIM = 128
MLA_ROPE_DIM = 64
MLA_V_DIM = 128
ROPE_THETA = 10000.0

MOBA_HEADS = 8
MOBA_HEAD_DIM = 128
MOBA_BLOCK = 256
MOBA_TOPK = 3

DIL_WINDOWS = (128, 512, 2048)
DIL_RATES = (1, 4, 16)
N_DIL = 3
DIL_HEADS_PER_GROUP = 4
DIL_HEAD_DIM = 128

SWA_Q_HEADS = 16
SWA_KV_HEADS = 2
SWA_HEAD_DIM = 64
SWA_WINDOW = 128

BAND = 128
DIL_GROUP_HEADS_PER_STEP = (4, 4, 2)
DIL_GROUP_CHUNKS = (4, 1, 1)

LANES = 128
MLA_QK_PAD = 256
EVEN_Z = 4224
ODD_Z = 5888
VMEM_LIMIT = 56 * 1024 * 1024
LOG2E = 1.4426950408889634


def _params(sem):
    return pltpu.CompilerParams(dimension_semantics=sem, vmem_limit_bytes=VMEM_LIMIT)


def _rms(x, g):
    ms = jnp.mean(x * x, axis=-1, keepdims=True)
    return x * lax.rsqrt(ms + NORM_EPS) * g


def _qk(q, k):
    return lax.dot_general(q, k, (((1,), (1,)), ((), ())), preferred_element_type=F32)


def _norm_matmul_kernel(x_ref, g_ref, w_ref, o_ref):
    h = _rms(x_ref[...], g_ref[...]).astype(BF16)
    o_ref[...] = jnp.dot(h, w_ref[...], preferred_element_type=F32).astype(o_ref.dtype)


def norm_matmul(x, g, w, *, tm):
    m, k = x.shape
    n = w.shape[1]
    return pl.pallas_call(
        _norm_matmul_kernel,
        grid=(m // tm,),
        in_specs=[
            pl.BlockSpec((tm, k), lambda i: (i, 0)),
            pl.BlockSpec((1, k), lambda i: (0, 0)),
            pl.BlockSpec((k, n), lambda i: (0, 0), pipeline_mode=pl.Buffered(1)),
        ],
        out_specs=pl.BlockSpec((tm, n), lambda i: (i, 0)),
        out_shape=jax.ShapeDtypeStruct((m, n), BF16),
        compiler_params=_params(("parallel",)),
        name="norm_matmul",
    )(x, g, w)


def _mlp_kernel(x_ref, g_ref, wu_ref, wd_ref, fg_ref, o_ref, h_ref, acc_ref, *, final_norm):
    f = pl.program_id(1)

    @pl.when(f == 0)
    def _():
        h_ref[...] = _rms(x_ref[...], g_ref[...]).astype(BF16)
        acc_ref[...] = jnp.zeros_like(acc_ref)

    u = jnp.dot(h_ref[...], wu_ref[...], preferred_element_type=F32)
    u = jnp.maximum(u, 0.0)
    u = u * u
    acc_ref[...] += jnp.dot(u.astype(BF16), wd_ref[...], preferred_element_type=F32)

    @pl.when(f == pl.num_programs(1) - 1)
    def _():
        y = x_ref[...] + acc_ref[...]
        if final_norm:
            y = _rms(y, fg_ref[...])
        o_ref[...] = y


def mlp_residual(x, g, w_up, w_down, final_g, *, tm, tf, final_norm):
    m, d = x.shape
    ff = w_up.shape[1]
    return pl.pallas_call(
        functools.partial(_mlp_kernel, final_norm=final_norm),
        grid=(m // tm, ff // tf),
        in_specs=[
            pl.BlockSpec((tm, d), lambda i, f: (i, 0)),
            pl.BlockSpec((1, d), lambda i, f: (0, 0)),
            pl.BlockSpec((d, tf), lambda i, f: (0, f)),
            pl.BlockSpec((tf, d), lambda i, f: (f, 0)),
            pl.BlockSpec((1, d), lambda i, f: (0, 0)),
        ],
        out_specs=pl.BlockSpec((tm, d), lambda i, f: (i, 0)),
        out_shape=jax.ShapeDtypeStruct((m, d), F32),
        scratch_shapes=[pltpu.VMEM((tm, d), BF16), pltpu.VMEM((tm, d), F32)],
        compiler_params=_params(("parallel", "arbitrary")),
        name="mlp_residual",
    )(x, g, w_up, w_down, final_g)


def _proj_residual_kernel(x_ref, a_ref, b_ref, wa_ref, wb_ref, o_ref):
    y = jnp.dot(a_ref[...], wa_ref[...], preferred_element_type=F32)
    y = y + jnp.dot(b_ref[...], wb_ref[...], preferred_element_type=F32)
    o_ref[...] = x_ref[...] + y


def proj_residual(x, a, b, wa, wb, *, tm):
    m, d = x.shape
    ka, kb = a.shape[1], b.shape[1]
    return pl.pallas_call(
        _proj_residual_kernel,
        grid=(m // tm,),
        in_specs=[
            pl.BlockSpec((tm, d), lambda i: (i, 0)),
            pl.BlockSpec((tm, ka), lambda i: (i, 0)),
            pl.BlockSpec((tm, kb), lambda i: (i, 0)),
            pl.BlockSpec((ka, d), lambda i: (0, 0)),
            pl.BlockSpec((kb, d), lambda i: (0, 0)),
        ],
        out_specs=pl.BlockSpec((tm, d), lambda i: (i, 0)),
        out_shape=jax.ShapeDtypeStruct((m, d), F32),
        compiler_params=_params(("parallel",)),
        name="proj_residual",
    )(x, a, b, wa, wb)


def _mla_up_kernel(ql_ref, kvl_ref, kpe_ref, qn_ref, kvn_ref, wq_ref, wk_ref, wv_ref, cos_ref, sin_ref,
                   q_ref, k_ref, v_ref):
    cos = cos_ref[...]
    sin = sin_ref[...]

    def rope(t):
        return t * cos + pltpu.roll(t, LANES // 2, 1) * sin

    hq = _rms(ql_ref[...].astype(F32), qn_ref[...]).astype(BF16)
    yq = jnp.dot(hq, wq_ref[...], preferred_element_type=F32)
    hkv = _rms(kvl_ref[...].astype(F32), kvn_ref[...]).astype(BF16)
    yk = jnp.dot(hkv, wk_ref[...], preferred_element_type=F32)
    yv = jnp.dot(hkv, wv_ref[...], preferred_element_type=F32)
    krot = rope(kpe_ref[...].astype(F32)).astype(BF16)
    for h in range(MLA_HEADS):
        lo = h * MLA_QK_PAD
        mid = lo + MLA_NOPE_DIM
        hi = lo + MLA_QK_PAD
        q_ref[:, lo:mid] = yq[:, lo:mid].astype(BF16)
        q_ref[:, mid:hi] = rope(yq[:, mid:hi]).astype(BF16)
        k_ref[:, lo:mid] = yk[:, h * MLA_NOPE_DIM:(h + 1) * MLA_NOPE_DIM].astype(BF16)
        k_ref[:, mid:hi] = krot
    v_ref[...] = yv.astype(BF16)


def mla_up(z, qn, kvn, wq, wk, wv, cos, sin, *, seq, tm):
    t = z.shape[0]
    r = MLA_Q_RANK
    nq = MLA_HEADS * MLA_QK_PAD
    nv = MLA_HEADS * MLA_V_DIM
    kpe_blk = (EVEN_Z - LANES) // LANES
    pos_blocks = seq // tm
    return pl.pallas_call(
        _mla_up_kernel,
        grid=(t // tm,),
        in_specs=[
            pl.BlockSpec((tm, r), lambda i: (i, 0)),
            pl.BlockSpec((tm, r), lambda i: (i, 1)),
            pl.BlockSpec((tm, LANES), lambda i: (i, kpe_blk)),
            pl.BlockSpec((1, r), lambda i: (0, 0)),
            pl.BlockSpec((1, r), lambda i: (0, 0)),
            pl.BlockSpec((r, nq), lambda i: (0, 0)),
            pl.BlockSpec((r, nv), lambda i: (0, 0)),
            pl.BlockSpec((r, nv), lambda i: (0, 0)),
            pl.BlockSpec((tm, LANES), lambda i: (i % pos_blocks, 0)),
            pl.BlockSpec((tm, LANES), lambda i: (i % pos_blocks, 0)),
        ],
        out_specs=[
            pl.BlockSpec((tm, nq), lambda i: (i, 0)),
            pl.BlockSpec((tm, nq), lambda i: (i, 0)),
            pl.BlockSpec((tm, nv), lambda i: (i, 0)),
        ],
        out_shape=[
            jax.ShapeDtypeStruct((t, nq), BF16),
            jax.ShapeDtypeStruct((t, nq), BF16),
            jax.ShapeDtypeStruct((t, nv), BF16),
        ],
        compiler_params=_params(("parallel",)),
        name="mla_up",
    )(z, z, z, qn, kvn, wq, wk, wv, cos, sin)


def _softmax_steps_t(sts, vts, carries):
    stats = []
    for st, (m, l, _) in zip(sts, carries):
        m_new = jnp.maximum(m, jnp.max(st, axis=0, keepdims=True))
        p = jnp.exp2(st - m_new)
        alpha = jnp.exp2(m - m_new)
        stats.append((m_new, alpha * l + jnp.sum(p, axis=0, keepdims=True), alpha, p.astype(BF16)))
    return tuple((m_new, l_new, alpha * acc + jnp.dot(vt, p, preferred_element_type=F32))
                 for (m_new, l_new, alpha, p), vt, (_, _, acc) in zip(stats, vts, carries))


def _softmax_init(dv, nq):
    return (jnp.full((1, nq), -jnp.inf, F32), jnp.zeros((1, nq), F32), jnp.zeros((dv, nq), F32))


def _transpose_bf16(x):
    return x.astype(F32).T.astype(BF16)


def _mla_attn_kernel(q_ref, k_ref, v_ref, o_ref, vt_ref, *, blk, heads):
    qi = pl.program_id(2)
    dq, dv = MLA_QK_PAD, MLA_V_DIM

    @pl.when(qi == 0)
    def _():
        def fill(n, c):
            start = pl.multiple_of(n * blk, blk)
            for g in range(heads):
                vt_ref[g * dv:(g + 1) * dv, pl.ds(start, blk)] = _transpose_bf16(
                    v_ref[pl.ds(start, blk), g * dv:(g + 1) * dv])
            return c

        lax.fori_loop(0, k_ref.shape[0] // blk, fill, 0)

    qts = [_transpose_bf16(q_ref[:, g * dq:(g + 1) * dq]) for g in range(heads)]
    tk = 2 * blk

    def tile(start, nk, carries, mask):
        sts = [jnp.dot(k_ref[pl.ds(start, nk), g * dq:(g + 1) * dq], qts[g], preferred_element_type=F32)
               for g in range(heads)]
        if mask is not None:
            sts = [jnp.where(mask, st, -jnp.inf) for st in sts]
        vts = [vt_ref[g * dv:(g + 1) * dv, pl.ds(start, nk)] for g in range(heads)]
        return _softmax_steps_t(sts, vts, carries)

    def finish(carries):
        for g in range(heads):
            _, l, acc = carries[g]
            o_ref[:, g * dv:(g + 1) * dv] = (acc / l).T.astype(o_ref.dtype)

    def causal(nk):
        key = lax.broadcasted_iota(jnp.int32, (nk, blk), 0)
        qry = lax.broadcasted_iota(jnp.int32, (nk, blk), 1)
        return key <= qry + (nk - blk)

    init = tuple(_softmax_init(dv, blk) for _ in range(heads))
    carries = lax.fori_loop(0, qi // 2, lambda n, c: tile(pl.multiple_of(n * tk, tk), tk, c, None), init)

    @pl.when(qi % 2 == 0)
    def _():
        finish(tile(pl.multiple_of(qi * blk, blk), blk, carries, causal(blk)))

    @pl.when(qi % 2 == 1)
    def _():
        finish(tile(pl.multiple_of((qi - 1) * blk, blk), tk, carries, causal(tk)))


def mla_attention(q, k, v, *, batch, seq, blk, heads):
    dq = MLA_QK_PAD * heads
    dv = MLA_V_DIM * heads
    return pl.pallas_call(
        functools.partial(_mla_attn_kernel, blk=blk, heads=heads),
        grid=(batch, MLA_HEADS // heads, seq // blk),
        in_specs=[
            pl.BlockSpec((None, blk, dq), lambda b, h, i: (b, i, h)),
            pl.BlockSpec((None, seq, dq), lambda b, h, i: (b, 0, h), pipeline_mode=pl.Buffered(1)),
            pl.BlockSpec((None, seq, dv), lambda b, h, i: (b, 0, h), pipeline_mode=pl.Buffered(1)),
        ],
        out_specs=pl.BlockSpec((None, blk, dv), lambda b, h, i: (b, i, h)),
        out_shape=jax.ShapeDtypeStruct((batch, seq, MLA_HEADS * MLA_V_DIM), BF16),
        scratch_shapes=[pltpu.VMEM((dv, seq), BF16)],
        compiler_params=_params(("parallel", "parallel", "arbitrary")),
        name="mla_attention",
    )(q, k, v)


MOBA_NB_PAD = 16
MOBA_MASKED = -1e30


def _moba_kernel(slopes_ref, q_ref, k_ref, v_ref, o_ref, vt_ref, kmean_ref, sel_ref, *, nb, heads):
    blk = MOBA_BLOCK
    dh = MOBA_HEAD_DIM
    nbp = kmean_ref.shape[1]
    hg = pl.program_id(1)
    i = pl.program_id(2)

    @pl.when(i == 0)
    def _():
        kmean_ref[...] = jnp.zeros_like(kmean_ref)

        def fill(n, c):
            start = pl.multiple_of(n * blk, blk)
            for g in range(heads):
                kn = k_ref[pl.ds(start, blk), g * dh:(g + 1) * dh].astype(F32)
                kmean_ref[g, pl.ds(n, 1), :] = jnp.mean(kn, axis=0, keepdims=True)
                vt_ref[g * dh:(g + 1) * dh, pl.ds(start, blk)] = _transpose_bf16(
                    v_ref[pl.ds(start, blk), g * dh:(g + 1) * dh])
            return c

        lax.fori_loop(0, nb, fill, 0)

    blk_id = lax.broadcasted_iota(jnp.int32, (nbp, blk), 0)
    past = blk_id < i
    tk = 2 * blk
    key = lax.broadcasted_iota(jnp.int32, (tk, blk), 0)
    qry = lax.broadcasted_iota(jnp.int32, (tk, blk), 1)
    keyf = key.astype(F32)
    key1 = lax.broadcasted_iota(jnp.int32, (blk, blk), 0)
    qry1 = lax.broadcasted_iota(jnp.int32, (blk, blk), 1)
    keyf1 = key1.astype(F32)

    qts, colbias, slopes = [], [], []
    for g in range(heads):
        qt = _transpose_bf16(q_ref[:, g * dh:(g + 1) * dh])
        km = kmean_ref[g]
        km_hi = km.astype(BF16)
        km_lo = (km - km_hi.astype(F32)).astype(BF16)
        gate = (jnp.dot(km_hi, qt, preferred_element_type=F32)
                + jnp.dot(km_lo, qt, preferred_element_type=F32))
        gm = jnp.where(past, gate, -jnp.inf)
        rank = jnp.zeros((nbp, blk), jnp.int32)
        for n in range(nb):
            gn = gm[n:n + 1, :]
            beats = (gn > gm) | ((gn == gm) & (blk_id > n))
            rank = rank + beats.astype(jnp.int32)
        sel_ref[g] = jnp.where(past & (rank < min(MOBA_TOPK, nb)), 0.0, MOBA_MASKED)
        slope = slopes_ref[hg * heads + g]
        qts.append(qt)
        slopes.append(slope)
        colbias.append(keyf * slope)

    def tile(b0, carries, gates, causal):
        nblk = len(gates[0])
        nk = nblk * blk
        start = pl.multiple_of(b0 * blk, blk)
        sts = [jnp.dot(k_ref[pl.ds(start, nk), g * dh:(g + 1) * dh], qts[g], preferred_element_type=F32)
               for g in range(heads)]
        biased = []
        for g in range(heads):
            shift = jnp.full((1, blk), (b0 - i) * blk, jnp.int32).astype(F32) * slopes[g]
            if nblk == 1:
                st = sts[g] + (keyf1 * slopes[g] + shift)
            else:
                rows = [shift if gate is None else gate + shift for gate in gates[g]]
                st = sts[g] + colbias[g]
                st = jnp.concatenate([st[c * blk:(c + 1) * blk] + rows[c] for c in range(nblk)], axis=0)
            if causal:
                st = jnp.where(key1 <= qry1 if nblk == 1 else key <= qry + blk, st, -jnp.inf)
            biased.append(st)
        vts = [vt_ref[g * dh:(g + 1) * dh, pl.ds(start, nk)] for g in range(heads)]
        return _softmax_steps_t(biased, vts, carries)

    def body(n, carries):
        gates = [(sel_ref[g, pl.ds(2 * n, 1), :], sel_ref[g, pl.ds(2 * n + 1, 1), :]) for g in range(heads)]
        return tile(2 * n, carries, gates, False)

    def finish(carries):
        for g in range(heads):
            _, l, acc = carries[g]
            o_ref[:, g * dh:(g + 1) * dh] = (acc / l).T.astype(o_ref.dtype)

    init = tuple(_softmax_init(dh, blk) for _ in range(heads))
    carries = lax.fori_loop(0, i // 2, body, init)

    @pl.when(i % 2 == 0)
    def _():
        finish(tile(i, carries, [(None,)] * heads, True))

    @pl.when(i % 2 == 1)
    def _():
        finish(tile(i - 1, carries, [(sel_ref[g, pl.ds(i - 1, 1), :], None) for g in range(heads)], True))


def moba_attention(z, slopes, *, batch, seq, heads):
    blk = MOBA_BLOCK
    w = MOBA_HEAD_DIM * heads
    nb = seq // blk
    qo, ko, vo = 1024 // w, 2048 // w, 3072 // w
    return pl.pallas_call(
        functools.partial(_moba_kernel, nb=nb, heads=heads),
        grid=(batch, MOBA_HEADS // heads, nb),
        in_specs=[
            pl.BlockSpec(memory_space=pltpu.SMEM),
            pl.BlockSpec((None, blk, w), lambda b, h, i: (b, i, qo + h)),
            pl.BlockSpec((None, seq, w), lambda b, h, i: (b, 0, ko + h)),
            pl.BlockSpec((None, seq, w), lambda b, h, i: (b, 0, vo + h)),
        ],
        out_specs=pl.BlockSpec((None, blk, w), lambda b, h, i: (b, i, h)),
        out_shape=jax.ShapeDtypeStruct((batch, seq, MOBA_HEADS * MOBA_HEAD_DIM), BF16),
        scratch_shapes=[
            pltpu.VMEM((w, seq), BF16),
            pltpu.VMEM((heads, max(MOBA_NB_PAD, nb), MOBA_HEAD_DIM), F32),
            pltpu.VMEM((heads, max(MOBA_NB_PAD, nb), blk), F32),
        ],
        compiler_params=_params(("parallel", "parallel", "arbitrary")),
        name="moba_attention",
    )(slopes, z, z, z)


def _band_mask(j, max_dist):
    row = lax.broadcasted_iota(jnp.int32, (BAND, 2 * BAND), 0)
    col = lax.broadcasted_iota(jnp.int32, (BAND, 2 * BAND), 1)
    dist = BAND + row - col
    mask = (dist >= 0) & (dist <= max_dist) & ((col >= BAND) | (j > 0))
    return mask, dist.astype(F32)


def _dil_kernel(slopes_ref, q_ref, k_ref, v_ref, o_ref, lse_ref, stage_ref, qd_ref, kd_ref, vd_ref, *, group):
    hb = pl.program_id(1)
    j = pl.program_id(2)
    rate = DIL_RATES[group]
    dh = DIL_HEAD_DIM
    nh = q_ref.shape[-1] // dh
    sup = BAND * rate
    chunks = q_ref.shape[0] // sup

    @pl.when(j == 0)
    def _():
        kd_ref[0] = jnp.zeros(kd_ref.shape[1:], BF16)
        vd_ref[0] = jnp.zeros(vd_ref.shape[1:], BF16)

    def split(src_ref, store):
        if rate == 1:
            for c in range(chunks):
                store(c, 0, src_ref[c * BAND:(c + 1) * BAND, :])
            return
        for hh in range(nh):
            stage_ref[0, hh] = src_ref[:, hh * dh:(hh + 1) * dh].astype(F32)
        for c in range(chunks):
            for r in range(rate):
                rows = pl.ds(c * sup + r, BAND, stride=rate)
                store(c, r, jnp.concatenate([stage_ref[0, hh, rows, :].astype(BF16) for hh in range(nh)], axis=1))

    def store_q(c, r, x):
        qd_ref[c, r] = x

    def store_k(c, r, x):
        kd_ref[1 + c, r] = x

    def store_v(c, r, x):
        vd_ref[1 + c, r] = x

    split(q_ref, store_q)
    split(k_ref, store_k)
    split(v_ref, store_v)

    def biases(first_flag):
        mask, distf = _band_mask(first_flag, DIL_WINDOWS[group] // rate)
        out = []
        for hh in range(nh):
            slope = slopes_ref[group * DIL_HEADS_PER_GROUP + hb * nh + hh] * rate
            out.append(jnp.where(mask, -slope * distf, -jnp.inf))
        return out

    sls = [slice(hh * dh, (hh + 1) * dh) for hh in range(nh)]
    for c in range(chunks):
        bias = biases(j if c == 0 else 1)

        def sub(r, carry, c=c, bias=bias):
            q = qd_ref[c, r]
            kw = jnp.concatenate([kd_ref[c, r], kd_ref[c + 1, r]], axis=0)
            vw = jnp.concatenate([vd_ref[c, r], vd_ref[c + 1, r]], axis=0)
            ss = [_qk(q[:, sl], kw[:, sl]) + bias[hh] for hh, sl in enumerate(sls)]
            stats = []
            for s in ss:
                m = jnp.max(s, axis=-1, keepdims=True)
                p = jnp.exp2(s - m)
                stats.append((m, jnp.sum(p, axis=-1, keepdims=True), p.astype(BF16)))
            for hh, ((m, l, p), sl) in enumerate(zip(stats, sls)):
                out = jnp.dot(p, vw[:, sl], preferred_element_type=F32) / l
                lse = jnp.broadcast_to(m + jnp.log2(l), (BAND, dh))
                if rate == 1:
                    o_ref[c * BAND:(c + 1) * BAND, sl] = out
                    lse_ref[c * BAND:(c + 1) * BAND, sl] = lse
                else:
                    rows = pl.ds(c * sup + r, BAND, stride=rate)
                    stage_ref[0, hh, rows, :] = out
                    stage_ref[1, hh, rows, :] = lse
            return carry

        if rate == 1:
            sub(0, 0)
        else:
            lax.fori_loop(0, rate, sub, 0)

    if rate > 1:
        for hh in range(nh):
            o_ref[:, sls[hh]] = stage_ref[0, hh]
            lse_ref[:, sls[hh]] = stage_ref[1, hh]
    kd_ref[0] = kd_ref[chunks]
    vd_ref[0] = vd_ref[chunks]


def dilated_group(z, slopes, *, group, batch, seq, heads, chunks):
    rate = DIL_RATES[group]
    tile = BAND * rate * chunks
    gw = DIL_HEADS_PER_GROUP * DIL_HEAD_DIM
    w = heads * DIL_HEAD_DIM
    nhb = DIL_HEADS_PER_GROUP // heads
    qo, ko, vo = (group * gw // w, (N_DIL + group) * gw // w, (2 * N_DIL + group) * gw // w)
    stage_rows = tile if rate > 1 else 8

    def spec(off):
        return pl.BlockSpec((None, tile, w), lambda b, h, j: (b, j, off + h))

    out_spec = pl.BlockSpec((None, tile, w), lambda b, h, j: (b, j, h))
    o, lse = pl.pallas_call(
        functools.partial(_dil_kernel, group=group),
        grid=(batch, nhb, seq // tile),
        in_specs=[pl.BlockSpec(memory_space=pltpu.SMEM), spec(qo), spec(ko), spec(vo)],
        out_specs=[out_spec, out_spec],
        out_shape=[jax.ShapeDtypeStruct((batch, seq, gw), F32)] * 2,
        scratch_shapes=[
            pltpu.VMEM((2, heads, stage_rows, DIL_HEAD_DIM), F32),
            pltpu.VMEM((chunks, rate, BAND, w), BF16),
            pltpu.VMEM((chunks + 1, rate, BAND, w), BF16),
            pltpu.VMEM((chunks + 1, rate, BAND, w), BF16),
        ],
        compiler_params=_params(("parallel", "parallel", "arbitrary")),
        name=f"dilated_group{group}",
    )(slopes, z, z, z)
    return o.reshape(batch * seq, gw), lse.reshape(batch * seq, gw)


def _odd_proj_kernel(x_ref, o0_ref, l0_ref, o1_ref, l1_ref, o2_ref, l2_ref, d_ref, wc_ref, wd_ref, out_ref):
    l0, l1, l2 = l0_ref[...], l1_ref[...], l2_ref[...]
    mx = jnp.maximum(jnp.maximum(l0, l1), l2)
    w0, w1, w2 = jnp.exp2(l0 - mx), jnp.exp2(l1 - mx), jnp.exp2(l2 - mx)
    num = o0_ref[...] * w0 + o1_ref[...] * w1 + o2_ref[...] * w2
    c = (num / (w0 + w1 + w2)).astype(BF16)
    y = jnp.dot(c, wc_ref[...], preferred_element_type=F32)
    y = y + jnp.dot(d_ref[...], wd_ref[...], preferred_element_type=F32)
    out_ref[...] = x_ref[...] + y


def odd_proj_residual(x, parts, dd, wc, wd, *, tm):
    m, d = x.shape
    kc, kd = wc.shape[0], wd.shape[0]
    part_spec = pl.BlockSpec((tm, kc), lambda i: (i, 0))
    return pl.pallas_call(
        _odd_proj_kernel,
        grid=(m // tm,),
        in_specs=[pl.BlockSpec((tm, d), lambda i: (i, 0))] + [part_spec] * 6 + [
            pl.BlockSpec((tm, kd), lambda i: (i, 0)),
            pl.BlockSpec((kc, d), lambda i: (0, 0), pipeline_mode=pl.Buffered(1)),
            pl.BlockSpec((kd, d), lambda i: (0, 0), pipeline_mode=pl.Buffered(1)),
        ],
        out_specs=pl.BlockSpec((tm, d), lambda i: (i, 0)),
        out_shape=jax.ShapeDtypeStruct((m, d), F32),
        compiler_params=_params(("parallel",)),
        name="odd_proj_residual",
    )(x, *parts, dd, wc, wd)


def _swa_kernel(sinks_ref, qa_ref, qb_ref, kp_ref, ko_ref, vp_ref, vo_ref, o_ref, bias_ref):
    j = pl.program_id(1)
    heads_per_kv = SWA_Q_HEADS // SWA_KV_HEADS
    half = LANES // 2

    @pl.when(j == 0)
    def _():
        for later in (0, 1):
            mask, distf = _band_mask(later, SWA_WINDOW - 1)
            for head in range(SWA_Q_HEADS):
                slope = 2.0 ** (-8.0 * (head + 1) / SWA_Q_HEADS) * LOG2E
                bias_ref[later, head] = jnp.where(mask, -slope * distf, -jnp.inf)

    later = jnp.minimum(j, 1)

    kw = jnp.concatenate([kp_ref[...], ko_ref[...]], axis=0).astype(F32)
    vw = jnp.concatenate([vp_ref[...], vo_ref[...]], axis=0).astype(F32)
    low = lax.broadcasted_iota(jnp.int32, (2 * BAND, LANES), 1) < half

    def halves(x):
        xr = pltpu.roll(x, half, 1)
        even = (jnp.where(low, x, 0.0).astype(BF16), jnp.where(low, xr, 0.0).astype(BF16))
        odd = (jnp.where(low, 0.0, xr).astype(BF16), jnp.where(low, 0.0, x).astype(BF16))
        return even, odd

    k_even, k_odd = halves(kw)
    v_even, v_odd = halves(vw)

    for kvh in range(SWA_KV_HEADS):
        heads = list(range(kvh * heads_per_kv, (kvh + 1) * heads_per_kv))
        ss = []
        q_ref = (qa_ref, qb_ref)[kvh]
        for head in heads:
            pair = (head - heads[0]) // 2
            slab = q_ref[:, pair * LANES:(pair + 1) * LANES]
            kk = (k_even if head % 2 == 0 else k_odd)[kvh]
            ss.append(_qk(slab, kk) + bias_ref[later, head])
        stats = []
        for s in ss:
            m = jnp.max(s, axis=-1, keepdims=True)
            p = jnp.exp2(s - m)
            stats.append((m, jnp.sum(p, axis=-1, keepdims=True), p.astype(BF16)))
        outs = {}
        for head, (m, l, p) in zip(heads, stats):
            vv = (v_even if head % 2 == 0 else v_odd)[kvh]
            acc = jnp.dot(p, vv, preferred_element_type=F32)
            sk = sinks_ref[head] * LOG2E
            mx = jnp.maximum(m, sk)
            a = jnp.exp2(m - mx)
            outs[head] = acc * a / (l * a + jnp.exp2(sk - mx))
        for pair in range(heads[0] // 2, heads[-1] // 2 + 1):
            o_ref[:, pair * LANES:(pair + 1) * LANES] = (outs[2 * pair] + outs[2 * pair + 1]).astype(o_ref.dtype)


def swa_attention(z, sinks, *, batch, seq):
    wq = SWA_Q_HEADS * SWA_HEAD_DIM
    wq_kv = wq // SWA_KV_HEADS
    q0 = 3 * N_DIL * DIL_HEADS_PER_GROUP * DIL_HEAD_DIM
    qo, ko, vo = q0 // wq_kv, (q0 + wq) // LANES, (q0 + wq) // LANES + 1

    def kvspec(off, prev):
        if prev:
            return pl.BlockSpec((None, BAND, LANES), lambda b, j: (b, jnp.maximum(j - 1, 0), off))
        return pl.BlockSpec((None, BAND, LANES), lambda b, j: (b, j, off))

    return pl.pallas_call(
        _swa_kernel,
        grid=(batch, seq // BAND),
        in_specs=[
            pl.BlockSpec(memory_space=pltpu.SMEM),
            pl.BlockSpec((None, BAND, wq_kv), lambda b, j: (b, j, qo)),
            pl.BlockSpec((None, BAND, wq_kv), lambda b, j: (b, j, qo + 1)),
            kvspec(ko, True), kvspec(ko, False), kvspec(vo, True), kvspec(vo, False),
        ],
        out_specs=pl.BlockSpec((None, BAND, wq), lambda b, j: (b, j, 0)),
        out_shape=jax.ShapeDtypeStruct((batch, seq, wq), BF16),
        scratch_shapes=[pltpu.VMEM((2, SWA_Q_HEADS, BAND, 2 * BAND), F32)],
        compiler_params=_params(("parallel", "arbitrary")),
        name="swa_attention",
    )(sinks, z, z, z, z, z, z)


def _rope_pad(w):
    half = MLA_ROPE_DIM // 2
    z = jnp.zeros(w.shape[:-1] + (half,), w.dtype)
    return jnp.concatenate([w[..., :half], z, w[..., half:], z], axis=-1)


def _even_weights(w_in, w_uq, w_ukv, w_out):
    hd = MOBA_HEADS * MOBA_HEAD_DIM
    o = 0
    q_lat = w_in[:, o:o + MLA_Q_RANK]; o += MLA_Q_RANK
    kv_lat = w_in[:, o:o + MLA_KV_RANK]; o += MLA_KV_RANK
    k_pe = w_in[:, o:o + MLA_ROPE_DIM]; o += MLA_ROPE_DIM
    qb = w_in[:, o:o + hd] * (MOBA_HEAD_DIM ** -0.5 * LOG2E); o += hd
    kb = w_in[:, o:o + hd]; o += hd
    vb = w_in[:, o:o + hd]
    w_in_p = jnp.concatenate([q_lat, kv_lat, qb, kb, vb, _rope_pad(k_pe)], axis=-1).astype(BF16)

    r = w_uq.shape[0]
    uq = w_uq.reshape(r, MLA_HEADS, MLA_NOPE_DIM + MLA_ROPE_DIM) * ((MLA_NOPE_DIM + MLA_ROPE_DIM) ** -0.5 * LOG2E)
    wq = jnp.concatenate([uq[..., :MLA_NOPE_DIM], _rope_pad(uq[..., MLA_NOPE_DIM:])], axis=-1)
    wq = wq.reshape(r, MLA_HEADS * MLA_QK_PAD).astype(BF16)
    ukv = w_ukv.reshape(w_ukv.shape[0], MLA_HEADS, MLA_NOPE_DIM + MLA_V_DIM)
    wk = ukv[..., :MLA_NOPE_DIM].reshape(-1, MLA_HEADS * MLA_NOPE_DIM).astype(BF16)
    wv = ukv[..., MLA_NOPE_DIM:].reshape(-1, MLA_HEADS * MLA_V_DIM).astype(BF16)
    na = MLA_HEADS * MLA_V_DIM
    return w_in_p, wq, wk, wv, w_out[:na].astype(BF16), w_out[na:].astype(BF16)


def _odd_weights(w_in, w_out):
    cd = N_DIL * DIL_HEADS_PER_GROUP * DIL_HEAD_DIM
    qd_w = SWA_Q_HEADS * SWA_HEAD_DIM
    kd_w = SWA_KV_HEADS * SWA_HEAD_DIM
    scale = jnp.concatenate([
        jnp.full((cd,), DIL_HEAD_DIM ** -0.5 * LOG2E, F32), jnp.ones((2 * cd,), F32),
        jnp.full((qd_w,), SWA_HEAD_DIM ** -0.5 * LOG2E, F32), jnp.ones((2 * kd_w,), F32)])
    w_in_p = (w_in * scale[None, :]).astype(BF16)
    nc = DIL_HEADS_PER_GROUP * DIL_HEAD_DIM
    return w_in_p, w_out[:nc].astype(BF16), w_out[nc:].astype(BF16)


def _rope_tables(seq):
    half = MLA_ROPE_DIM // 2
    inv_freq = ROPE_THETA ** (-jnp.arange(half, dtype=F32) / half)
    ang = jnp.arange(seq, dtype=jnp.int32).astype(F32)[:, None] * inv_freq[None, :]
    c, s = jnp.cos(ang), jnp.sin(ang)
    z = jnp.zeros_like(c)
    return jnp.concatenate([c, z, c, z], axis=-1), jnp.concatenate([-s, z, s, z], axis=-1)


def kernel(x, attn_norm, mlp_norm, w_up, w_down, ev_w_in, ev_q_norm, ev_w_uq, ev_kv_norm, ev_w_ukv, ev_w_out,
           od_w_in, od_sinks, od_w_out, final_norm):
    batch, seq, d = x.shape
    depth = attn_norm.shape[0]
    t = batch * seq
    xt = x.reshape(t, d)
    cos, sin = _rope_tables(seq)
    moba_slopes = jnp.exp2(-8.0 * jnp.arange(1, MOBA_HEADS + 1, dtype=F32) / MOBA_HEADS) * LOG2E
    n_dil_heads = N_DIL * DIL_HEADS_PER_GROUP
    dil_slopes = jnp.exp2(-8.0 * jnp.arange(1, n_dil_heads + 1, dtype=F32) / n_dil_heads) * LOG2E
    fg = final_norm.reshape(1, d)

    for layer in range(depth):
        i = layer // 2
        g = attn_norm[layer].reshape(1, d)
        if layer % 2 == 0:
            w_in_p, wq, wk, wv, wo_a, wo_b = _even_weights(ev_w_in[i], ev_w_uq[i], ev_w_ukv[i], ev_w_out[i])
            z = norm_matmul(xt, g, w_in_p, tm=256)
            q, k, v = mla_up(z, ev_q_norm[i].reshape(1, -1), ev_kv_norm[i].reshape(1, -1), wq, wk, wv, cos, sin,
                             seq=seq, tm=256)
            a = mla_attention(q.reshape(batch, seq, -1), k.reshape(batch, seq, -1), v.reshape(batch, seq, -1),
                              batch=batch, seq=seq, blk=256, heads=8)
            b = moba_attention(z.reshape(batch, seq, EVEN_Z), moba_slopes, batch=batch, seq=seq, heads=8)
            xt = proj_residual(xt, a.reshape(t, -1), b.reshape(t, -1), wo_a, wo_b, tm=512)
        else:
            w_in_p, wo_c, wo_d = _odd_weights(od_w_in[i], od_w_out[i])
            z = norm_matmul(xt, g, w_in_p, tm=256)
            z3 = z.reshape(batch, seq, ODD_Z)
            parts = []
            for grp in range(N_DIL):
                parts.extend(dilated_group(z3, dil_slopes, group=grp, batch=batch, seq=seq,
                                           heads=DIL_GROUP_HEADS_PER_STEP[grp], chunks=DIL_GROUP_CHUNKS[grp]))
            dd = swa_attention(z3, od_sinks[i], batch=batch, seq=seq)
            xt = odd_proj_residual(xt, parts, dd.reshape(t, -1), wo_c, wo_d, tm=512)
        xt = mlp_residual(xt, mlp_norm[layer].reshape(1, d), w_up[layer].astype(BF16), w_down[layer].astype(BF16), fg,
                          tm=512, tf=1024, final_norm=(layer == depth - 1))
    return xt.reshape(batch, seq, d)
```

```python
import functools

import jax
import jax.numpy as jnp
from jax import lax
from jax.experimental import pallas as pl
from jax.experimental.pallas import tpu as pltpu

F32 = jnp.float32
BF16 = jnp.bfloat16

NORM_EPS = 1e-6
D_FF_MULT = 4

MLA_HEADS = 8
MLA_Q_RANK = 512
MLA_KV_RANK = 512
MLA_NOPE_DIM = 128
MLA_ROPE_DIM = 64
MLA_V_DIM = 128
ROPE_THETA = 10000.0

MOBA_HEADS = 8
MOBA_HEAD_DIM = 128
MOBA_BLOCK = 256
MOBA_TOPK = 3

DIL_WINDOWS = (128, 512, 2048)
DIL_RATES = (1, 4, 16)
N_DIL = 3
DIL_HEADS_PER_GROUP = 4
DIL_HEAD_DIM = 128

SWA_Q_HEADS = 16
SWA_KV_HEADS = 2
SWA_HEAD_DIM = 64
SWA_WINDOW = 128

BAND = 128
PROJ_ROWS = 256
MLA_UP_ROWS = 512
ATTN_BLOCK = 256
ATTN_HEADS_PER_STEP = 8
OUT_PROJ_ROWS = 512
MLP_ROWS = 512
MLP_FF_COLS = 1024
SWA_CHUNKS = 4
DIL_GROUP_HEADS_PER_STEP = (4, 4, 2)
DIL_GROUP_CHUNKS = (4, 2, 1)

LANES = 128
MLA_QK_PAD = 256
EVEN_Z = 4224
ODD_Z = 5888
VMEM_LIMIT = 56 * 1024 * 1024
LOG2E = 1.4426950408889634


def _params(sem):
    return pltpu.CompilerParams(dimension_semantics=sem, vmem_limit_bytes=VMEM_LIMIT)


def _rms(x, g):
    ms = jnp.mean(x * x, axis=-1, keepdims=True)
    return x * lax.rsqrt(ms + NORM_EPS) * g


def _qk(q, k):
    return lax.dot_general(q, k, (((1,), (1,)), ((), ())), preferred_element_type=F32)


def _norm_matmul_kernel(x_ref, g_ref, w_ref, o_ref):
    h = _rms(x_ref[...], g_ref[...]).astype(BF16)
    o_ref[...] = jnp.dot(h, w_ref[...], preferred_element_type=F32).astype(o_ref.dtype)


def norm_matmul(x, g, w, *, tm):
    m, k = x.shape
    n = w.shape[1]
    return pl.pallas_call(
        _norm_matmul_kernel,
        grid=(m // tm,),
        in_specs=[
            pl.BlockSpec((tm, k), lambda i: (i, 0)),
            pl.BlockSpec((1, k), lambda i: (0, 0)),
            pl.BlockSpec((k, n), lambda i: (0, 0), pipeline_mode=pl.Buffered(1)),
        ],
        out_specs=pl.BlockSpec((tm, n), lambda i: (i, 0)),
        out_shape=jax.ShapeDtypeStruct((m, n), BF16),
        compiler_params=_params(("parallel",)),
        name="norm_matmul",
    )(x, g, w)


def _mlp_kernel(x_ref, g_ref, wu_ref, wd_ref, fg_ref, o_ref, h_ref, acc_ref, *, final_norm):
    f = pl.program_id(1)

    @pl.when(f == 0)
    def _():
        h_ref[...] = _rms(x_ref[...], g_ref[...]).astype(BF16)
        acc_ref[...] = jnp.zeros_like(acc_ref)

    u = jnp.dot(h_ref[...], wu_ref[...], preferred_element_type=F32)
    u = jnp.maximum(u, 0.0)
    u = u * u
    acc_ref[...] += jnp.dot(u.astype(BF16), wd_ref[...], preferred_element_type=F32)

    @pl.when(f == pl.num_programs(1) - 1)
    def _():
        y = x_ref[...] + acc_ref[...]
        if final_norm:
            y = _rms(y, fg_ref[...])
        o_ref[...] = y


def mlp_residual(x, g, w_up, w_down, final_g, *, tm, tf, final_norm):
    m, d = x.shape
    ff = w_up.shape[1]
    return pl.pallas_call(
        functools.partial(_mlp_kernel, final_norm=final_norm),
        grid=(m // tm, ff // tf),
        in_specs=[
            pl.BlockSpec((tm, d), lambda i, f: (i, 0)),
            pl.BlockSpec((1, d), lambda i, f: (0, 0)),
            pl.BlockSpec((d, tf), lambda i, f: (0, f)),
            pl.BlockSpec((tf, d), lambda i, f: (f, 0)),
            pl.BlockSpec((1, d), lambda i, f: (0, 0)),
        ],
        out_specs=pl.BlockSpec((tm, d), lambda i, f: (i, 0)),
        out_shape=jax.ShapeDtypeStruct((m, d), F32),
        scratch_shapes=[pltpu.VMEM((tm, d), BF16), pltpu.VMEM((tm, d), F32)],
        compiler_params=_params(("parallel", "arbitrary")),
        name="mlp_residual",
    )(x, g, w_up, w_down, final_g)


def _proj_residual_kernel(x_ref, a_ref, b_ref, wa_ref, wb_ref, o_ref):
    y = jnp.dot(a_ref[...], wa_ref[...], preferred_element_type=F32)
    y = y + jnp.dot(b_ref[...], wb_ref[...], preferred_element_type=F32)
    o_ref[...] = x_ref[...] + y


def proj_residual(x, a, b, wa, wb, *, tm):
    m, d = x.shape
    ka, kb = a.shape[1], b.shape[1]
    return pl.pallas_call(
        _proj_residual_kernel,
        grid=(m // tm,),
        in_specs=[
            pl.BlockSpec((tm, d), lambda i: (i, 0)),
            pl.BlockSpec((tm, ka), lambda i: (i, 0)),
            pl.BlockSpec((tm, kb), lambda i: (i, 0)),
            pl.BlockSpec((ka, d), lambda i: (0, 0)),
            pl.BlockSpec((kb, d), lambda i: (0, 0)),
        ],
        out_specs=pl.BlockSpec((tm, d), lambda i: (i, 0)),
        out_shape=jax.ShapeDtypeStruct((m, d), F32),
        compiler_params=_params(("parallel",)),
        name="proj_residual",
    )(x, a, b, wa, wb)


def _mla_up_kernel(ql_ref, kvl_ref, kpe_ref, qn_ref, kvn_ref, wq_ref, wk_ref, wv_ref, cos_ref, sin_ref,
                   q_ref, k_ref, v_ref):
    cos = cos_ref[...]
    sin = sin_ref[...]

    def rope(t):
        return t * cos + pltpu.roll(t, LANES // 2, 1) * sin

    hq = _rms(ql_ref[...].astype(F32), qn_ref[...]).astype(BF16)
    yq = jnp.dot(hq, wq_ref[...], preferred_element_type=F32)
    hkv = _rms(kvl_ref[...].astype(F32), kvn_ref[...]).astype(BF16)
    yk = jnp.dot(hkv, wk_ref[...], preferred_element_type=F32)
    yv = jnp.dot(hkv, wv_ref[...], preferred_element_type=F32)
    krot = rope(kpe_ref[...].astype(F32)).astype(BF16)
    for h in range(MLA_HEADS):
        lo = h * MLA_QK_PAD
        mid = lo + MLA_NOPE_DIM
        hi = lo + MLA_QK_PAD
        q_ref[:, lo:mid] = yq[:, lo:mid].astype(BF16)
        q_ref[:, mid:hi] = rope(yq[:, mid:hi]).astype(BF16)
        k_ref[:, lo:mid] = yk[:, h * MLA_NOPE_DIM:(h + 1) * MLA_NOPE_DIM].astype(BF16)
        k_ref[:, mid:hi] = krot
    v_ref[...] = yv.astype(BF16)


def mla_up(z, qn, kvn, wq, wk, wv, cos, sin, *, seq, tm):
    t = z.shape[0]
    r = MLA_Q_RANK
    nq = MLA_HEADS * MLA_QK_PAD
    nv = MLA_HEADS * MLA_V_DIM
    kpe_blk = (EVEN_Z - LANES) // LANES
    pos_blocks = seq // tm
    return pl.pallas_call(
        _mla_up_kernel,
        grid=(t // tm,),
        in_specs=[
            pl.BlockSpec((tm, r), lambda i: (i, 0)),
            pl.BlockSpec((tm, r), lambda i: (i, 1)),
            pl.BlockSpec((tm, LANES), lambda i: (i, kpe_blk)),
            pl.BlockSpec((1, r), lambda i: (0, 0)),
            pl.BlockSpec((1, r), lambda i: (0, 0)),
            pl.BlockSpec((r, nq), lambda i: (0, 0)),
            pl.BlockSpec((r, nv), lambda i: (0, 0)),
            pl.BlockSpec((r, nv), lambda i: (0, 0)),
            pl.BlockSpec((tm, LANES), lambda i: (i % pos_blocks, 0)),
            pl.BlockSpec((tm, LANES), lambda i: (i % pos_blocks, 0)),
        ],
        out_specs=[
            pl.BlockSpec((tm, nq), lambda i: (i, 0)),
            pl.BlockSpec((tm, nq), lambda i: (i, 0)),
            pl.BlockSpec((tm, nv), lambda i: (i, 0)),
        ],
        out_shape=[
            jax.ShapeDtypeStruct((t, nq), BF16),
            jax.ShapeDtypeStruct((t, nq), BF16),
            jax.ShapeDtypeStruct((t, nv), BF16),
        ],
        compiler_params=_params(("parallel",)),
        name="mla_up",
    )(z, z, z, qn, kvn, wq, wk, wv, cos, sin)


def _softmax_steps_t(sts, vts, carries):
    stats = []
    for st, (m, l, _) in zip(sts, carries):
        m_new = jnp.maximum(m, jnp.max(st, axis=0, keepdims=True))
        p = jnp.exp2(st - m_new)
        alpha = jnp.exp2(m - m_new)
        stats.append((m_new, alpha * l + jnp.sum(p, axis=0, keepdims=True), alpha, p.astype(BF16)))
    return tuple((m_new, l_new, alpha * acc + jnp.dot(vt, p, preferred_element_type=F32))
                 for (m_new, l_new, alpha, p), vt, (_, _, acc) in zip(stats, vts, carries))


def _softmax_init(dv, nq):
    return (jnp.full((1, nq), -jnp.inf, F32), jnp.zeros((1, nq), F32), jnp.zeros((dv, nq), F32))


def _transpose_bf16(x):
    return x.astype(F32).T.astype(BF16)


def _mla_attn_kernel(q_ref, k_ref, v_ref, o_ref, vt_ref, *, blk, heads):
    qi = pl.program_id(2)
    dq, dv = MLA_QK_PAD, MLA_V_DIM

    @pl.when(qi == 0)
    def _():
        def fill(n, c):
            start = pl.multiple_of(n * blk, blk)
            for g in range(heads):
                vt_ref[g * dv:(g + 1) * dv, pl.ds(start, blk)] = _transpose_bf16(
                    v_ref[pl.ds(start, blk), g * dv:(g + 1) * dv])
            return c

        lax.fori_loop(0, k_ref.shape[0] // blk, fill, 0)

    qts = [_transpose_bf16(q_ref[:, g * dq:(g + 1) * dq]) for g in range(heads)]
    tk = 2 * blk

    def tile(start, nk, carries, mask):
        sts = [jnp.dot(k_ref[pl.ds(start, nk), g * dq:(g + 1) * dq], qts[g], preferred_element_type=F32)
               for g in range(heads)]
        if mask is not None:
            sts = [jnp.where(mask, st, -jnp.inf) for st in sts]
        vts = [vt_ref[g * dv:(g + 1) * dv, pl.ds(start, nk)] for g in range(heads)]
        return _softmax_steps_t(sts, vts, carries)

    def finish(carries):
        for g in range(heads):
            _, l, acc = carries[g]
            o_ref[:, g * dv:(g + 1) * dv] = (acc / l).T.astype(o_ref.dtype)

    def causal(nk):
        key = lax.broadcasted_iota(jnp.int32, (nk, blk), 0)
        qry = lax.broadcasted_iota(jnp.int32, (nk, blk), 1)
        return key <= qry + (nk - blk)

    init = tuple(_softmax_init(dv, blk) for _ in range(heads))
    carries = lax.fori_loop(0, qi // 2, lambda n, c: tile(pl.multiple_of(n * tk, tk), tk, c, None), init)

    @pl.when(qi % 2 == 0)
    def _():
        finish(tile(pl.multiple_of(qi * blk, blk), blk, carries, causal(blk)))

    @pl.when(qi % 2 == 1)
    def _():
        finish(tile(pl.multiple_of((qi - 1) * blk, blk), tk, carries, causal(tk)))


def mla_attention(q, k, v, *, batch, seq, blk, heads):
    dq = MLA_QK_PAD * heads
    dv = MLA_V_DIM * heads
    return pl.pallas_call(
        functools.partial(_mla_attn_kernel, blk=blk, heads=heads),
        grid=(batch, MLA_HEADS // heads, seq // blk),
        in_specs=[
            pl.BlockSpec((None, blk, dq), lambda b, h, i: (b, i, h)),
            pl.BlockSpec((None, seq, dq), lambda b, h, i: (b, 0, h), pipeline_mode=pl.Buffered(1)),
            pl.BlockSpec((None, seq, dv), lambda b, h, i: (b, 0, h), pipeline_mode=pl.Buffered(1)),
        ],
        out_specs=pl.BlockSpec((None, blk, dv), lambda b, h, i: (b, i, h)),
        out_shape=jax.ShapeDtypeStruct((batch, seq, MLA_HEADS * MLA_V_DIM), BF16),
        scratch_shapes=[pltpu.VMEM((dv, seq), BF16)],
        compiler_params=_params(("parallel", "parallel", "arbitrary")),
        name="mla_attention",
    )(q, k, v)


MOBA_NB_PAD = 16
MOBA_MASKED = -1e30


def _moba_kernel(slopes_ref, q_ref, k_ref, v_ref, o_ref, vt_ref, kmean_ref, sel_ref, *, nb, heads):
    blk = MOBA_BLOCK
    dh = MOBA_HEAD_DIM
    nbp = kmean_ref.shape[1]
    hg = pl.program_id(1)
    i = pl.program_id(2)

    @pl.when(i == 0)
    def _():
        kmean_ref[...] = jnp.zeros_like(kmean_ref)

        def fill(n, c):
            start = pl.multiple_of(n * blk, blk)
            for g in range(heads):
                kn = k_ref[pl.ds(start, blk), g * dh:(g + 1) * dh].astype(F32)
                kmean_ref[g, pl.ds(n, 1), :] = jnp.mean(kn, axis=0, keepdims=True)
                vt_ref[g * dh:(g + 1) * dh, pl.ds(start, blk)] = _transpose_bf16(
                    v_ref[pl.ds(start, blk), g * dh:(g + 1) * dh])
            return c

        lax.fori_loop(0, nb, fill, 0)

    blk_id = lax.broadcasted_iota(jnp.int32, (nbp, blk), 0)
    past = blk_id < i
    tk = 2 * blk
    key = lax.broadcasted_iota(jnp.int32, (tk, blk), 0)
    qry = lax.broadcasted_iota(jnp.int32, (tk, blk), 1)
    keyf = key.astype(F32)
    key1 = lax.broadcasted_iota(jnp.int32, (blk, blk), 0)
    qry1 = lax.broadcasted_iota(jnp.int32, (blk, blk), 1)
    keyf1 = key1.astype(F32)

    qts, colbias, slopes = [], [], []
    for g in range(heads):
        qt = _transpose_bf16(q_ref[:, g * dh:(g + 1) * dh])
        km = kmean_ref[g]
        km_hi = km.astype(BF16)
        km_lo = (km - km_hi.astype(F32)).astype(BF16)
        gate = (jnp.dot(km_hi, qt, preferred_element_type=F32)
                + jnp.dot(km_lo, qt, preferred_element_type=F32))
        gm = jnp.where(past, gate, -jnp.inf)
        rank = jnp.zeros((nbp, blk), jnp.int32)
        for n in range(nb):
            gn = gm[n:n + 1, :]
            beats = (gn > gm) | ((gn == gm) & (blk_id > n))
            rank = rank + beats.astype(jnp.int32)
        sel_ref[g] = jnp.where(past & (rank < min(MOBA_TOPK, nb)), 0.0, MOBA_MASKED)
        slope = slopes_ref[hg * heads + g]
        qts.append(qt)
        slopes.append(slope)
        colbias.append(keyf * slope)

    def tile(b0, carries, gates, causal):
        nblk = len(gates[0])
        nk = nblk * blk
        start = pl.multiple_of(b0 * blk, blk)
        sts = [jnp.dot(k_ref[pl.ds(start, nk), g * dh:(g + 1) * dh], qts[g], preferred_element_type=F32)
               for g in range(heads)]
        biased = []
        for g in range(heads):
            shift = jnp.full((1, blk), (b0 - i) * blk, jnp.int32).astype(F32) * slopes[g]
            if nblk == 1:
                st = sts[g] + (keyf1 * slopes[g] + shift)
            else:
                rows = [shift if gate is None else gate + shift for gate in gates[g]]
                st = sts[g] + colbias[g]
                st = jnp.concatenate([st[c * blk:(c + 1) * blk] + rows[c] for c in range(nblk)], axis=0)
            if causal:
                st = jnp.where(key1 <= qry1 if nblk == 1 else key <= qry + blk, st, -jnp.inf)
            biased.append(st)
        vts = [vt_ref[g * dh:(g + 1) * dh, pl.ds(start, nk)] for g in range(heads)]
        return _softmax_steps_t(biased, vts, carries)

    def body(n, carries):
        gates = [(sel_ref[g, pl.ds(2 * n, 1), :], sel_ref[g, pl.ds(2 * n + 1, 1), :]) for g in range(heads)]
        return tile(2 * n, carries, gates, False)

    def finish(carries):
        for g in range(heads):
            _, l, acc = carries[g]
            o_ref[:, g * dh:(g + 1) * dh] = (acc / l).T.astype(o_ref.dtype)

    init = tuple(_softmax_init(dh, blk) for _ in range(heads))
    carries = lax.fori_loop(0, i // 2, body, init)

    @pl.when(i % 2 == 0)
    def _():
        finish(tile(i, carries, [(None,)] * heads, True))

    @pl.when(i % 2 == 1)
    def _():
        finish(tile(i - 1, carries, [(sel_ref[g, pl.ds(i - 1, 1), :], None) for g in range(heads)], True))


def moba_attention(z, slopes, *, batch, seq, heads):
    blk = MOBA_BLOCK
    w = MOBA_HEAD_DIM * heads
    nb = seq // blk
    qo, ko, vo = 1024 // w, 2048 // w, 3072 // w
    return pl.pallas_call(
        functools.partial(_moba_kernel, nb=nb, heads=heads),
        grid=(batch, MOBA_HEADS // heads, nb),
        in_specs=[
            pl.BlockSpec(memory_space=pltpu.SMEM),
            pl.BlockSpec((None, blk, w), lambda b, h, i: (b, i, qo + h)),
            pl.BlockSpec((None, seq, w), lambda b, h, i: (b, 0, ko + h)),
            pl.BlockSpec((None, seq, w), lambda b, h, i: (b, 0, vo + h)),
        ],
        out_specs=pl.BlockSpec((None, blk, w), lambda b, h, i: (b, i, h)),
        out_shape=jax.ShapeDtypeStruct((batch, seq, MOBA_HEADS * MOBA_HEAD_DIM), BF16),
        scratch_shapes=[
            pltpu.VMEM((w, seq), BF16),
            pltpu.VMEM((heads, max(MOBA_NB_PAD, nb), MOBA_HEAD_DIM), F32),
            pltpu.VMEM((heads, max(MOBA_NB_PAD, nb), blk), F32),
        ],
        compiler_params=_params(("parallel", "parallel", "arbitrary")),
        name="moba_attention",
    )(slopes, z, z, z)


def _band_mask(j, max_dist):
    row = lax.broadcasted_iota(jnp.int32, (BAND, 2 * BAND), 0)
    col = lax.broadcasted_iota(jnp.int32, (BAND, 2 * BAND), 1)
    dist = BAND + row - col
    mask = (dist >= 0) & (dist <= max_dist) & ((col >= BAND) | (j > 0))
    return mask, dist.astype(F32)


def _dil_kernel(slopes_ref, q_ref, k_ref, v_ref, o_ref, lse_ref, stage_ref, qd_ref, kd_ref, vd_ref, *, group):
    hb = pl.program_id(1)
    j = pl.program_id(2)
    rate = DIL_RATES[group]
    dh = DIL_HEAD_DIM
    nh = q_ref.shape[-1] // dh
    sup = BAND * rate
    chunks = q_ref.shape[0] // sup

    @pl.when(j == 0)
    def _():
        kd_ref[0] = jnp.zeros(kd_ref.shape[1:], BF16)
        vd_ref[0] = jnp.zeros(vd_ref.shape[1:], BF16)

    def split(src_ref, store):
        if rate == 1:
            for c in range(chunks):
                store(c, 0, src_ref[c * BAND:(c + 1) * BAND, :])
            return
        for hh in range(nh):
            stage_ref[0, hh] = src_ref[:, hh * dh:(hh + 1) * dh].astype(F32)
        for c in range(chunks):
            for r in range(rate):
                rows = pl.ds(c * sup + r, BAND, stride=rate)
                store(c, r, jnp.concatenate([stage_ref[0, hh, rows, :].astype(BF16) for hh in range(nh)], axis=1))

    def store_q(c, r, x):
        qd_ref[c, r] = x

    def store_k(c, r, x):
        kd_ref[1 + c, r] = x

    def store_v(c, r, x):
        vd_ref[1 + c, r] = x

    split(q_ref, store_q)
    split(k_ref, store_k)
    split(v_ref, store_v)

    def biases(first_flag):
        mask, distf = _band_mask(first_flag, DIL_WINDOWS[group] // rate)
        out = []
        for hh in range(nh):
            slope = slopes_ref[group * DIL_HEADS_PER_GROUP + hb * nh + hh] * rate
            out.append(jnp.where(mask, -slope * distf, -jnp.inf))
        return out

    sls = [slice(hh * dh, (hh + 1) * dh) for hh in range(nh)]
    for c in range(chunks):
        bias = biases(j if c == 0 else 1)

        def sub(r, carry, c=c, bias=bias):
            q = qd_ref[c, r]
            kw = jnp.concatenate([kd_ref[c, r], kd_ref[c + 1, r]], axis=0)
            vw = jnp.concatenate([vd_ref[c, r], vd_ref[c + 1, r]], axis=0)
            ss = [_qk(q[:, sl], kw[:, sl]) + bias[hh] for hh, sl in enumerate(sls)]
            stats = []
            for s in ss:
                m = jnp.max(s, axis=-1, keepdims=True)
                p = jnp.exp2(s - m)
                stats.append((m, jnp.sum(p, axis=-1, keepdims=True), p.astype(BF16)))
            for hh, ((m, l, p), sl) in enumerate(zip(stats, sls)):
                out = jnp.dot(p, vw[:, sl], preferred_element_type=F32) / l
                lse = jnp.broadcast_to(m + jnp.log2(l), (BAND, dh))
                if rate == 1:
                    o_ref[c * BAND:(c + 1) * BAND, sl] = out
                    lse_ref[c * BAND:(c + 1) * BAND, sl] = lse
                else:
                    rows = pl.ds(c * sup + r, BAND, stride=rate)
                    stage_ref[0, hh, rows, :] = out
                    stage_ref[1, hh, rows, :] = lse
            return carry

        if rate == 1:
            sub(0, 0)
        else:
            lax.fori_loop(0, rate, sub, 0)

    if rate > 1:
        for hh in range(nh):
            o_ref[:, sls[hh]] = stage_ref[0, hh]
            lse_ref[:, sls[hh]] = stage_ref[1, hh]
    kd_ref[0] = kd_ref[chunks]
    vd_ref[0] = vd_ref[chunks]


def dilated_group(z, slopes, *, group, batch, seq, heads, chunks):
    rate = DIL_RATES[group]
    tile = BAND * rate * chunks
    gw = DIL_HEADS_PER_GROUP * DIL_HEAD_DIM
    w = heads * DIL_HEAD_DIM
    nhb = DIL_HEADS_PER_GROUP // heads
    qo, ko, vo = (group * gw // w, (N_DIL + group) * gw // w, (2 * N_DIL + group) * gw // w)
    stage_rows = tile if rate > 1 else 8

    def spec(off):
        return pl.BlockSpec((None, tile, w), lambda b, h, j: (b, j, off + h))

    out_spec = pl.BlockSpec((None, tile, w), lambda b, h, j: (b, j, h))
    o, lse = pl.pallas_call(
        functools.partial(_dil_kernel, group=group),
        grid=(batch, nhb, seq // tile),
        in_specs=[pl.BlockSpec(memory_space=pltpu.SMEM), spec(qo), spec(ko), spec(vo)],
        out_specs=[out_spec, out_spec],
        out_shape=[jax.ShapeDtypeStruct((batch, seq, gw), F32)] * 2,
        scratch_shapes=[
            pltpu.VMEM((2, heads, stage_rows, DIL_HEAD_DIM), F32),
            pltpu.VMEM((chunks, rate, BAND, w), BF16),
            pltpu.VMEM((chunks + 1, rate, BAND, w), BF16),
            pltpu.VMEM((chunks + 1, rate, BAND, w), BF16),
        ],
        compiler_params=_params(("parallel", "parallel", "arbitrary")),
        name=f"dilated_group{group}",
    )(slopes, z, z, z)
    return o.reshape(batch * seq, gw), lse.reshape(batch * seq, gw)


def _odd_proj_kernel(x_ref, o0_ref, l0_ref, o1_ref, l1_ref, o2_ref, l2_ref, d_ref, wc_ref, wd_ref, out_ref):
    l0, l1, l2 = l0_ref[...], l1_ref[...], l2_ref[...]
    mx = jnp.maximum(jnp.maximum(l0, l1), l2)
    w0, w1, w2 = jnp.exp2(l0 - mx), jnp.exp2(l1 - mx), jnp.exp2(l2 - mx)
    num = o0_ref[...] * w0 + o1_ref[...] * w1 + o2_ref[...] * w2
    c = (num / (w0 + w1 + w2)).astype(BF16)
    y = jnp.dot(c, wc_ref[...], preferred_element_type=F32)
    y = y + jnp.dot(d_ref[...], wd_ref[...], preferred_element_type=F32)
    out_ref[...] = x_ref[...] + y


def odd_proj_residual(x, parts, dd, wc, wd, *, tm):
    m, d = x.shape
    kc, kd = wc.shape[0], wd.shape[0]
    part_spec = pl.BlockSpec((tm, kc), lambda i: (i, 0))
    return pl.pallas_call(
        _odd_proj_kernel,
        grid=(m // tm,),
        in_specs=[pl.BlockSpec((tm, d), lambda i: (i, 0))] + [part_spec] * 6 + [
            pl.BlockSpec((tm, kd), lambda i: (i, 0)),
            pl.BlockSpec((kc, d), lambda i: (0, 0), pipeline_mode=pl.Buffered(1)),
            pl.BlockSpec((kd, d), lambda i: (0, 0), pipeline_mode=pl.Buffered(1)),
        ],
        out_specs=pl.BlockSpec((tm, d), lambda i: (i, 0)),
        out_shape=jax.ShapeDtypeStruct((m, d), F32),
        compiler_params=_params(("parallel",)),
        name="odd_proj_residual",
    )(x, *parts, dd, wc, wd)


def _swa_kernel(sinks_ref, qa_ref, qb_ref, kp_ref, ko_ref, vp_ref, vo_ref, o_ref, bias_ref):
    j = pl.program_id(1)
    heads_per_kv = SWA_Q_HEADS // SWA_KV_HEADS
    half = LANES // 2

    @pl.when(j == 0)
    def _():
        for later in (0, 1):
            mask, distf = _band_mask(later, SWA_WINDOW - 1)
            for head in range(SWA_Q_HEADS):
                slope = 2.0 ** (-8.0 * (head + 1) / SWA_Q_HEADS) * LOG2E
                bias_ref[later, head] = jnp.where(mask, -slope * distf, -jnp.inf)

    chunks = ko_ref.shape[0] // BAND

    kw = jnp.concatenate([kp_ref[...], ko_ref[...]], axis=0).astype(F32)
    vw = jnp.concatenate([vp_ref[...], vo_ref[...]], axis=0).astype(F32)
    low = lax.broadcasted_iota(jnp.int32, kw.shape, 1) < half

    def halves(x):
        xr = pltpu.roll(x, half, 1)
        even = (jnp.where(low, x, 0.0).astype(BF16), jnp.where(low, xr, 0.0).astype(BF16))
        odd = (jnp.where(low, 0.0, xr).astype(BF16), jnp.where(low, 0.0, x).astype(BF16))
        return even, odd

    k_even, k_odd = halves(kw)
    v_even, v_odd = halves(vw)

    for blk in range(chunks):
        later = jnp.minimum(j, 1) if blk == 0 else 1
        rows = slice(blk * BAND, (blk + 1) * BAND)
        win = slice(blk * BAND, (blk + 2) * BAND)
        for kvh in range(SWA_KV_HEADS):
            heads = list(range(kvh * heads_per_kv, (kvh + 1) * heads_per_kv))
            ss = []
            q_ref = (qa_ref, qb_ref)[kvh]
            for head in heads:
                pair = (head - heads[0]) // 2
                slab = q_ref[rows, pair * LANES:(pair + 1) * LANES]
                kk = (k_even if head % 2 == 0 else k_odd)[kvh][win]
                ss.append(_qk(slab, kk) + bias_ref[later, head])
            stats = []
            for s in ss:
                m = jnp.max(s, axis=-1, keepdims=True)
                p = jnp.exp2(s - m)
                stats.append((m, jnp.sum(p, axis=-1, keepdims=True), p.astype(BF16)))
            outs = {}
            for head, (m, l, p) in zip(heads, stats):
                vv = (v_even if head % 2 == 0 else v_odd)[kvh][win]
                acc = jnp.dot(p, vv, preferred_element_type=F32)
                sk = sinks_ref[head] * LOG2E
                mx = jnp.maximum(m, sk)
                a = jnp.exp2(m - mx)
                outs[head] = acc * a / (l * a + jnp.exp2(sk - mx))
            for pair in range(heads[0] // 2, heads[-1] // 2 + 1):
                o_ref[rows, pair * LANES:(pair + 1) * LANES] = (outs[2 * pair] + outs[2 * pair + 1]).astype(o_ref.dtype)


def swa_attention(z, sinks, *, batch, seq, chunks):
    wq = SWA_Q_HEADS * SWA_HEAD_DIM
    wq_kv = wq // SWA_KV_HEADS
    q0 = 3 * N_DIL * DIL_HEADS_PER_GROUP * DIL_HEAD_DIM
    qo, ko, vo = q0 // wq_kv, (q0 + wq) // LANES, (q0 + wq) // LANES + 1

    tile = chunks * BAND

    def kvspec(off, prev):
        if prev:
            return pl.BlockSpec((None, BAND, LANES), lambda b, j: (b, jnp.maximum(j * chunks - 1, 0), off))
        return pl.BlockSpec((None, tile, LANES), lambda b, j: (b, j, off))

    return pl.pallas_call(
        _swa_kernel,
        grid=(batch, seq // tile),
        in_specs=[
            pl.BlockSpec(memory_space=pltpu.SMEM),
            pl.BlockSpec((None, tile, wq_kv), lambda b, j: (b, j, qo)),
            pl.BlockSpec((None, tile, wq_kv), lambda b, j: (b, j, qo + 1)),
            kvspec(ko, True), kvspec(ko, False), kvspec(vo, True), kvspec(vo, False),
        ],
        out_specs=pl.BlockSpec((None, tile, wq), lambda b, j: (b, j, 0)),
        out_shape=jax.ShapeDtypeStruct((batch, seq, wq), BF16),
        scratch_shapes=[pltpu.VMEM((2, SWA_Q_HEADS, BAND, 2 * BAND), F32)],
        compiler_params=_params(("parallel", "arbitrary")),
        name="swa_attention",
    )(sinks, z, z, z, z, z, z)


def _rope_pad(w):
    half = MLA_ROPE_DIM // 2
    z = jnp.zeros(w.shape[:-1] + (half,), w.dtype)
    return jnp.concatenate([w[..., :half], z, w[..., half:], z], axis=-1)


def _even_weights(w_in, w_uq, w_ukv, w_out):
    hd = MOBA_HEADS * MOBA_HEAD_DIM
    o = 0
    q_lat = w_in[:, o:o + MLA_Q_RANK]; o += MLA_Q_RANK
    kv_lat = w_in[:, o:o + MLA_KV_RANK]; o += MLA_KV_RANK
    k_pe = w_in[:, o:o + MLA_ROPE_DIM]; o += MLA_ROPE_DIM
    qb = w_in[:, o:o + hd] * (MOBA_HEAD_DIM ** -0.5 * LOG2E); o += hd
    kb = w_in[:, o:o + hd]; o += hd
    vb = w_in[:, o:o + hd]
    w_in_p = jnp.concatenate([q_lat, kv_lat, qb, kb, vb, _rope_pad(k_pe)], axis=-1).astype(BF16)

    r = w_uq.shape[0]
    uq = w_uq.reshape(r, MLA_HEADS, MLA_NOPE_DIM + MLA_ROPE_DIM) * ((MLA_NOPE_DIM + MLA_ROPE_DIM) ** -0.5 * LOG2E)
    wq = jnp.concatenate([uq[..., :MLA_NOPE_DIM], _rope_pad(uq[..., MLA_NOPE_DIM:])], axis=-1)
    wq = wq.reshape(r, MLA_HEADS * MLA_QK_PAD).astype(BF16)
    ukv = w_ukv.reshape(w_ukv.shape[0], MLA_HEADS, MLA_NOPE_DIM + MLA_V_DIM)
    wk = ukv[..., :MLA_NOPE_DIM].reshape(-1, MLA_HEADS * MLA_NOPE_DIM).astype(BF16)
    wv = ukv[..., MLA_NOPE_DIM:].reshape(-1, MLA_HEADS * MLA_V_DIM).astype(BF16)
    na = MLA_HEADS * MLA_V_DIM
    return w_in_p, wq, wk, wv, w_out[:na].astype(BF16), w_out[na:].astype(BF16)


def _odd_weights(w_in, w_out):
    cd = N_DIL * DIL_HEADS_PER_GROUP * DIL_HEAD_DIM
    qd_w = SWA_Q_HEADS * SWA_HEAD_DIM
    kd_w = SWA_KV_HEADS * SWA_HEAD_DIM
    scale = jnp.concatenate([
        jnp.full((cd,), DIL_HEAD_DIM ** -0.5 * LOG2E, F32), jnp.ones((2 * cd,), F32),
        jnp.full((qd_w,), SWA_HEAD_DIM ** -0.5 * LOG2E, F32), jnp.ones((2 * kd_w,), F32)])
    w_in_p = (w_in * scale[None, :]).astype(BF16)
    nc = DIL_HEADS_PER_GROUP * DIL_HEAD_DIM
    return w_in_p, w_out[:nc].astype(BF16), w_out[nc:].astype(BF16)


def _rope_tables(seq):
    half = MLA_ROPE_DIM // 2
    inv_freq = ROPE_THETA ** (-jnp.arange(half, dtype=F32) / half)
    ang = jnp.arange(seq, dtype=jnp.int32).astype(F32)[:, None] * inv_freq[None, :]
    c, s = jnp.cos(ang), jnp.sin(ang)
    z = jnp.zeros_like(c)
    return jnp.concatenate([c, z, c, z], axis=-1), jnp.concatenate([-s, z, s, z], axis=-1)


def kernel(x, attn_norm, mlp_norm, w_up, w_down, ev_w_in, ev_q_norm, ev_w_uq, ev_kv_norm, ev_w_ukv, ev_w_out,
           od_w_in, od_sinks, od_w_out, final_norm):
    batch, seq, d = x.shape
    depth = attn_norm.shape[0]
    t = batch * seq
    xt = x.reshape(t, d)
    cos, sin = _rope_tables(seq)
    moba_slopes = jnp.exp2(-8.0 * jnp.arange(1, MOBA_HEADS + 1, dtype=F32) / MOBA_HEADS) * LOG2E
    n_dil_heads = N_DIL * DIL_HEADS_PER_GROUP
    dil_slopes = jnp.exp2(-8.0 * jnp.arange(1, n_dil_heads + 1, dtype=F32) / n_dil_heads) * LOG2E
    fg = final_norm.reshape(1, d)

    for layer in range(depth):
        i = layer // 2
        g = attn_norm[layer].reshape(1, d)
        if layer % 2 == 0:
            w_in_p, wq, wk, wv, wo_a, wo_b = _even_weights(ev_w_in[i], ev_w_uq[i], ev_w_ukv[i], ev_w_out[i])
            z = norm_matmul(xt, g, w_in_p, tm=PROJ_ROWS)
            q, k, v = mla_up(z, ev_q_norm[i].reshape(1, -1), ev_kv_norm[i].reshape(1, -1), wq, wk, wv, cos, sin,
                             seq=seq, tm=MLA_UP_ROWS)
            a = mla_attention(q.reshape(batch, seq, -1), k.reshape(batch, seq, -1), v.reshape(batch, seq, -1),
                              batch=batch, seq=seq, blk=ATTN_BLOCK, heads=ATTN_HEADS_PER_STEP)
            b = moba_attention(z.reshape(batch, seq, EVEN_Z), moba_slopes, batch=batch, seq=seq,
                               heads=ATTN_HEADS_PER_STEP)
            xt = proj_residual(xt, a.reshape(t, -1), b.reshape(t, -1), wo_a, wo_b, tm=OUT_PROJ_ROWS)
        else:
            w_in_p, wo_c, wo_d = _odd_weights(od_w_in[i], od_w_out[i])
            z = norm_matmul(xt, g, w_in_p, tm=PROJ_ROWS)
            z3 = z.reshape(batch, seq, ODD_Z)
            parts = []
            for grp in range(N_DIL):
                parts.extend(dilated_group(z3, dil_slopes, group=grp, batch=batch, seq=seq,
                                           heads=DIL_GROUP_HEADS_PER_STEP[grp], chunks=DIL_GROUP_CHUNKS[grp]))
            dd = swa_attention(z3, od_sinks[i], batch=batch, seq=seq, chunks=SWA_CHUNKS)
            xt = odd_proj_residual(xt, parts, dd.reshape(t, -1), wo_c, wo_d, tm=OUT_PROJ_ROWS)
        xt = mlp_residual(xt, mlp_norm[layer].reshape(1, d), w_up[layer].astype(BF16), w_down[layer].astype(BF16), fg,
                          tm=MLP_ROWS, tf=MLP_FF_COLS, final_norm=(layer == depth - 1))
    return xt.reshape(batch, seq, d)
```

```python
import functools

import jax
import jax.numpy as jnp
from jax import lax
from jax.experimental import pallas as pl
from jax.experimental.pallas import tpu as pltpu

F32 = jnp.float32
BF16 = jnp.bfloat16

NORM_EPS = 1e-6
D_FF_MULT = 4

MLA_HEADS = 8
MLA_Q_RANK = 512
MLA_KV_RANK = 512
MLA_NOPE_DIM = 128
MLA_ROPE_DIM = 64
MLA_V_DIM = 128
ROPE_THETA = 10000.0

MOBA_HEADS = 8
MOBA_HEAD_DIM = 128
MOBA_BLOCK = 256
MOBA_TOPK = 3

DIL_WINDOWS = (128, 512, 2048)
DIL_RATES = (1, 4, 16)
N_DIL = 3
DIL_HEADS_PER_GROUP = 4
DIL_HEAD_DIM = 128

SWA_Q_HEADS = 16
SWA_KV_HEADS = 2
SWA_HEAD_DIM = 64
SWA_WINDOW = 128

BAND = 128
PROJ_ROWS = 256
MLA_UP_ROWS = 512
ATTN_BLOCK = 256
ATTN_HEADS_PER_STEP = 8
OUT_PROJ_ROWS = 512
MLP_ROWS = 512
MLP_FF_COLS = 1024
SWA_CHUNKS = 4
DIL_GROUP_HEADS_PER_STEP = (4, 4, 2)
DIL_GROUP_CHUNKS = (4, 2, 1)

LANES = 128
MLA_QK_PAD = 256
EVEN_Z = 4224
ODD_Z = 5888
VMEM_LIMIT = 56 * 1024 * 1024
LOG2E = 1.4426950408889634


def _params(sem):
    return pltpu.CompilerParams(dimension_semantics=sem, vmem_limit_bytes=VMEM_LIMIT)


def _rms(x, g):
    ms = jnp.mean(x * x, axis=-1, keepdims=True)
    return x * lax.rsqrt(ms + NORM_EPS) * g


def _qk(q, k):
    return lax.dot_general(q, k, (((1,), (1,)), ((), ())), preferred_element_type=F32)


def _norm_matmul_kernel(x_ref, g_ref, w_ref, o_ref):
    h = _rms(x_ref[...], g_ref[...]).astype(BF16)
    o_ref[...] = jnp.dot(h, w_ref[...], preferred_element_type=F32).astype(o_ref.dtype)


def norm_matmul(x, g, w, *, tm):
    m, k = x.shape
    n = w.shape[1]
    return pl.pallas_call(
        _norm_matmul_kernel,
        grid=(m // tm,),
        in_specs=[
            pl.BlockSpec((tm, k), lambda i: (i, 0)),
            pl.BlockSpec((1, k), lambda i: (0, 0)),
            pl.BlockSpec((k, n), lambda i: (0, 0), pipeline_mode=pl.Buffered(1)),
        ],
        out_specs=pl.BlockSpec((tm, n), lambda i: (i, 0)),
        out_shape=jax.ShapeDtypeStruct((m, n), BF16),
        compiler_params=_params(("parallel",)),
        name="norm_matmul",
    )(x, g, w)


def _mlp_kernel(x_ref, g_ref, wu_ref, wd_ref, fg_ref, o_ref, h_ref, acc_ref, *, final_norm):
    f = pl.program_id(1)

    @pl.when(f == 0)
    def _():
        h_ref[...] = _rms(x_ref[...], g_ref[...]).astype(BF16)
        acc_ref[...] = jnp.zeros_like(acc_ref)

    u = jnp.dot(h_ref[...], wu_ref[...], preferred_element_type=F32)
    u = jnp.maximum(u, 0.0)
    u = u * u
    acc_ref[...] += jnp.dot(u.astype(BF16), wd_ref[...], preferred_element_type=F32)

    @pl.when(f == pl.num_programs(1) - 1)
    def _():
        y = x_ref[...] + acc_ref[...]
        if final_norm:
            y = _rms(y, fg_ref[...])
        o_ref[...] = y


def mlp_residual(x, g, w_up, w_down, final_g, *, tm, tf, final_norm):
    m, d = x.shape
    ff = w_up.shape[1]
    return pl.pallas_call(
        functools.partial(_mlp_kernel, final_norm=final_norm),
        grid=(m // tm, ff // tf),
        in_specs=[
            pl.BlockSpec((tm, d), lambda i, f: (i, 0)),
            pl.BlockSpec((1, d), lambda i, f: (0, 0)),
            pl.BlockSpec((d, tf), lambda i, f: (0, f)),
            pl.BlockSpec((tf, d), lambda i, f: (f, 0)),
            pl.BlockSpec((1, d), lambda i, f: (0, 0)),
        ],
        out_specs=pl.BlockSpec((tm, d), lambda i, f: (i, 0)),
        out_shape=jax.ShapeDtypeStruct((m, d), F32),
        scratch_shapes=[pltpu.VMEM((tm, d), BF16), pltpu.VMEM((tm, d), F32)],
        compiler_params=_params(("parallel", "arbitrary")),
        name="mlp_residual",
    )(x, g, w_up, w_down, final_g)


def _proj_residual_kernel(x_ref, a_ref, b_ref, wa_ref, wb_ref, o_ref):
    y = jnp.dot(a_ref[...], wa_ref[...], preferred_element_type=F32)
    y = y + jnp.dot(b_ref[...], wb_ref[...], preferred_element_type=F32)
    o_ref[...] = x_ref[...] + y


def proj_residual(x, a, b, wa, wb, *, tm):
    m, d = x.shape
    ka, kb = a.shape[1], b.shape[1]
    return pl.pallas_call(
        _proj_residual_kernel,
        grid=(m // tm,),
        in_specs=[
            pl.BlockSpec((tm, d), lambda i: (i, 0)),
            pl.BlockSpec((tm, ka), lambda i: (i, 0)),
            pl.BlockSpec((tm, kb), lambda i: (i, 0)),
            pl.BlockSpec((ka, d), lambda i: (0, 0)),
            pl.BlockSpec((kb, d), lambda i: (0, 0)),
        ],
        out_specs=pl.BlockSpec((tm, d), lambda i: (i, 0)),
        out_shape=jax.ShapeDtypeStruct((m, d), F32),
        compiler_params=_params(("parallel",)),
        name="proj_residual",
    )(x, a, b, wa, wb)


def _mla_up_kernel(ql_ref, kvl_ref, kpe_ref, qn_ref, kvn_ref, wq_ref, wk_ref, wv_ref, cos_ref, sin_ref,
                   q_ref, k_ref, v_ref):
    cos = cos_ref[...]
    sin = sin_ref[...]

    def rope(t):
        return t * cos + pltpu.roll(t, LANES // 2, 1) * sin

    hq = _rms(ql_ref[...].astype(F32), qn_ref[...]).astype(BF16)
    yq = jnp.dot(hq, wq_ref[...], preferred_element_type=F32)
    hkv = _rms(kvl_ref[...].astype(F32), kvn_ref[...]).astype(BF16)
    yk = jnp.dot(hkv, wk_ref[...], preferred_element_type=F32)
    yv = jnp.dot(hkv, wv_ref[...], preferred_element_type=F32)
    krot = rope(kpe_ref[...].astype(F32)).astype(BF16)
    for h in range(MLA_HEADS):
        lo = h * MLA_QK_PAD
        mid = lo + MLA_NOPE_DIM
        hi = lo + MLA_QK_PAD
        q_ref[:, lo:mid] = yq[:, lo:mid].astype(BF16)
        q_ref[:, mid:hi] = rope(yq[:, mid:hi]).astype(BF16)
        k_ref[:, lo:mid] = yk[:, h * MLA_NOPE_DIM:(h + 1) * MLA_NOPE_DIM].astype(BF16)
        k_ref[:, mid:hi] = krot
    v_ref[...] = yv.astype(BF16)


def mla_up(z, qn, kvn, wq, wk, wv, cos, sin, *, seq, tm):
    t = z.shape[0]
    r = MLA_Q_RANK
    nq = MLA_HEADS * MLA_QK_PAD
    nv = MLA_HEADS * MLA_V_DIM
    kpe_blk = (EVEN_Z - LANES) // LANES
    pos_blocks = seq // tm
    return pl.pallas_call(
        _mla_up_kernel,
        grid=(t // tm,),
        in_specs=[
            pl.BlockSpec((tm, r), lambda i: (i, 0)),
            pl.BlockSpec((tm, r), lambda i: (i, 1)),
            pl.BlockSpec((tm, LANES), lambda i: (i, kpe_blk)),
            pl.BlockSpec((1, r), lambda i: (0, 0)),
            pl.BlockSpec((1, r), lambda i: (0, 0)),
            pl.BlockSpec((r, nq), lambda i: (0, 0)),
            pl.BlockSpec((r, nv), lambda i: (0, 0)),
            pl.BlockSpec((r, nv), lambda i: (0, 0)),
            pl.BlockSpec((tm, LANES), lambda i: (i % pos_blocks, 0)),
            pl.BlockSpec((tm, LANES), lambda i: (i % pos_blocks, 0)),
        ],
        out_specs=[
            pl.BlockSpec((tm, nq), lambda i: (i, 0)),
            pl.BlockSpec((tm, nq), lambda i: (i, 0)),
            pl.BlockSpec((tm, nv), lambda i: (i, 0)),
        ],
        out_shape=[
            jax.ShapeDtypeStruct((t, nq), BF16),
            jax.ShapeDtypeStruct((t, nq), BF16),
            jax.ShapeDtypeStruct((t, nv), BF16),
        ],
        compiler_params=_params(("parallel",)),
        name="mla_up",
    )(z, z, z, qn, kvn, wq, wk, wv, cos, sin)


def _softmax_steps_t(sts, vts, carries):
    stats = []
    for st, (m, l, _) in zip(sts, carries):
        m_new = jnp.maximum(m, jnp.max(st, axis=0, keepdims=True))
        p = jnp.exp2(st - m_new)
        alpha = jnp.exp2(m - m_new)
        stats.append((m_new, alpha * l + jnp.sum(p, axis=0, keepdims=True), alpha, p.astype(BF16)))
    return tuple((m_new, l_new, alpha * acc + jnp.dot(vt, p, preferred_element_type=F32))
                 for (m_new, l_new, alpha, p), vt, (_, _, acc) in zip(stats, vts, carries))


def _softmax_init(dv, nq):
    return (jnp.full((1, nq), -jnp.inf, F32), jnp.zeros((1, nq), F32), jnp.zeros((dv, nq), F32))


def _transpose_bf16(x):
    return x.astype(F32).T.astype(BF16)


def _mla_attn_kernel(q_ref, k_ref, v_ref, o_ref, vt_ref, *, blk, heads):
    qi = pl.program_id(2)
    dq, dv = MLA_QK_PAD, MLA_V_DIM

    @pl.when(qi == 0)
    def _():
        def fill(n, c):
            start = pl.multiple_of(n * blk, blk)
            for g in range(heads):
                vt_ref[g * dv:(g + 1) * dv, pl.ds(start, blk)] = _transpose_bf16(
                    v_ref[pl.ds(start, blk), g * dv:(g + 1) * dv])
            return c

        lax.fori_loop(0, k_ref.shape[0] // blk, fill, 0)

    qts = [_transpose_bf16(q_ref[:, g * dq:(g + 1) * dq]) for g in range(heads)]
    tk = 2 * blk

    def tile(start, nk, carries, mask):
        sts = [jnp.dot(k_ref[pl.ds(start, nk), g * dq:(g + 1) * dq], qts[g], preferred_element_type=F32)
               for g in range(heads)]
        if mask is not None:
            sts = [jnp.where(mask, st, -jnp.inf) for st in sts]
        vts = [vt_ref[g * dv:(g + 1) * dv, pl.ds(start, nk)] for g in range(heads)]
        return _softmax_steps_t(sts, vts, carries)

    def finish(carries):
        for g in range(heads):
            _, l, acc = carries[g]
            o_ref[:, g * dv:(g + 1) * dv] = (acc / l).T.astype(o_ref.dtype)

    def causal(nk):
        key = lax.broadcasted_iota(jnp.int32, (nk, blk), 0)
        qry = lax.broadcasted_iota(jnp.int32, (nk, blk), 1)
        return key <= qry + (nk - blk)

    init = tuple(_softmax_init(dv, blk) for _ in range(heads))
    carries = lax.fori_loop(0, qi // 2, lambda n, c: tile(pl.multiple_of(n * tk, tk), tk, c, None), init)

    @pl.when(qi % 2 == 0)
    def _():
        finish(tile(pl.multiple_of(qi * blk, blk), blk, carries, causal(blk)))

    @pl.when(qi % 2 == 1)
    def _():
        finish(tile(pl.multiple_of((qi - 1) * blk, blk), tk, carries, causal(tk)))


def mla_attention(q, k, v, *, batch, seq, blk, heads):
    dq = MLA_QK_PAD * heads
    dv = MLA_V_DIM * heads
    return pl.pallas_call(
        functools.partial(_mla_attn_kernel, blk=blk, heads=heads),
        grid=(batch, MLA_HEADS // heads, seq // blk),
        in_specs=[
            pl.BlockSpec((None, blk, dq), lambda b, h, i: (b, i, h)),
            pl.BlockSpec((None, seq, dq), lambda b, h, i: (b, 0, h), pipeline_mode=pl.Buffered(1)),
            pl.BlockSpec((None, seq, dv), lambda b, h, i: (b, 0, h)),
        ],
        out_specs=pl.BlockSpec((None, blk, dv), lambda b, h, i: (b, i, h)),
        out_shape=jax.ShapeDtypeStruct((batch, seq, MLA_HEADS * MLA_V_DIM), BF16),
        scratch_shapes=[pltpu.VMEM((dv, seq), BF16)],
        compiler_params=_params(("parallel", "parallel", "arbitrary")),
        name="mla_attention",
    )(q, k, v)


MOBA_NB_PAD = 16
MOBA_MASKED = -1e30


def _moba_kernel(slopes_ref, q_ref, k_ref, v_ref, o_ref, vt_ref, kmean_ref, sel_ref, *, nb, heads):
    blk = MOBA_BLOCK
    dh = MOBA_HEAD_DIM
    nbp = kmean_ref.shape[1]
    hg = pl.program_id(1)
    i = pl.program_id(2)

    @pl.when(i == 0)
    def _():
        kmean_ref[...] = jnp.zeros_like(kmean_ref)

        def fill(n, c):
            start = pl.multiple_of(n * blk, blk)
            for g in range(heads):
                kn = k_ref[pl.ds(start, blk), g * dh:(g + 1) * dh].astype(F32)
                kmean_ref[g, pl.ds(n, 1), :] = jnp.mean(kn, axis=0, keepdims=True)
                vt_ref[g * dh:(g + 1) * dh, pl.ds(start, blk)] = _transpose_bf16(
                    v_ref[pl.ds(start, blk), g * dh:(g + 1) * dh])
            return c

        lax.fori_loop(0, nb, fill, 0)

    blk_id = lax.broadcasted_iota(jnp.int32, (nbp, blk), 0)
    past = blk_id < i
    tk = 2 * blk
    key = lax.broadcasted_iota(jnp.int32, (tk, blk), 0)
    qry = lax.broadcasted_iota(jnp.int32, (tk, blk), 1)
    keyf = key.astype(F32)
    key1 = lax.broadcasted_iota(jnp.int32, (blk, blk), 0)
    qry1 = lax.broadcasted_iota(jnp.int32, (blk, blk), 1)
    keyf1 = key1.astype(F32)

    qts, colbias, slopes = [], [], []
    for g in range(heads):
        qt = _transpose_bf16(q_ref[:, g * dh:(g + 1) * dh])
        km = kmean_ref[g]
        km_hi = km.astype(BF16)
        km_lo = (km - km_hi.astype(F32)).astype(BF16)
        gate = (jnp.dot(km_hi, qt, preferred_element_type=F32)
                + jnp.dot(km_lo, qt, preferred_element_type=F32))
        gm = jnp.where(past, gate, -jnp.inf)
        rank = jnp.zeros((nbp, blk), jnp.int32)
        for n in range(nb):
            gn = gm[n:n + 1, :]
            beats = (gn > gm) | ((gn == gm) & (blk_id > n))
            rank = rank + beats.astype(jnp.int32)
        sel_ref[g] = jnp.where(past & (rank < min(MOBA_TOPK, nb)), 0.0, MOBA_MASKED)
        slope = slopes_ref[hg * heads + g]
        qts.append(qt)
        slopes.append(slope)
        colbias.append(keyf * slope)

    def tile(b0, carries, gates, causal):
        nblk = len(gates[0])
        nk = nblk * blk
        start = pl.multiple_of(b0 * blk, blk)
        sts = [jnp.dot(k_ref[pl.ds(start, nk), g * dh:(g + 1) * dh], qts[g], preferred_element_type=F32)
               for g in range(heads)]
        biased = []
        for g in range(heads):
            shift = jnp.full((1, blk), (b0 - i) * blk, jnp.int32).astype(F32) * slopes[g]
            if nblk == 1:
                st = sts[g] + (keyf1 * slopes[g] + shift)
            else:
                rows = [shift if gate is None else gate + shift for gate in gates[g]]
                st = sts[g] + colbias[g]
                st = jnp.concatenate([st[c * blk:(c + 1) * blk] + rows[c] for c in range(nblk)], axis=0)
            if causal:
                st = jnp.where(key1 <= qry1 if nblk == 1 else key <= qry + blk, st, -jnp.inf)
            biased.append(st)
        vts = [vt_ref[g * dh:(g + 1) * dh, pl.ds(start, nk)] for g in range(heads)]
        return _softmax_steps_t(biased, vts, carries)

    def body(n, carries):
        gates = [(sel_ref[g, pl.ds(2 * n, 1), :], sel_ref[g, pl.ds(2 * n + 1, 1), :]) for g in range(heads)]
        return tile(2 * n, carries, gates, False)

    def finish(carries):
        for g in range(heads):
            _, l, acc = carries[g]
            o_ref[:, g * dh:(g + 1) * dh] = (acc / l).T.astype(o_ref.dtype)

    init = tuple(_softmax_init(dh, blk) for _ in range(heads))
    carries = lax.fori_loop(0, i // 2, body, init)

    @pl.when(i % 2 == 0)
    def _():
        finish(tile(i, carries, [(None,)] * heads, True))

    @pl.when(i % 2 == 1)
    def _():
        finish(tile(i - 1, carries, [(sel_ref[g, pl.ds(i - 1, 1), :], None) for g in range(heads)], True))


def moba_attention(z, slopes, *, batch, seq, heads):
    blk = MOBA_BLOCK
    w = MOBA_HEAD_DIM * heads
    nb = seq // blk
    qo, ko, vo = 1024 // w, 2048 // w, 3072 // w
    return pl.pallas_call(
        functools.partial(_moba_kernel, nb=nb, heads=heads),
        grid=(batch, MOBA_HEADS // heads, nb),
        in_specs=[
            pl.BlockSpec(memory_space=pltpu.SMEM),
            pl.BlockSpec((None, blk, w), lambda b, h, i: (b, i, qo + h)),
            pl.BlockSpec((None, seq, w), lambda b, h, i: (b, 0, ko + h)),
            pl.BlockSpec((None, seq, w), lambda b, h, i: (b, 0, vo + h)),
        ],
        out_specs=pl.BlockSpec((None, blk, w), lambda b, h, i: (b, i, h)),
        out_shape=jax.ShapeDtypeStruct((batch, seq, MOBA_HEADS * MOBA_HEAD_DIM), BF16),
        scratch_shapes=[
            pltpu.VMEM((w, seq), BF16),
            pltpu.VMEM((heads, max(MOBA_NB_PAD, nb), MOBA_HEAD_DIM), F32),
            pltpu.VMEM((heads, max(MOBA_NB_PAD, nb), blk), F32),
        ],
        compiler_params=_params(("parallel", "parallel", "arbitrary")),
        name="moba_attention",
    )(slopes, z, z, z)


def _band_mask(j, max_dist):
    row = lax.broadcasted_iota(jnp.int32, (BAND, 2 * BAND), 0)
    col = lax.broadcasted_iota(jnp.int32, (BAND, 2 * BAND), 1)
    dist = BAND + row - col
    mask = (dist >= 0) & (dist <= max_dist) & ((col >= BAND) | (j > 0))
    return mask, dist.astype(F32)


def _dil_kernel(slopes_ref, q_ref, k_ref, v_ref, o_ref, lse_ref, stage_ref, qd_ref, kd_ref, vd_ref, *, group):
    hb = pl.program_id(1)
    j = pl.program_id(2)
    rate = DIL_RATES[group]
    dh = DIL_HEAD_DIM
    nh = q_ref.shape[-1] // dh
    sup = BAND * rate
    chunks = q_ref.shape[0] // sup
    per_iter = DIL_HEADS_PER_GROUP // nh if rate > 1 else 1

    @pl.when(j == 0)
    def _():
        kd_ref[0] = jnp.zeros(kd_ref.shape[1:], BF16)
        vd_ref[0] = jnp.zeros(vd_ref.shape[1:], BF16)

    def split(src_ref, store):
        if rate == 1:
            for c in range(chunks):
                store(c, 0, src_ref[c * BAND:(c + 1) * BAND, :])
            return
        for hh in range(nh):
            stage_ref[0, hh] = src_ref[:, hh * dh:(hh + 1) * dh].astype(F32)
        for c in range(chunks):
            for r in range(rate):
                rows = pl.ds(c * sup + r, BAND, stride=rate)
                store(c, r, jnp.concatenate([stage_ref[0, hh, rows, :].astype(BF16) for hh in range(nh)], axis=1))

    def store_q(c, r, x):
        qd_ref[c, r] = x

    def store_k(c, r, x):
        kd_ref[1 + c, r] = x

    def store_v(c, r, x):
        vd_ref[1 + c, r] = x

    split(q_ref, store_q)
    split(k_ref, store_k)
    split(v_ref, store_v)

    def biases(first_flag):
        mask, distf = _band_mask(first_flag, DIL_WINDOWS[group] // rate)
        out = []
        for hh in range(nh):
            slope = slopes_ref[group * DIL_HEADS_PER_GROUP + hb * nh + hh] * rate
            out.append(jnp.where(mask, -slope * distf, -jnp.inf))
        return out

    sls = [slice(hh * dh, (hh + 1) * dh) for hh in range(nh)]
    for c in range(chunks):
        bias = biases(j if c == 0 else 1)

        def sub(it, carry, c=c, bias=bias):
            chains = []
            for u in range(per_iter):
                r = it * per_iter + u
                q = qd_ref[c, r]
                kw = jnp.concatenate([kd_ref[c, r], kd_ref[c + 1, r]], axis=0)
                vw = jnp.concatenate([vd_ref[c, r], vd_ref[c + 1, r]], axis=0)
                chains.extend((r, hh, _qk(q[:, sl], kw[:, sl]) + bias[hh], vw[:, sl]) for hh, sl in enumerate(sls))
            stats = []
            for _, _, s, _ in chains:
                m = jnp.max(s, axis=-1, keepdims=True)
                p = jnp.exp2(s - m)
                stats.append((m, jnp.sum(p, axis=-1, keepdims=True), p.astype(BF16)))
            for (r, hh, _, vwh), (m, l, p) in zip(chains, stats):
                out = jnp.dot(p, vwh, preferred_element_type=F32) / l
                lse = jnp.broadcast_to(m + jnp.log2(l), (BAND, dh))
                if rate == 1:
                    o_ref[c * BAND:(c + 1) * BAND, sls[hh]] = out
                    lse_ref[c * BAND:(c + 1) * BAND, sls[hh]] = lse
                else:
                    rows = pl.ds(c * sup + r, BAND, stride=rate)
                    stage_ref[0, hh, rows, :] = out
                    stage_ref[1, hh, rows, :] = lse
            return carry

        if rate == 1:
            sub(0, 0)
        else:
            lax.fori_loop(0, rate // per_iter, sub, 0)

    if rate > 1:
        for hh in range(nh):
            o_ref[:, sls[hh]] = stage_ref[0, hh]
            lse_ref[:, sls[hh]] = stage_ref[1, hh]
    kd_ref[0] = kd_ref[chunks]
    vd_ref[0] = vd_ref[chunks]


def dilated_group(z, slopes, *, group, batch, seq, heads, chunks):
    rate = DIL_RATES[group]
    tile = BAND * rate * chunks
    gw = DIL_HEADS_PER_GROUP * DIL_HEAD_DIM
    w = heads * DIL_HEAD_DIM
    nhb = DIL_HEADS_PER_GROUP // heads
    qo, ko, vo = (group * gw // w, (N_DIL + group) * gw // w, (2 * N_DIL + group) * gw // w)
    stage_rows = tile if rate > 1 else 8

    def spec(off):
        return pl.BlockSpec((None, tile, w), lambda b, h, j: (b, j, off + h))

    out_spec = pl.BlockSpec((None, tile, w), lambda b, h, j: (b, j, h))
    o, lse = pl.pallas_call(
        functools.partial(_dil_kernel, group=group),
        grid=(batch, nhb, seq // tile),
        in_specs=[pl.BlockSpec(memory_space=pltpu.SMEM), spec(qo), spec(ko), spec(vo)],
        out_specs=[out_spec, out_spec],
        out_shape=[jax.ShapeDtypeStruct((batch, seq, gw), F32)] * 2,
        scratch_shapes=[
            pltpu.VMEM((2, heads, stage_rows, DIL_HEAD_DIM), F32),
            pltpu.VMEM((chunks, rate, BAND, w), BF16),
            pltpu.VMEM((chunks + 1, rate, BAND, w), BF16),
            pltpu.VMEM((chunks + 1, rate, BAND, w), BF16),
        ],
        compiler_params=_params(("parallel", "parallel", "arbitrary")),
        name=f"dilated_group{group}",
    )(slopes, z, z, z)
    return o.reshape(batch * seq, gw), lse.reshape(batch * seq, gw)


def _odd_proj_kernel(x_ref, o0_ref, l0_ref, o1_ref, l1_ref, o2_ref, l2_ref, d_ref, wc_ref, wd_ref, out_ref):
    l0, l1, l2 = l0_ref[...], l1_ref[...], l2_ref[...]
    mx = jnp.maximum(jnp.maximum(l0, l1), l2)
    w0, w1, w2 = jnp.exp2(l0 - mx), jnp.exp2(l1 - mx), jnp.exp2(l2 - mx)
    num = o0_ref[...] * w0 + o1_ref[...] * w1 + o2_ref[...] * w2
    c = (num / (w0 + w1 + w2)).astype(BF16)
    y = jnp.dot(c, wc_ref[...], preferred_element_type=F32)
    y = y + jnp.dot(d_ref[...], wd_ref[...], preferred_element_type=F32)
    out_ref[...] = x_ref[...] + y


def odd_proj_residual(x, parts, dd, wc, wd, *, tm):
    m, d = x.shape
    kc, kd = wc.shape[0], wd.shape[0]
    part_spec = pl.BlockSpec((tm, kc), lambda i: (i, 0))
    return pl.pallas_call(
        _odd_proj_kernel,
        grid=(m // tm,),
        in_specs=[pl.BlockSpec((tm, d), lambda i: (i, 0))] + [part_spec] * 6 + [
            pl.BlockSpec((tm, kd), lambda i: (i, 0)),
            pl.BlockSpec((kc, d), lambda i: (0, 0), pipeline_mode=pl.Buffered(1)),
            pl.BlockSpec((kd, d), lambda i: (0, 0), pipeline_mode=pl.Buffered(1)),
        ],
        out_specs=pl.BlockSpec((tm, d), lambda i: (i, 0)),
        out_shape=jax.ShapeDtypeStruct((m, d), F32),
        compiler_params=_params(("parallel",)),
        name="odd_proj_residual",
    )(x, *parts, dd, wc, wd)


def _swa_kernel(sinks_ref, qa_ref, qb_ref, kp_ref, ko_ref, vp_ref, vo_ref, o_ref, bias_ref):
    j = pl.program_id(1)
    heads_per_kv = SWA_Q_HEADS // SWA_KV_HEADS
    half = LANES // 2

    @pl.when(j == 0)
    def _():
        for later in (0, 1):
            mask, distf = _band_mask(later, SWA_WINDOW - 1)
            for head in range(SWA_Q_HEADS):
                slope = 2.0 ** (-8.0 * (head + 1) / SWA_Q_HEADS) * LOG2E
                bias_ref[later, head] = jnp.where(mask, -slope * distf, -jnp.inf)

    chunks = ko_ref.shape[0] // BAND

    kw = jnp.concatenate([kp_ref[...], ko_ref[...]], axis=0).astype(F32)
    vw = jnp.concatenate([vp_ref[...], vo_ref[...]], axis=0).astype(F32)
    low = lax.broadcasted_iota(jnp.int32, kw.shape, 1) < half

    def halves(x):
        xr = pltpu.roll(x, half, 1)
        even = (jnp.where(low, x, 0.0).astype(BF16), jnp.where(low, xr, 0.0).astype(BF16))
        odd = (jnp.where(low, 0.0, xr).astype(BF16), jnp.where(low, 0.0, x).astype(BF16))
        return even, odd

    k_even, k_odd = halves(kw)
    v_even, v_odd = halves(vw)

    for blk in range(chunks):
        later = jnp.minimum(j, 1) if blk == 0 else 1
        rows = slice(blk * BAND, (blk + 1) * BAND)
        win = slice(blk * BAND, (blk + 2) * BAND)
        for kvh in range(SWA_KV_HEADS):
            heads = list(range(kvh * heads_per_kv, (kvh + 1) * heads_per_kv))
            ss = []
            q_ref = (qa_ref, qb_ref)[kvh]
            for head in heads:
                pair = (head - heads[0]) // 2
                slab = q_ref[rows, pair * LANES:(pair + 1) * LANES]
                kk = (k_even if head % 2 == 0 else k_odd)[kvh][win]
                ss.append(_qk(slab, kk) + bias_ref[later, head])
            stats = []
            for s in ss:
                m = jnp.max(s, axis=-1, keepdims=True)
                p = jnp.exp2(s - m)
                stats.append((m, jnp.sum(p, axis=-1, keepdims=True), p.astype(BF16)))
            outs = {}
            for head, (m, l, p) in zip(heads, stats):
                vv = (v_even if head % 2 == 0 else v_odd)[kvh][win]
                acc = jnp.dot(p, vv, preferred_element_type=F32)
                sk = sinks_ref[head] * LOG2E
                mx = jnp.maximum(m, sk)
                a = jnp.exp2(m - mx)
                outs[head] = acc * a / (l * a + jnp.exp2(sk - mx))
            for pair in range(heads[0] // 2, heads[-1] // 2 + 1):
                o_ref[rows, pair * LANES:(pair + 1) * LANES] = (outs[2 * pair] + outs[2 * pair + 1]).astype(o_ref.dtype)


def swa_attention(z, sinks, *, batch, seq, chunks):
    wq = SWA_Q_HEADS * SWA_HEAD_DIM
    wq_kv = wq // SWA_KV_HEADS
    q0 = 3 * N_DIL * DIL_HEADS_PER_GROUP * DIL_HEAD_DIM
    qo, ko, vo = q0 // wq_kv, (q0 + wq) // LANES, (q0 + wq) // LANES + 1

    tile = chunks * BAND

    def kvspec(off, prev):
        if prev:
            return pl.BlockSpec((None, BAND, LANES), lambda b, j: (b, jnp.maximum(j * chunks - 1, 0), off))
        return pl.BlockSpec((None, tile, LANES), lambda b, j: (b, j, off))

    return pl.pallas_call(
        _swa_kernel,
        grid=(batch, seq // tile),
        in_specs=[
            pl.BlockSpec(memory_space=pltpu.SMEM),
            pl.BlockSpec((None, tile, wq_kv), lambda b, j: (b, j, qo)),
            pl.BlockSpec((None, tile, wq_kv), lambda b, j: (b, j, qo + 1)),
            kvspec(ko, True), kvspec(ko, False), kvspec(vo, True), kvspec(vo, False),
        ],
        out_specs=pl.BlockSpec((None, tile, wq), lambda b, j: (b, j, 0)),
        out_shape=jax.ShapeDtypeStruct((batch, seq, wq), BF16),
        scratch_shapes=[pltpu.VMEM((2, SWA_Q_HEADS, BAND, 2 * BAND), F32)],
        compiler_params=_params(("parallel", "arbitrary")),
        name="swa_attention",
    )(sinks, z, z, z, z, z, z)


def _rope_pad(w):
    half = MLA_ROPE_DIM // 2
    z = jnp.zeros(w.shape[:-1] + (half,), w.dtype)
    return jnp.concatenate([w[..., :half], z, w[..., half:], z], axis=-1)


def _even_weights(w_in, w_uq, w_ukv, w_out):
    hd = MOBA_HEADS * MOBA_HEAD_DIM
    o = 0
    q_lat = w_in[:, o:o + MLA_Q_RANK]; o += MLA_Q_RANK
    kv_lat = w_in[:, o:o + MLA_KV_RANK]; o += MLA_KV_RANK
    k_pe = w_in[:, o:o + MLA_ROPE_DIM]; o += MLA_ROPE_DIM
    qb = w_in[:, o:o + hd] * (MOBA_HEAD_DIM ** -0.5 * LOG2E); o += hd
    kb = w_in[:, o:o + hd]; o += hd
    vb = w_in[:, o:o + hd]
    w_in_p = jnp.concatenate([q_lat, kv_lat, qb, kb, vb, _rope_pad(k_pe)], axis=-1).astype(BF16)

    r = w_uq.shape[0]
    uq = w_uq.reshape(r, MLA_HEADS, MLA_NOPE_DIM + MLA_ROPE_DIM) * ((MLA_NOPE_DIM + MLA_ROPE_DIM) ** -0.5 * LOG2E)
    wq = jnp.concatenate([uq[..., :MLA_NOPE_DIM], _rope_pad(uq[..., MLA_NOPE_DIM:])], axis=-1)
    wq = wq.reshape(r, MLA_HEADS * MLA_QK_PAD).astype(BF16)
    ukv = w_ukv.reshape(w_ukv.shape[0], MLA_HEADS, MLA_NOPE_DIM + MLA_V_DIM)
    wk = ukv[..., :MLA_NOPE_DIM].reshape(-1, MLA_HEADS * MLA_NOPE_DIM).astype(BF16)
    wv = ukv[..., MLA_NOPE_DIM:].reshape(-1, MLA_HEADS * MLA_V_DIM).astype(BF16)
    na = MLA_HEADS * MLA_V_DIM
    return w_in_p, wq, wk, wv, w_out[:na].astype(BF16), w_out[na:].astype(BF16)


def _odd_weights(w_in, w_out):
    cd = N_DIL * DIL_HEADS_PER_GROUP * DIL_HEAD_DIM
    qd_w = SWA_Q_HEADS * SWA_HEAD_DIM
    kd_w = SWA_KV_HEADS * SWA_HEAD_DIM
    scale = jnp.concatenate([
        jnp.full((cd,), DIL_HEAD_DIM ** -0.5 * LOG2E, F32), jnp.ones((2 * cd,), F32),
        jnp.full((qd_w,), SWA_HEAD_DIM ** -0.5 * LOG2E, F32), jnp.ones((2 * kd_w,), F32)])
    w_in_p = (w_in * scale[None, :]).astype(BF16)
    nc = DIL_HEADS_PER_GROUP * DIL_HEAD_DIM
    return w_in_p, w_out[:nc].astype(BF16), w_out[nc:].astype(BF16)


def _rope_tables(seq):
    half = MLA_ROPE_DIM // 2
    inv_freq = ROPE_THETA ** (-jnp.arange(half, dtype=F32) / half)
    ang = jnp.arange(seq, dtype=jnp.int32).astype(F32)[:, None] * inv_freq[None, :]
    c, s = jnp.cos(ang), jnp.sin(ang)
    z = jnp.zeros_like(c)
    return jnp.concatenate([c, z, c, z], axis=-1), jnp.concatenate([-s, z, s, z], axis=-1)


def kernel(x, attn_norm, mlp_norm, w_up, w_down, ev_w_in, ev_q_norm, ev_w_uq, ev_kv_norm, ev_w_ukv, ev_w_out,
           od_w_in, od_sinks, od_w_out, final_norm):
    batch, seq, d = x.shape
    depth = attn_norm.shape[0]
    t = batch * seq
    xt = x.reshape(t, d)
    cos, sin = _rope_tables(seq)
    moba_slopes = jnp.exp2(-8.0 * jnp.arange(1, MOBA_HEADS + 1, dtype=F32) / MOBA_HEADS) * LOG2E
    n_dil_heads = N_DIL * DIL_HEADS_PER_GROUP
    dil_slopes = jnp.exp2(-8.0 * jnp.arange(1, n_dil_heads + 1, dtype=F32) / n_dil_heads) * LOG2E
    fg = final_norm.reshape(1, d)

    for layer in range(depth):
        i = layer // 2
        g = attn_norm[layer].reshape(1, d)
        if layer % 2 == 0:
            w_in_p, wq, wk, wv, wo_a, wo_b = _even_weights(ev_w_in[i], ev_w_uq[i], ev_w_ukv[i], ev_w_out[i])
            z = norm_matmul(xt, g, w_in_p, tm=PROJ_ROWS)
            q, k, v = mla_up(z, ev_q_norm[i].reshape(1, -1), ev_kv_norm[i].reshape(1, -1), wq, wk, wv, cos, sin,
                             seq=seq, tm=MLA_UP_ROWS)
            a = mla_attention(q.reshape(batch, seq, -1), k.reshape(batch, seq, -1), v.reshape(batch, seq, -1),
                              batch=batch, seq=seq, blk=ATTN_BLOCK, heads=ATTN_HEADS_PER_STEP)
            b = moba_attention(z.reshape(batch, seq, EVEN_Z), moba_slopes, batch=batch, seq=seq,
                               heads=ATTN_HEADS_PER_STEP)
            xt = proj_residual(xt, a.reshape(t, -1), b.reshape(t, -1), wo_a, wo_b, tm=OUT_PROJ_ROWS)
        else:
            w_in_p, wo_c, wo_d = _odd_weights(od_w_in[i], od_w_out[i])
            z = norm_matmul(xt, g, w_in_p, tm=PROJ_ROWS)
            z3 = z.reshape(batch, seq, ODD_Z)
            parts = []
            for grp in range(N_DIL):
                parts.extend(dilated_group(z3, dil_slopes, group=grp, batch=batch, seq=seq,
                                           heads=DIL_GROUP_HEADS_PER_STEP[grp], chunks=DIL_GROUP_CHUNKS[grp]))
            dd = swa_attention(z3, od_sinks[i], batch=batch, seq=seq, chunks=SWA_CHUNKS)
            xt = odd_proj_residual(xt, parts, dd.reshape(t, -1), wo_c, wo_d, tm=OUT_PROJ_ROWS)
        xt = mlp_residual(xt, mlp_norm[layer].reshape(1, d), w_up[layer].astype(BF16), w_down[layer].astype(BF16), fg,
                          tm=MLP_ROWS, tf=MLP_FF_COLS, final_norm=(layer == depth - 1))
    return xt.reshape(batch, seq, d)
```

```python
import functools

import jax
import jax.numpy as jnp
from jax import lax
from jax.experimental import pallas as pl
from jax.experimental.pallas import tpu as pltpu

F32 = jnp.float32
BF16 = jnp.bfloat16

NORM_EPS = 1e-6
D_FF_MULT = 4

MLA_HEADS = 8
MLA_Q_RANK = 512
MLA_KV_RANK = 512
MLA_NOPE_DIM = 128
MLA_ROPE_DIM = 64
MLA_V_DIM = 128
ROPE_THETA = 10000.0

MOBA_HEADS = 8
MOBA_HEAD_DIM = 128
MOBA_BLOCK = 256
MOBA_TOPK = 3

DIL_WINDOWS = (128, 512, 2048)
DIL_RATES = (1, 4, 16)
N_DIL = 3
DIL_HEADS_PER_GROUP = 4
DIL_HEAD_DIM = 128

SWA_Q_HEADS = 16
SWA_KV_HEADS = 2
SWA_HEAD_DIM = 64
SWA_WINDOW = 128

BAND = 128
PROJ_ROWS_EVEN = 512
PROJ_ROWS_ODD = 256
MLA_UP_ROWS = 512
ATTN_BLOCK = 256
ATTN_HEADS_PER_STEP = 8
ATTN_KEY_BLOCKS = 4
OUT_PROJ_ROWS = 512
MLP_ROWS = 512
MLP_FF_COLS = 1024
SWA_CHUNKS = 4
DIL_GROUP_HEADS_PER_STEP = (4, 4, 2)
DIL_GROUP_CHUNKS = (4, 2, 1)

LANES = 128
MLA_QK_PAD = 256
EVEN_Z = 4224
ODD_Z = 5888
VMEM_LIMIT = 56 * 1024 * 1024
LOG2E = 1.4426950408889634


def _params(sem):
    return pltpu.CompilerParams(dimension_semantics=sem, vmem_limit_bytes=VMEM_LIMIT)


def _rms(x, g):
    ms = jnp.mean(x * x, axis=-1, keepdims=True)
    return x * lax.rsqrt(ms + NORM_EPS) * g


def _qk(q, k):
    return lax.dot_general(q, k, (((1,), (1,)), ((), ())), preferred_element_type=F32)


def _norm_matmul_kernel(x_ref, g_ref, w_ref, o_ref):
    h = _rms(x_ref[...], g_ref[...]).astype(BF16)
    o_ref[...] = jnp.dot(h, w_ref[...], preferred_element_type=F32).astype(o_ref.dtype)


def norm_matmul(x, g, w, *, tm):
    m, k = x.shape
    n = w.shape[1]
    return pl.pallas_call(
        _norm_matmul_kernel,
        grid=(m // tm,),
        in_specs=[
            pl.BlockSpec((tm, k), lambda i: (i, 0)),
            pl.BlockSpec((1, k), lambda i: (0, 0)),
            pl.BlockSpec((k, n), lambda i: (0, 0), pipeline_mode=pl.Buffered(1)),
        ],
        out_specs=pl.BlockSpec((tm, n), lambda i: (i, 0)),
        out_shape=jax.ShapeDtypeStruct((m, n), BF16),
        compiler_params=_params(("parallel",)),
        name="norm_matmul",
    )(x, g, w)


def _mlp_kernel(x_ref, g_ref, wu_ref, wd_ref, fg_ref, o_ref, h_ref, acc_ref, *, final_norm):
    f = pl.program_id(1)

    @pl.when(f == 0)
    def _():
        h_ref[...] = _rms(x_ref[...], g_ref[...]).astype(BF16)
        acc_ref[...] = jnp.zeros_like(acc_ref)

    u = jnp.dot(h_ref[...], wu_ref[...], preferred_element_type=F32)
    u = jnp.maximum(u, 0.0)
    u = u * u
    acc_ref[...] += jnp.dot(u.astype(BF16), wd_ref[...], preferred_element_type=F32)

    @pl.when(f == pl.num_programs(1) - 1)
    def _():
        y = x_ref[...] + acc_ref[...]
        if final_norm:
            y = _rms(y, fg_ref[...])
        o_ref[...] = y


def mlp_residual(x, g, w_up, w_down, final_g, *, tm, tf, final_norm):
    m, d = x.shape
    ff = w_up.shape[1]
    return pl.pallas_call(
        functools.partial(_mlp_kernel, final_norm=final_norm),
        grid=(m // tm, ff // tf),
        in_specs=[
            pl.BlockSpec((tm, d), lambda i, f: (i, 0)),
            pl.BlockSpec((1, d), lambda i, f: (0, 0)),
            pl.BlockSpec((d, tf), lambda i, f: (0, f)),
            pl.BlockSpec((tf, d), lambda i, f: (f, 0)),
            pl.BlockSpec((1, d), lambda i, f: (0, 0)),
        ],
        out_specs=pl.BlockSpec((tm, d), lambda i, f: (i, 0)),
        out_shape=jax.ShapeDtypeStruct((m, d), F32),
        scratch_shapes=[pltpu.VMEM((tm, d), BF16), pltpu.VMEM((tm, d), F32)],
        compiler_params=_params(("parallel", "arbitrary")),
        name="mlp_residual",
    )(x, g, w_up, w_down, final_g)


def _proj_residual_kernel(x_ref, a_ref, b_ref, wa_ref, wb_ref, o_ref):
    y = jnp.dot(a_ref[...], wa_ref[...], preferred_element_type=F32)
    y = y + jnp.dot(b_ref[...], wb_ref[...], preferred_element_type=F32)
    o_ref[...] = x_ref[...] + y


def proj_residual(x, a, b, wa, wb, *, tm):
    m, d = x.shape
    ka, kb = a.shape[1], b.shape[1]
    return pl.pallas_call(
        _proj_residual_kernel,
        grid=(m // tm,),
        in_specs=[
            pl.BlockSpec((tm, d), lambda i: (i, 0)),
            pl.BlockSpec((tm, ka), lambda i: (i, 0)),
            pl.BlockSpec((tm, kb), lambda i: (i, 0)),
            pl.BlockSpec((ka, d), lambda i: (0, 0)),
            pl.BlockSpec((kb, d), lambda i: (0, 0)),
        ],
        out_specs=pl.BlockSpec((tm, d), lambda i: (i, 0)),
        out_shape=jax.ShapeDtypeStruct((m, d), F32),
        compiler_params=_params(("parallel",)),
        name="proj_residual",
    )(x, a, b, wa, wb)


def _mla_up_kernel(ql_ref, kvl_ref, kpe_ref, qn_ref, kvn_ref, wq_ref, wk_ref, wv_ref, cos_ref, sin_ref,
                   q_ref, k_ref, v_ref):
    cos = cos_ref[...]
    sin = sin_ref[...]

    def rope(t):
        return t * cos + pltpu.roll(t, LANES // 2, 1) * sin

    hq = _rms(ql_ref[...].astype(F32), qn_ref[...]).astype(BF16)
    yq = jnp.dot(hq, wq_ref[...], preferred_element_type=F32)
    hkv = _rms(kvl_ref[...].astype(F32), kvn_ref[...]).astype(BF16)
    yk = jnp.dot(hkv, wk_ref[...], preferred_element_type=F32)
    yv = jnp.dot(hkv, wv_ref[...], preferred_element_type=F32)
    krot = rope(kpe_ref[...].astype(F32)).astype(BF16)
    for h in range(MLA_HEADS):
        lo = h * MLA_QK_PAD
        mid = lo + MLA_NOPE_DIM
        hi = lo + MLA_QK_PAD
        q_ref[:, lo:mid] = yq[:, lo:mid].astype(BF16)
        q_ref[:, mid:hi] = rope(yq[:, mid:hi]).astype(BF16)
        k_ref[:, lo:mid] = yk[:, h * MLA_NOPE_DIM:(h + 1) * MLA_NOPE_DIM].astype(BF16)
        k_ref[:, mid:hi] = krot
    v_ref[...] = yv.astype(BF16)


def mla_up(z, qn, kvn, wq, wk, wv, cos, sin, *, seq, tm):
    t = z.shape[0]
    r = MLA_Q_RANK
    nq = MLA_HEADS * MLA_QK_PAD
    nv = MLA_HEADS * MLA_V_DIM
    kpe_blk = (EVEN_Z - LANES) // LANES
    pos_blocks = seq // tm
    return pl.pallas_call(
        _mla_up_kernel,
        grid=(t // tm,),
        in_specs=[
            pl.BlockSpec((tm, r), lambda i: (i, 0)),
            pl.BlockSpec((tm, r), lambda i: (i, 1)),
            pl.BlockSpec((tm, LANES), lambda i: (i, kpe_blk)),
            pl.BlockSpec((1, r), lambda i: (0, 0)),
            pl.BlockSpec((1, r), lambda i: (0, 0)),
            pl.BlockSpec((r, nq), lambda i: (0, 0)),
            pl.BlockSpec((r, nv), lambda i: (0, 0)),
            pl.BlockSpec((r, nv), lambda i: (0, 0)),
            pl.BlockSpec((tm, LANES), lambda i: (i % pos_blocks, 0)),
            pl.BlockSpec((tm, LANES), lambda i: (i % pos_blocks, 0)),
        ],
        out_specs=[
            pl.BlockSpec((tm, nq), lambda i: (i, 0)),
            pl.BlockSpec((tm, nq), lambda i: (i, 0)),
            pl.BlockSpec((tm, nv), lambda i: (i, 0)),
        ],
        out_shape=[
            jax.ShapeDtypeStruct((t, nq), BF16),
            jax.ShapeDtypeStruct((t, nq), BF16),
            jax.ShapeDtypeStruct((t, nv), BF16),
        ],
        compiler_params=_params(("parallel",)),
        name="mla_up",
    )(z, z, z, qn, kvn, wq, wk, wv, cos, sin)


def _softmax_steps_t(sts, vts, carries):
    stats = []
    for st, (m, l, _) in zip(sts, carries):
        m_new = jnp.maximum(m, jnp.max(st, axis=0, keepdims=True))
        p = jnp.exp2(st - m_new)
        alpha = jnp.exp2(m - m_new)
        stats.append((m_new, alpha * l + jnp.sum(p, axis=0, keepdims=True), alpha, p.astype(BF16)))
    return tuple((m_new, l_new, alpha * acc + jnp.dot(vt, p, preferred_element_type=F32))
                 for (m_new, l_new, alpha, p), vt, (_, _, acc) in zip(stats, vts, carries))


def _softmax_init(dv, nq):
    return (jnp.full((1, nq), -jnp.inf, F32), jnp.zeros((1, nq), F32), jnp.zeros((dv, nq), F32))


def _transpose_bf16(x):
    return x.astype(F32).T.astype(BF16)


def _mla_attn_kernel(q_ref, k_ref, v_ref, o_ref, vt_ref, *, blk, heads):
    qi = pl.program_id(2)
    dq, dv = MLA_QK_PAD, MLA_V_DIM

    @pl.when(qi == 0)
    def _():
        def fill(n, c):
            start = pl.multiple_of(n * blk, blk)
            for g in range(heads):
                vt_ref[g * dv:(g + 1) * dv, pl.ds(start, blk)] = _transpose_bf16(
                    v_ref[pl.ds(start, blk), g * dv:(g + 1) * dv])
            return c

        lax.fori_loop(0, k_ref.shape[0] // blk, fill, 0)

    qts = [_transpose_bf16(q_ref[:, g * dq:(g + 1) * dq]) for g in range(heads)]
    tk = ATTN_KEY_BLOCKS * blk

    def tile(start, nk, carries, mask):
        sts = [jnp.dot(k_ref[pl.ds(start, nk), g * dq:(g + 1) * dq], qts[g], preferred_element_type=F32)
               for g in range(heads)]
        if mask is not None:
            sts = [jnp.where(mask, st, -jnp.inf) for st in sts]
        vts = [vt_ref[g * dv:(g + 1) * dv, pl.ds(start, nk)] for g in range(heads)]
        return _softmax_steps_t(sts, vts, carries)

    def finish(carries):
        for g in range(heads):
            _, l, acc = carries[g]
            o_ref[:, g * dv:(g + 1) * dv] = (acc / l).T.astype(o_ref.dtype)

    def causal(nk):
        key = lax.broadcasted_iota(jnp.int32, (nk, blk), 0)
        qry = lax.broadcasted_iota(jnp.int32, (nk, blk), 1)
        return key <= qry + (nk - blk)

    init = tuple(_softmax_init(dv, blk) for _ in range(heads))
    carries = lax.fori_loop(0, qi // ATTN_KEY_BLOCKS,
                            lambda n, c: tile(pl.multiple_of(n * tk, tk), tk, c, None), init)

    for rem in range(ATTN_KEY_BLOCKS):
        @pl.when(qi % ATTN_KEY_BLOCKS == rem)
        def _(rem=rem):
            nk = (rem + 1) * blk
            finish(tile(pl.multiple_of((qi - rem) * blk, blk), nk, carries, causal(nk)))


def mla_attention(q, k, v, *, batch, seq, blk, heads):
    dq = MLA_QK_PAD * heads
    dv = MLA_V_DIM * heads
    return pl.pallas_call(
        functools.partial(_mla_attn_kernel, blk=blk, heads=heads),
        grid=(batch, MLA_HEADS // heads, seq // blk),
        in_specs=[
            pl.BlockSpec((None, blk, dq), lambda b, h, i: (b, i, h)),
            pl.BlockSpec((None, seq, dq), lambda b, h, i: (b, 0, h), pipeline_mode=pl.Buffered(1)),
            pl.BlockSpec((None, seq, dv), lambda b, h, i: (b, 0, h)),
        ],
        out_specs=pl.BlockSpec((None, blk, dv), lambda b, h, i: (b, i, h)),
        out_shape=jax.ShapeDtypeStruct((batch, seq, MLA_HEADS * MLA_V_DIM), BF16),
        scratch_shapes=[pltpu.VMEM((dv, seq), BF16)],
        compiler_params=_params(("parallel", "parallel", "arbitrary")),
        name="mla_attention",
    )(q, k, v)


MOBA_NB_PAD = 16
MOBA_MASKED = -1e30


def _moba_kernel(slopes_ref, q_ref, k_ref, v_ref, o_ref, vt_ref, kmean_ref, sel_ref, *, nb, heads):
    blk = MOBA_BLOCK
    dh = MOBA_HEAD_DIM
    nbp = kmean_ref.shape[1]
    hg = pl.program_id(1)
    i = pl.program_id(2)

    @pl.when(i == 0)
    def _():
        kmean_ref[...] = jnp.zeros_like(kmean_ref)

        def fill(n, c):
            start = pl.multiple_of(n * blk, blk)
            for g in range(heads):
                kn = k_ref[pl.ds(start, blk), g * dh:(g + 1) * dh].astype(F32)
                kmean_ref[g, pl.ds(n, 1), :] = jnp.mean(kn, axis=0, keepdims=True)
                vt_ref[g * dh:(g + 1) * dh, pl.ds(start, blk)] = _transpose_bf16(
                    v_ref[pl.ds(start, blk), g * dh:(g + 1) * dh])
            return c

        lax.fori_loop(0, nb, fill, 0)

    blk_id = lax.broadcasted_iota(jnp.int32, (nbp, blk), 0)
    past = blk_id < i
    tk = ATTN_KEY_BLOCKS * blk
    keyf = lax.broadcasted_iota(jnp.int32, (tk, blk), 0).astype(F32)

    qts, colbias, slopes = [], [], []
    for g in range(heads):
        qt = _transpose_bf16(q_ref[:, g * dh:(g + 1) * dh])
        km = kmean_ref[g]
        km_hi = km.astype(BF16)
        km_lo = (km - km_hi.astype(F32)).astype(BF16)
        gate = (jnp.dot(km_hi, qt, preferred_element_type=F32)
                + jnp.dot(km_lo, qt, preferred_element_type=F32))
        gm = jnp.where(past, gate, -jnp.inf)
        rank = jnp.zeros((nbp, blk), jnp.int32)
        for n in range(nb):
            gn = gm[n:n + 1, :]
            beats = (gn > gm) | ((gn == gm) & (blk_id > n))
            rank = rank + beats.astype(jnp.int32)
        sel_ref[g] = jnp.where(past & (rank < min(MOBA_TOPK, nb)), 0.0, MOBA_MASKED)
        slope = slopes_ref[hg * heads + g]
        qts.append(qt)
        slopes.append(slope)
        colbias.append(keyf * slope)

    def tile(b0, carries, gates, causal):
        nblk = len(gates[0])
        nk = nblk * blk
        start = pl.multiple_of(b0 * blk, blk)
        sts = [jnp.dot(k_ref[pl.ds(start, nk), g * dh:(g + 1) * dh], qts[g], preferred_element_type=F32)
               for g in range(heads)]
        if nk == tk:
            cb = colbias
        else:
            keyf_nk = lax.broadcasted_iota(jnp.int32, (nk, blk), 0).astype(F32)
            cb = [keyf_nk * slopes[g] for g in range(heads)]
        biased = []
        for g in range(heads):
            shift = jnp.full((1, blk), (b0 - i) * blk, jnp.int32).astype(F32) * slopes[g]
            rows = [shift if gate is None else gate + shift for gate in gates[g]]
            st = sts[g] + cb[g]
            if nblk == 1:
                st = st + rows[0]
            else:
                st = jnp.concatenate([st[c * blk:(c + 1) * blk] + rows[c] for c in range(nblk)], axis=0)
            if causal:
                key_nk = lax.broadcasted_iota(jnp.int32, (nk, blk), 0)
                qry_nk = lax.broadcasted_iota(jnp.int32, (nk, blk), 1)
                st = jnp.where(key_nk <= qry_nk + (nk - blk), st, -jnp.inf)
            biased.append(st)
        vts = [vt_ref[g * dh:(g + 1) * dh, pl.ds(start, nk)] for g in range(heads)]
        return _softmax_steps_t(biased, vts, carries)

    def gate_rows(b0, nblk):
        return [tuple(sel_ref[g, pl.ds(b0 + c, 1), :] for c in range(nblk)) for g in range(heads)]

    def body(n, carries):
        return tile(ATTN_KEY_BLOCKS * n, carries, gate_rows(ATTN_KEY_BLOCKS * n, ATTN_KEY_BLOCKS), False)

    def finish(carries):
        for g in range(heads):
            _, l, acc = carries[g]
            o_ref[:, g * dh:(g + 1) * dh] = (acc / l).T.astype(o_ref.dtype)

    init = tuple(_softmax_init(dh, blk) for _ in range(heads))
    carries = lax.fori_loop(0, i // ATTN_KEY_BLOCKS, body, init)

    for rem in range(ATTN_KEY_BLOCKS):
        @pl.when(i % ATTN_KEY_BLOCKS == rem)
        def _(rem=rem):
            gates = [rows + (None,) for rows in gate_rows(i - rem, rem)]
            finish(tile(i - rem, carries, gates, True))


def moba_attention(z, slopes, *, batch, seq, heads):
    blk = MOBA_BLOCK
    w = MOBA_HEAD_DIM * heads
    nb = seq // blk
    qo, ko, vo = 1024 // w, 2048 // w, 3072 // w
    return pl.pallas_call(
        functools.partial(_moba_kernel, nb=nb, heads=heads),
        grid=(batch, MOBA_HEADS // heads, nb),
        in_specs=[
            pl.BlockSpec(memory_space=pltpu.SMEM),
            pl.BlockSpec((None, blk, w), lambda b, h, i: (b, i, qo + h)),
            pl.BlockSpec((None, seq, w), lambda b, h, i: (b, 0, ko + h), pipeline_mode=pl.Buffered(1)),
            pl.BlockSpec((None, seq, w), lambda b, h, i: (b, 0, vo + h)),
        ],
        out_specs=pl.BlockSpec((None, blk, w), lambda b, h, i: (b, i, h)),
        out_shape=jax.ShapeDtypeStruct((batch, seq, MOBA_HEADS * MOBA_HEAD_DIM), BF16),
        scratch_shapes=[
            pltpu.VMEM((w, seq), BF16),
            pltpu.VMEM((heads, max(MOBA_NB_PAD, nb), MOBA_HEAD_DIM), F32),
            pltpu.VMEM((heads, max(MOBA_NB_PAD, nb), blk), F32),
        ],
        compiler_params=_params(("parallel", "parallel", "arbitrary")),
        name="moba_attention",
    )(slopes, z, z, z)


def _band_mask(j, max_dist):
    row = lax.broadcasted_iota(jnp.int32, (BAND, 2 * BAND), 0)
    col = lax.broadcasted_iota(jnp.int32, (BAND, 2 * BAND), 1)
    dist = BAND + row - col
    mask = (dist >= 0) & (dist <= max_dist) & ((col >= BAND) | (j > 0))
    return mask, dist.astype(F32)


def _dil_kernel(slopes_ref, q_ref, k_ref, v_ref, o_ref, lse_ref, stage_ref, qd_ref, kd_ref, vd_ref, *, group):
    hb = pl.program_id(1)
    j = pl.program_id(2)
    rate = DIL_RATES[group]
    dh = DIL_HEAD_DIM
    nh = q_ref.shape[-1] // dh
    sup = BAND * rate
    chunks = q_ref.shape[0] // sup
    per_iter = DIL_HEADS_PER_GROUP // nh if rate > 1 else 1

    @pl.when(j == 0)
    def _():
        kd_ref[0] = jnp.zeros(kd_ref.shape[1:], BF16)
        vd_ref[0] = jnp.zeros(vd_ref.shape[1:], BF16)

    def split(src_ref, store):
        if rate == 1:
            for c in range(chunks):
                store(c, 0, src_ref[c * BAND:(c + 1) * BAND, :])
            return
        for hh in range(nh):
            stage_ref[0, hh] = src_ref[:, hh * dh:(hh + 1) * dh].astype(F32)
        for c in range(chunks):
            for r in range(rate):
                rows = pl.ds(c * sup + r, BAND, stride=rate)
                store(c, r, jnp.concatenate([stage_ref[0, hh, rows, :].astype(BF16) for hh in range(nh)], axis=1))

    def store_q(c, r, x):
        qd_ref[c, r] = x

    def store_k(c, r, x):
        kd_ref[1 + c, r] = x

    def store_v(c, r, x):
        vd_ref[1 + c, r] = x

    split(q_ref, store_q)
    split(k_ref, store_k)
    split(v_ref, store_v)

    def biases(first_flag):
        mask, distf = _band_mask(first_flag, DIL_WINDOWS[group] // rate)
        out = []
        for hh in range(nh):
            slope = slopes_ref[group * DIL_HEADS_PER_GROUP + hb * nh + hh] * rate
            out.append(jnp.where(mask, -slope * distf, -jnp.inf))
        return out

    sls = [slice(hh * dh, (hh + 1) * dh) for hh in range(nh)]
    for c in range(chunks):
        bias = biases(j if c == 0 else 1)

        def sub(it, carry, c=c, bias=bias):
            chains = []
            for u in range(per_iter):
                r = it * per_iter + u
                q = qd_ref[c, r]
                kw = jnp.concatenate([kd_ref[c, r], kd_ref[c + 1, r]], axis=0)
                vw = jnp.concatenate([vd_ref[c, r], vd_ref[c + 1, r]], axis=0)
                chains.extend((r, hh, _qk(q[:, sl], kw[:, sl]) + bias[hh], vw[:, sl]) for hh, sl in enumerate(sls))
            stats = []
            for _, _, s, _ in chains:
                m = jnp.max(s, axis=-1, keepdims=True)
                p = jnp.exp2(s - m)
                stats.append((m, jnp.sum(p, axis=-1, keepdims=True), p.astype(BF16)))
            for (r, hh, _, vwh), (m, l, p) in zip(chains, stats):
                out = jnp.dot(p, vwh, preferred_element_type=F32) / l
                lse = jnp.broadcast_to(m + jnp.log2(l), (BAND, dh))
                if rate == 1:
                    o_ref[c * BAND:(c + 1) * BAND, sls[hh]] = out
                    lse_ref[c * BAND:(c + 1) * BAND, sls[hh]] = lse
                else:
                    rows = pl.ds(c * sup + r, BAND, stride=rate)
                    stage_ref[0, hh, rows, :] = out
                    stage_ref[1, hh, rows, :] = lse
            return carry

        if rate == 1:
            sub(0, 0)
        else:
            lax.fori_loop(0, rate // per_iter, sub, 0)

    if rate > 1:
        for hh in range(nh):
            o_ref[:, sls[hh]] = stage_ref[0, hh]
            lse_ref[:, sls[hh]] = stage_ref[1, hh]
    kd_ref[0] = kd_ref[chunks]
    vd_ref[0] = vd_ref[chunks]


def dilated_group(z, slopes, *, group, batch, seq, heads, chunks):
    rate = DIL_RATES[group]
    tile = BAND * rate * chunks
    gw = DIL_HEADS_PER_GROUP * DIL_HEAD_DIM
    w = heads * DIL_HEAD_DIM
    nhb = DIL_HEADS_PER_GROUP // heads
    qo, ko, vo = (group * gw // w, (N_DIL + group) * gw // w, (2 * N_DIL + group) * gw // w)
    stage_rows = tile if rate > 1 else 8

    def spec(off):
        return pl.BlockSpec((None, tile, w), lambda b, h, j: (b, j, off + h))

    out_spec = pl.BlockSpec((None, tile, w), lambda b, h, j: (b, j, h))
    o, lse = pl.pallas_call(
        functools.partial(_dil_kernel, group=group),
        grid=(batch, nhb, seq // tile),
        in_specs=[pl.BlockSpec(memory_space=pltpu.SMEM), spec(qo), spec(ko), spec(vo)],
        out_specs=[out_spec, out_spec],
        out_shape=[jax.ShapeDtypeStruct((batch, seq, gw), F32)] * 2,
        scratch_shapes=[
            pltpu.VMEM((2, heads, stage_rows, DIL_HEAD_DIM), F32),
            pltpu.VMEM((chunks, rate, BAND, w), BF16),
            pltpu.VMEM((chunks + 1, rate, BAND, w), BF16),
            pltpu.VMEM((chunks + 1, rate, BAND, w), BF16),
        ],
        compiler_params=_params(("parallel", "parallel", "arbitrary")),
        name=f"dilated_group{group}",
    )(slopes, z, z, z)
    return o.reshape(batch * seq, gw), lse.reshape(batch * seq, gw)


def _odd_proj_kernel(x_ref, o0_ref, l0_ref, o1_ref, l1_ref, o2_ref, l2_ref, d_ref, wc_ref, wd_ref, out_ref):
    l0, l1, l2 = l0_ref[...], l1_ref[...], l2_ref[...]
    mx = jnp.maximum(jnp.maximum(l0, l1), l2)
    w0, w1, w2 = jnp.exp2(l0 - mx), jnp.exp2(l1 - mx), jnp.exp2(l2 - mx)
    num = o0_ref[...] * w0 + o1_ref[...] * w1 + o2_ref[...] * w2
    c = (num / (w0 + w1 + w2)).astype(BF16)
    y = jnp.dot(c, wc_ref[...], preferred_element_type=F32)
    y = y + jnp.dot(d_ref[...], wd_ref[...], preferred_element_type=F32)
    out_ref[...] = x_ref[...] + y


def odd_proj_residual(x, parts, dd, wc, wd, *, tm):
    m, d = x.shape
    kc, kd = wc.shape[0], wd.shape[0]
    part_spec = pl.BlockSpec((tm, kc), lambda i: (i, 0))
    return pl.pallas_call(
        _odd_proj_kernel,
        grid=(m // tm,),
        in_specs=[pl.BlockSpec((tm, d), lambda i: (i, 0))] + [part_spec] * 6 + [
            pl.BlockSpec((tm, kd), lambda i: (i, 0)),
            pl.BlockSpec((kc, d), lambda i: (0, 0), pipeline_mode=pl.Buffered(1)),
            pl.BlockSpec((kd, d), lambda i: (0, 0), pipeline_mode=pl.Buffered(1)),
        ],
        out_specs=pl.BlockSpec((tm, d), lambda i: (i, 0)),
        out_shape=jax.ShapeDtypeStruct((m, d), F32),
        compiler_params=_params(("parallel",)),
        name="odd_proj_residual",
    )(x, *parts, dd, wc, wd)


def _swa_kernel(sinks_ref, qa_ref, qb_ref, kp_ref, ko_ref, vp_ref, vo_ref, o_ref, bias_ref):
    j = pl.program_id(1)
    heads_per_kv = SWA_Q_HEADS // SWA_KV_HEADS
    half = LANES // 2

    @pl.when(j == 0)
    def _():
        for later in (0, 1):
            mask, distf = _band_mask(later, SWA_WINDOW - 1)
            for head in range(SWA_Q_HEADS):
                slope = 2.0 ** (-8.0 * (head + 1) / SWA_Q_HEADS) * LOG2E
                bias_ref[later, head] = jnp.where(mask, -slope * distf, -jnp.inf)

    chunks = ko_ref.shape[0] // BAND

    kw = jnp.concatenate([kp_ref[...], ko_ref[...]], axis=0).astype(F32)
    vw = jnp.concatenate([vp_ref[...], vo_ref[...]], axis=0).astype(F32)
    low = lax.broadcasted_iota(jnp.int32, kw.shape, 1) < half

    def halves(x):
        xr = pltpu.roll(x, half, 1)
        even = (jnp.where(low, x, 0.0).astype(BF16), jnp.where(low, xr, 0.0).astype(BF16))
        odd = (jnp.where(low, 0.0, xr).astype(BF16), jnp.where(low, 0.0, x).astype(BF16))
        return even, odd

    k_even, k_odd = halves(kw)
    v_even, v_odd = halves(vw)

    for blk in range(chunks):
        later = jnp.minimum(j, 1) if blk == 0 else 1
        rows = slice(blk * BAND, (blk + 1) * BAND)
        win = slice(blk * BAND, (blk + 2) * BAND)
        for kvh in range(SWA_KV_HEADS):
            heads = list(range(kvh * heads_per_kv, (kvh + 1) * heads_per_kv))
            ss = []
            q_ref = (qa_ref, qb_ref)[kvh]
            for head in heads:
                pair = (head - heads[0]) // 2
                slab = q_ref[rows, pair * LANES:(pair + 1) * LANES]
                kk = (k_even if head % 2 == 0 else k_odd)[kvh][win]
                ss.append(_qk(slab, kk) + bias_ref[later, head])
            stats = []
            for s in ss:
                m = jnp.max(s, axis=-1, keepdims=True)
                p = jnp.exp2(s - m)
                stats.append((m, jnp.sum(p, axis=-1, keepdims=True), p.astype(BF16)))
            outs = {}
            for head, (m, l, p) in zip(heads, stats):
                vv = (v_even if head % 2 == 0 else v_odd)[kvh][win]
                acc = jnp.dot(p, vv, preferred_element_type=F32)
                sk = sinks_ref[head] * LOG2E
                mx = jnp.maximum(m, sk)
                a = jnp.exp2(m - mx)
                outs[head] = acc * a / (l * a + jnp.exp2(sk - mx))
            for pair in range(heads[0] // 2, heads[-1] // 2 + 1):
                o_ref[rows, pair * LANES:(pair + 1) * LANES] = (outs[2 * pair] + outs[2 * pair + 1]).astype(o_ref.dtype)


def swa_attention(z, sinks, *, batch, seq, chunks):
    wq = SWA_Q_HEADS * SWA_HEAD_DIM
    wq_kv = wq // SWA_KV_HEADS
    q0 = 3 * N_DIL * DIL_HEADS_PER_GROUP * DIL_HEAD_DIM
    qo, ko, vo = q0 // wq_kv, (q0 + wq) // LANES, (q0 + wq) // LANES + 1

    tile = chunks * BAND

    def kvspec(off, prev):
        if prev:
            return pl.BlockSpec((None, BAND, LANES), lambda b, j: (b, jnp.maximum(j * chunks - 1, 0), off))
        return pl.BlockSpec((None, tile, LANES), lambda b, j: (b, j, off))

    return pl.pallas_call(
        _swa_kernel,
        grid=(batch, seq // tile),
        in_specs=[
            pl.BlockSpec(memory_space=pltpu.SMEM),
            pl.BlockSpec((None, tile, wq_kv), lambda b, j: (b, j, qo)),
            pl.BlockSpec((None, tile, wq_kv), lambda b, j: (b, j, qo + 1)),
            kvspec(ko, True), kvspec(ko, False), kvspec(vo, True), kvspec(vo, False),
        ],
        out_specs=pl.BlockSpec((None, tile, wq), lambda b, j: (b, j, 0)),
        out_shape=jax.ShapeDtypeStruct((batch, seq, wq), BF16),
        scratch_shapes=[pltpu.VMEM((2, SWA_Q_HEADS, BAND, 2 * BAND), F32)],
        compiler_params=_params(("parallel", "arbitrary")),
        name="swa_attention",
    )(sinks, z, z, z, z, z, z)


def _rope_pad(w):
    half = MLA_ROPE_DIM // 2
    z = jnp.zeros(w.shape[:-1] + (half,), w.dtype)
    return jnp.concatenate([w[..., :half], z, w[..., half:], z], axis=-1)


def _even_weights(w_in, w_uq, w_ukv, w_out):
    hd = MOBA_HEADS * MOBA_HEAD_DIM
    lat = MLA_Q_RANK + MLA_KV_RANK
    scale = jnp.concatenate([jnp.ones((lat + MLA_ROPE_DIM,), F32),
                             jnp.full((hd,), MOBA_HEAD_DIM ** -0.5 * LOG2E, F32), jnp.ones((2 * hd,), F32)])
    w_bf = (w_in * scale[None, :]).astype(BF16)
    w_in_p = jnp.concatenate([w_bf[:, :lat], w_bf[:, lat + MLA_ROPE_DIM:],
                              _rope_pad(w_bf[:, lat:lat + MLA_ROPE_DIM])], axis=-1)

    r = w_uq.shape[0]
    uq = w_uq.reshape(r, MLA_HEADS, MLA_NOPE_DIM + MLA_ROPE_DIM) * ((MLA_NOPE_DIM + MLA_ROPE_DIM) ** -0.5 * LOG2E)
    wq = jnp.concatenate([uq[..., :MLA_NOPE_DIM], _rope_pad(uq[..., MLA_NOPE_DIM:])], axis=-1)
    wq = wq.reshape(r, MLA_HEADS * MLA_QK_PAD).astype(BF16)
    ukv = w_ukv.reshape(w_ukv.shape[0], MLA_HEADS, MLA_NOPE_DIM + MLA_V_DIM)
    wk = ukv[..., :MLA_NOPE_DIM].reshape(-1, MLA_HEADS * MLA_NOPE_DIM).astype(BF16)
    wv = ukv[..., MLA_NOPE_DIM:].reshape(-1, MLA_HEADS * MLA_V_DIM).astype(BF16)
    na = MLA_HEADS * MLA_V_DIM
    return w_in_p, wq, wk, wv, w_out[:na].astype(BF16), w_out[na:].astype(BF16)


def _odd_weights(w_in, w_out):
    cd = N_DIL * DIL_HEADS_PER_GROUP * DIL_HEAD_DIM
    qd_w = SWA_Q_HEADS * SWA_HEAD_DIM
    kd_w = SWA_KV_HEADS * SWA_HEAD_DIM
    scale = jnp.concatenate([
        jnp.full((cd,), DIL_HEAD_DIM ** -0.5 * LOG2E, F32), jnp.ones((2 * cd,), F32),
        jnp.full((qd_w,), SWA_HEAD_DIM ** -0.5 * LOG2E, F32), jnp.ones((2 * kd_w,), F32)])
    w_in_p = (w_in * scale[None, :]).astype(BF16)
    nc = DIL_HEADS_PER_GROUP * DIL_HEAD_DIM
    return w_in_p, w_out[:nc].astype(BF16), w_out[nc:].astype(BF16)


def _rope_tables(seq):
    half = MLA_ROPE_DIM // 2
    inv_freq = ROPE_THETA ** (-jnp.arange(half, dtype=F32) / half)
    ang = jnp.arange(seq, dtype=jnp.int32).astype(F32)[:, None] * inv_freq[None, :]
    c, s = jnp.cos(ang), jnp.sin(ang)
    z = jnp.zeros_like(c)
    return jnp.concatenate([c, z, c, z], axis=-1), jnp.concatenate([-s, z, s, z], axis=-1)


def kernel(x, attn_norm, mlp_norm, w_up, w_down, ev_w_in, ev_q_norm, ev_w_uq, ev_kv_norm, ev_w_ukv, ev_w_out,
           od_w_in, od_sinks, od_w_out, final_norm):
    batch, seq, d = x.shape
    depth = attn_norm.shape[0]
    t = batch * seq
    xt = x.reshape(t, d)
    cos, sin = _rope_tables(seq)
    moba_slopes = jnp.exp2(-8.0 * jnp.arange(1, MOBA_HEADS + 1, dtype=F32) / MOBA_HEADS) * LOG2E
    n_dil_heads = N_DIL * DIL_HEADS_PER_GROUP
    dil_slopes = jnp.exp2(-8.0 * jnp.arange(1, n_dil_heads + 1, dtype=F32) / n_dil_heads) * LOG2E
    fg = final_norm.reshape(1, d)

    for layer in range(depth):
        i = layer // 2
        g = attn_norm[layer].reshape(1, d)
        if layer % 2 == 0:
            w_in_p, wq, wk, wv, wo_a, wo_b = _even_weights(ev_w_in[i], ev_w_uq[i], ev_w_ukv[i], ev_w_out[i])
            z = norm_matmul(xt, g, w_in_p, tm=PROJ_ROWS_EVEN)
            q, k, v = mla_up(z, ev_q_norm[i].reshape(1, -1), ev_kv_norm[i].reshape(1, -1), wq, wk, wv, cos, sin,
                             seq=seq, tm=MLA_UP_ROWS)
            a = mla_attention(q.reshape(batch, seq, -1), k.reshape(batch, seq, -1), v.reshape(batch, seq, -1),
                              batch=batch, seq=seq, blk=ATTN_BLOCK, heads=ATTN_HEADS_PER_STEP)
            b = moba_attention(z.reshape(batch, seq, EVEN_Z), moba_slopes, batch=batch, seq=seq,
                               heads=ATTN_HEADS_PER_STEP)
            xt = proj_residual(xt, a.reshape(t, -1), b.reshape(t, -1), wo_a, wo_b, tm=OUT_PROJ_ROWS)
        else:
            w_in_p, wo_c, wo_d = _odd_weights(od_w_in[i], od_w_out[i])
            z = norm_matmul(xt, g, w_in_p, tm=PROJ_ROWS_ODD)
            z3 = z.reshape(batch, seq, ODD_Z)
            parts = []
            for grp in range(N_DIL):
                parts.extend(dilated_group(z3, dil_slopes, group=grp, batch=batch, seq=seq,
                                           heads=DIL_GROUP_HEADS_PER_STEP[grp], chunks=DIL_GROUP_CHUNKS[grp]))
            dd = swa_attention(z3, od_sinks[i], batch=batch, seq=seq, chunks=SWA_CHUNKS)
            xt = odd_proj_residual(xt, parts, dd.reshape(t, -1), wo_c, wo_d, tm=OUT_PROJ_ROWS)
        xt = mlp_residual(xt, mlp_norm[layer].reshape(1, d), w_up[layer].astype(BF16), w_down[layer].astype(BF16), fg,
                          tm=MLP_ROWS, tf=MLP_FF_COLS, final_norm=(layer == depth - 1))
    return xt.reshape(batch, seq, d)
```

```python
import functools

import jax
import jax.numpy as jnp
from jax import lax
from jax.experimental import pallas as pl
from jax.experimental.pallas import tpu as pltpu

F32 = jnp.float32
BF16 = jnp.bfloat16

NORM_EPS = 1e-6

MLA_HEADS = 8
MLA_Q_RANK = 512
MLA_KV_RANK = 512
MLA_NOPE_DIM = 128
MLA_ROPE_DIM = 64
MLA_V_DIM = 128
ROPE_THETA = 10000.0

MOBA_HEADS = 8
MOBA_HEAD_DIM = 128
MOBA_BLOCK = 256
MOBA_TOPK = 3

DIL_WINDOWS = (128, 512, 2048)
DIL_RATES = (1, 4, 16)
N_DIL = 3
DIL_HEADS_PER_GROUP = 4
DIL_HEAD_DIM = 128

SWA_Q_HEADS = 16
SWA_KV_HEADS = 2
SWA_HEAD_DIM = 64
SWA_WINDOW = 128

BAND = 128
PROJ_ROWS_EVEN = 512
PROJ_ROWS_ODD = 256
MLA_UP_ROWS = 512
ATTN_BLOCK = 256
ATTN_HEADS_PER_STEP = 8
ATTN_KEY_BLOCKS = 4
OUT_PROJ_ROWS = 512
MLP_ROWS = 512
MLP_FF_COLS = 1024
SWA_CHUNKS = 8
DIL_GROUP_HEADS_PER_STEP = (4, 4, 2)
DIL_GROUP_CHUNKS = (8, 2, 1)

LANES = 128
MLA_QK_PAD = 256
EVEN_Z = 4224
ODD_Z = 5888
VMEM_LIMIT = 56 * 1024 * 1024
LOG2E = 1.4426950408889634


def _params(sem):
    return pltpu.CompilerParams(dimension_semantics=sem, vmem_limit_bytes=VMEM_LIMIT)


def _rms(x, g):
    ms = jnp.mean(x * x, axis=-1, keepdims=True)
    return x * lax.rsqrt(ms + NORM_EPS) * g


def _qk(q, k):
    return lax.dot_general(q, k, (((1,), (1,)), ((), ())), preferred_element_type=F32)


def _norm_matmul_kernel(x_ref, g_ref, w_ref, o_ref):
    h = _rms(x_ref[...], g_ref[...]).astype(BF16)
    o_ref[...] = jnp.dot(h, w_ref[...], preferred_element_type=F32).astype(o_ref.dtype)


def norm_matmul(x, g, w, *, tm):
    m, k = x.shape
    n = w.shape[1]
    return pl.pallas_call(
        _norm_matmul_kernel,
        grid=(m // tm,),
        in_specs=[
            pl.BlockSpec((tm, k), lambda i: (i, 0)),
            pl.BlockSpec((1, k), lambda i: (0, 0)),
            pl.BlockSpec((k, n), lambda i: (0, 0), pipeline_mode=pl.Buffered(1)),
        ],
        out_specs=pl.BlockSpec((tm, n), lambda i: (i, 0)),
        out_shape=jax.ShapeDtypeStruct((m, n), BF16),
        compiler_params=_params(("parallel",)),
        name="norm_matmul",
    )(x, g, w)


def _mlp_kernel(x_ref, g_ref, wu_ref, wd_ref, fg_ref, o_ref, h_ref, acc_ref, *, final_norm):
    f = pl.program_id(1)

    @pl.when(f == 0)
    def _():
        h_ref[...] = _rms(x_ref[...], g_ref[...]).astype(BF16)
        acc_ref[...] = jnp.zeros_like(acc_ref)

    u = jnp.dot(h_ref[...], wu_ref[...], preferred_element_type=F32)
    u = jnp.maximum(u, 0.0)
    u = u * u
    acc_ref[...] += jnp.dot(u.astype(BF16), wd_ref[...], preferred_element_type=F32)

    @pl.when(f == pl.num_programs(1) - 1)
    def _():
        y = x_ref[...] + acc_ref[...]
        if final_norm:
            y = _rms(y, fg_ref[...])
        o_ref[...] = y


def mlp_residual(x, g, w_up, w_down, final_g, *, tm, tf, final_norm):
    m, d = x.shape
    ff = w_up.shape[1]
    return pl.pallas_call(
        functools.partial(_mlp_kernel, final_norm=final_norm),
        grid=(m // tm, ff // tf),
        in_specs=[
            pl.BlockSpec((tm, d), lambda i, f: (i, 0)),
            pl.BlockSpec((1, d), lambda i, f: (0, 0)),
            pl.BlockSpec((d, tf), lambda i, f: (0, f)),
            pl.BlockSpec((tf, d), lambda i, f: (f, 0)),
            pl.BlockSpec((1, d), lambda i, f: (0, 0)),
        ],
        out_specs=pl.BlockSpec((tm, d), lambda i, f: (i, 0)),
        out_shape=jax.ShapeDtypeStruct((m, d), F32),
        scratch_shapes=[pltpu.VMEM((tm, d), BF16), pltpu.VMEM((tm, d), F32)],
        compiler_params=_params(("parallel", "arbitrary")),
        name="mlp_residual",
    )(x, g, w_up, w_down, final_g)


def _proj_residual_kernel(x_ref, a_ref, b_ref, wa_ref, wb_ref, o_ref):
    y = jnp.dot(a_ref[...], wa_ref[...], preferred_element_type=F32)
    y = y + jnp.dot(b_ref[...], wb_ref[...], preferred_element_type=F32)
    o_ref[...] = x_ref[...] + y


def proj_residual(x, a, b, wa, wb, *, tm):
    m, d = x.shape
    ka, kb = a.shape[1], b.shape[1]
    return pl.pallas_call(
        _proj_residual_kernel,
        grid=(m // tm,),
        in_specs=[
            pl.BlockSpec((tm, d), lambda i: (i, 0)),
            pl.BlockSpec((tm, ka), lambda i: (i, 0)),
            pl.BlockSpec((tm, kb), lambda i: (i, 0)),
            pl.BlockSpec((ka, d), lambda i: (0, 0)),
            pl.BlockSpec((kb, d), lambda i: (0, 0)),
        ],
        out_specs=pl.BlockSpec((tm, d), lambda i: (i, 0)),
        out_shape=jax.ShapeDtypeStruct((m, d), F32),
        compiler_params=_params(("parallel",)),
        name="proj_residual",
    )(x, a, b, wa, wb)


def _mla_up_kernel(ql_ref, kvl_ref, kpe_ref, qn_ref, kvn_ref, wq_ref, wk_ref, wv_ref, cos_ref, sin_ref,
                   q_ref, k_ref, v_ref):
    cos = cos_ref[...]
    sin = sin_ref[...]

    def rope(t):
        return t * cos + pltpu.roll(t, LANES // 2, 1) * sin

    hq = _rms(ql_ref[...].astype(F32), qn_ref[...]).astype(BF16)
    yq = jnp.dot(hq, wq_ref[...], preferred_element_type=F32)
    hkv = _rms(kvl_ref[...].astype(F32), kvn_ref[...]).astype(BF16)
    yk = jnp.dot(hkv, wk_ref[...], preferred_element_type=F32)
    yv = jnp.dot(hkv, wv_ref[...], preferred_element_type=F32)
    krot = rope(kpe_ref[...].astype(F32)).astype(BF16)
    for h in range(MLA_HEADS):
        lo = h * MLA_QK_PAD
        mid = lo + MLA_NOPE_DIM
        hi = lo + MLA_QK_PAD
        q_ref[:, lo:mid] = yq[:, lo:mid].astype(BF16)
        q_ref[:, mid:hi] = rope(yq[:, mid:hi]).astype(BF16)
        k_ref[:, lo:mid] = yk[:, h * MLA_NOPE_DIM:(h + 1) * MLA_NOPE_DIM].astype(BF16)
        k_ref[:, mid:hi] = krot
    v_ref[...] = yv.astype(BF16)


def mla_up(z, qn, kvn, wq, wk, wv, cos, sin, *, seq, tm):
    t = z.shape[0]
    r = MLA_Q_RANK
    nq = MLA_HEADS * MLA_QK_PAD
    nv = MLA_HEADS * MLA_V_DIM
    kpe_blk = (EVEN_Z - LANES) // LANES
    pos_blocks = seq // tm
    return pl.pallas_call(
        _mla_up_kernel,
        grid=(t // tm,),
        in_specs=[
            pl.BlockSpec((tm, r), lambda i: (i, 0)),
            pl.BlockSpec((tm, r), lambda i: (i, 1)),
            pl.BlockSpec((tm, LANES), lambda i: (i, kpe_blk)),
            pl.BlockSpec((1, r), lambda i: (0, 0)),
            pl.BlockSpec((1, r), lambda i: (0, 0)),
            pl.BlockSpec((r, nq), lambda i: (0, 0)),
            pl.BlockSpec((r, nv), lambda i: (0, 0)),
            pl.BlockSpec((r, nv), lambda i: (0, 0)),
            pl.BlockSpec((tm, LANES), lambda i: (i % pos_blocks, 0)),
            pl.BlockSpec((tm, LANES), lambda i: (i % pos_blocks, 0)),
        ],
        out_specs=[
            pl.BlockSpec((tm, nq), lambda i: (i, 0)),
            pl.BlockSpec((tm, nq), lambda i: (i, 0)),
            pl.BlockSpec((tm, nv), lambda i: (i, 0)),
        ],
        out_shape=[
            jax.ShapeDtypeStruct((t, nq), BF16),
            jax.ShapeDtypeStruct((t, nq), BF16),
            jax.ShapeDtypeStruct((t, nv), BF16),
        ],
        compiler_params=_params(("parallel",)),
        name="mla_up",
    )(z, z, z, qn, kvn, wq, wk, wv, cos, sin)


def _softmax_steps_t(sts, vts, carries):
    stats = []
    for st, (m, l, _) in zip(sts, carries):
        m_new = jnp.maximum(m, jnp.max(st, axis=0, keepdims=True))
        p = jnp.exp2(st - m_new)
        alpha = jnp.exp2(m - m_new)
        stats.append((m_new, alpha * l + jnp.sum(p, axis=0, keepdims=True), alpha, p.astype(BF16)))
    return tuple((m_new, l_new, alpha * acc + jnp.dot(vt, p, preferred_element_type=F32))
                 for (m_new, l_new, alpha, p), vt, (_, _, acc) in zip(stats, vts, carries))


def _softmax_init(dv, nq):
    return (jnp.full((1, nq), -jnp.inf, F32), jnp.zeros((1, nq), F32), jnp.zeros((dv, nq), F32))


def _transpose_bf16(x):
    return x.astype(F32).T.astype(BF16)


def _mla_attn_kernel(q_ref, k_ref, v_ref, o_ref, vt_ref, *, blk, heads):
    qi = pl.program_id(2)
    dq, dv = MLA_QK_PAD, MLA_V_DIM

    @pl.when(qi == 0)
    def _():
        def fill(n, c):
            start = pl.multiple_of(n * blk, blk)
            for g in range(heads):
                vt_ref[g * dv:(g + 1) * dv, pl.ds(start, blk)] = _transpose_bf16(
                    v_ref[pl.ds(start, blk), g * dv:(g + 1) * dv])
            return c

        lax.fori_loop(0, k_ref.shape[0] // blk, fill, 0)

    qts = [_transpose_bf16(q_ref[:, g * dq:(g + 1) * dq]) for g in range(heads)]
    tk = ATTN_KEY_BLOCKS * blk

    def tile(start, nk, carries, mask):
        sts = [jnp.dot(k_ref[pl.ds(start, nk), g * dq:(g + 1) * dq], qts[g], preferred_element_type=F32)
               for g in range(heads)]
        if mask is not None:
            sts = [jnp.where(mask, st, -jnp.inf) for st in sts]
        vts = [vt_ref[g * dv:(g + 1) * dv, pl.ds(start, nk)] for g in range(heads)]
        return _softmax_steps_t(sts, vts, carries)

    def finish(carries):
        for g in range(heads):
            _, l, acc = carries[g]
            o_ref[:, g * dv:(g + 1) * dv] = (acc / l).T.astype(o_ref.dtype)

    def causal(nk):
        key = lax.broadcasted_iota(jnp.int32, (nk, blk), 0)
        qry = lax.broadcasted_iota(jnp.int32, (nk, blk), 1)
        return key <= qry + (nk - blk)

    init = tuple(_softmax_init(dv, blk) for _ in range(heads))
    carries = lax.fori_loop(0, qi // ATTN_KEY_BLOCKS,
                            lambda n, c: tile(pl.multiple_of(n * tk, tk), tk, c, None), init)

    for rem in range(ATTN_KEY_BLOCKS):
        @pl.when(qi % ATTN_KEY_BLOCKS == rem)
        def _(rem=rem):
            nk = (rem + 1) * blk
            finish(tile(pl.multiple_of((qi - rem) * blk, blk), nk, carries, causal(nk)))


def mla_attention(q, k, v, *, batch, seq, blk, heads):
    dq = MLA_QK_PAD * heads
    dv = MLA_V_DIM * heads
    return pl.pallas_call(
        functools.partial(_mla_attn_kernel, blk=blk, heads=heads),
        grid=(batch, MLA_HEADS // heads, seq // blk),
        in_specs=[
            pl.BlockSpec((None, blk, dq), lambda b, h, i: (b, i, h)),
            pl.BlockSpec((None, seq, dq), lambda b, h, i: (b, 0, h), pipeline_mode=pl.Buffered(1)),
            pl.BlockSpec((None, seq, dv), lambda b, h, i: (b, 0, h)),
        ],
        out_specs=pl.BlockSpec((None, blk, dv), lambda b, h, i: (b, i, h)),
        out_shape=jax.ShapeDtypeStruct((batch, seq, MLA_HEADS * MLA_V_DIM), BF16),
        scratch_shapes=[pltpu.VMEM((dv, seq), BF16)],
        compiler_params=_params(("parallel", "parallel", "arbitrary")),
        name="mla_attention",
    )(q, k, v)


MOBA_NB_PAD = 16
MOBA_MASKED = -1e30


def _moba_kernel(slopes_ref, q_ref, k_ref, v_ref, o_ref, vt_ref, kmean_ref, sel_ref, *, nb, heads):
    blk = MOBA_BLOCK
    dh = MOBA_HEAD_DIM
    nbp = kmean_ref.shape[1]
    hg = pl.program_id(1)
    i = pl.program_id(2)

    @pl.when(i == 0)
    def _():
        kmean_ref[...] = jnp.zeros_like(kmean_ref)

        def fill(n, c):
            start = pl.multiple_of(n * blk, blk)
            for g in range(heads):
                kn = k_ref[pl.ds(start, blk), g * dh:(g + 1) * dh].astype(F32)
                kmean_ref[g, pl.ds(n, 1), :] = jnp.mean(kn, axis=0, keepdims=True)
                vt_ref[g * dh:(g + 1) * dh, pl.ds(start, blk)] = _transpose_bf16(
                    v_ref[pl.ds(start, blk), g * dh:(g + 1) * dh])
            return c

        lax.fori_loop(0, nb, fill, 0)

    blk_id = lax.broadcasted_iota(jnp.int32, (nbp, blk), 0)
    past = blk_id < i
    tk = ATTN_KEY_BLOCKS * blk
    keyf = lax.broadcasted_iota(jnp.int32, (tk, blk), 0).astype(F32)

    qts, colbias, slopes = [], [], []
    for g in range(heads):
        qt = _transpose_bf16(q_ref[:, g * dh:(g + 1) * dh])
        km = kmean_ref[g]
        km_hi = km.astype(BF16)
        km_lo = (km - km_hi.astype(F32)).astype(BF16)
        gate = (jnp.dot(km_hi, qt, preferred_element_type=F32)
                + jnp.dot(km_lo, qt, preferred_element_type=F32))
        gm = jnp.where(past, gate, -jnp.inf)
        rank = jnp.zeros((nbp, blk), jnp.int32)
        for n in range(nb):
            gn = gm[n:n + 1, :]
            beats = (gn > gm) | ((gn == gm) & (blk_id > n))
            rank = rank + beats.astype(jnp.int32)
        sel_ref[g] = jnp.where(past & (rank < min(MOBA_TOPK, nb)), 0.0, MOBA_MASKED)
        slope = slopes_ref[hg * heads + g]
        qts.append(qt)
        slopes.append(slope)
        colbias.append(keyf * slope)

    def tile(b0, carries, gates, causal):
        nblk = len(gates[0])
        nk = nblk * blk
        start = pl.multiple_of(b0 * blk, blk)
        sts = [jnp.dot(k_ref[pl.ds(start, nk), g * dh:(g + 1) * dh], qts[g], preferred_element_type=F32)
               for g in range(heads)]
        if nk == tk:
            cb = colbias
        else:
            keyf_nk = lax.broadcasted_iota(jnp.int32, (nk, blk), 0).astype(F32)
            cb = [keyf_nk * slopes[g] for g in range(heads)]
        biased = []
        for g in range(heads):
            shift = jnp.full((1, blk), (b0 - i) * blk, jnp.int32).astype(F32) * slopes[g]
            rows = [shift if gate is None else gate + shift for gate in gates[g]]
            st = sts[g] + cb[g]
            if nblk == 1:
                st = st + rows[0]
            else:
                st = jnp.concatenate([st[c * blk:(c + 1) * blk] + rows[c] for c in range(nblk)], axis=0)
            if causal:
                key_nk = lax.broadcasted_iota(jnp.int32, (nk, blk), 0)
                qry_nk = lax.broadcasted_iota(jnp.int32, (nk, blk), 1)
                st = jnp.where(key_nk <= qry_nk + (nk - blk), st, -jnp.inf)
            biased.append(st)
        vts = [vt_ref[g * dh:(g + 1) * dh, pl.ds(start, nk)] for g in range(heads)]
        return _softmax_steps_t(biased, vts, carries)

    def gate_rows(b0, nblk):
        return [tuple(sel_ref[g, pl.ds(b0 + c, 1), :] for c in range(nblk)) for g in range(heads)]

    def body(n, carries):
        return tile(ATTN_KEY_BLOCKS * n, carries, gate_rows(ATTN_KEY_BLOCKS * n, ATTN_KEY_BLOCKS), False)

    def finish(carries):
        for g in range(heads):
            _, l, acc = carries[g]
            o_ref[:, g * dh:(g + 1) * dh] = (acc / l).T.astype(o_ref.dtype)

    init = tuple(_softmax_init(dh, blk) for _ in range(heads))
    carries = lax.fori_loop(0, i // ATTN_KEY_BLOCKS, body, init)

    for rem in range(ATTN_KEY_BLOCKS):
        @pl.when(i % ATTN_KEY_BLOCKS == rem)
        def _(rem=rem):
            gates = [rows + (None,) for rows in gate_rows(i - rem, rem)]
            finish(tile(i - rem, carries, gates, True))


def moba_attention(z, slopes, *, batch, seq, heads):
    blk = MOBA_BLOCK
    w = MOBA_HEAD_DIM * heads
    nb = seq // blk
    qo, ko, vo = 1024 // w, 2048 // w, 3072 // w
    return pl.pallas_call(
        functools.partial(_moba_kernel, nb=nb, heads=heads),
        grid=(batch, MOBA_HEADS // heads, nb),
        in_specs=[
            pl.BlockSpec(memory_space=pltpu.SMEM),
            pl.BlockSpec((None, blk, w), lambda b, h, i: (b, i, qo + h)),
            pl.BlockSpec((None, seq, w), lambda b, h, i: (b, 0, ko + h), pipeline_mode=pl.Buffered(1)),
            pl.BlockSpec((None, seq, w), lambda b, h, i: (b, 0, vo + h)),
        ],
        out_specs=pl.BlockSpec((None, blk, w), lambda b, h, i: (b, i, h)),
        out_shape=jax.ShapeDtypeStruct((batch, seq, MOBA_HEADS * MOBA_HEAD_DIM), BF16),
        scratch_shapes=[
            pltpu.VMEM((w, seq), BF16),
            pltpu.VMEM((heads, max(MOBA_NB_PAD, nb), MOBA_HEAD_DIM), F32),
            pltpu.VMEM((heads, max(MOBA_NB_PAD, nb), blk), F32),
        ],
        compiler_params=_params(("parallel", "parallel", "arbitrary")),
        name="moba_attention",
    )(slopes, z, z, z)


def _band_mask(j, max_dist):
    row = lax.broadcasted_iota(jnp.int32, (BAND, 2 * BAND), 0)
    col = lax.broadcasted_iota(jnp.int32, (BAND, 2 * BAND), 1)
    dist = BAND + row - col
    mask = (dist >= 0) & (dist <= max_dist) & ((col >= BAND) | (j > 0))
    return mask, dist.astype(F32)


def _dil_kernel(slopes_ref, q_ref, k_ref, v_ref, o_ref, lse_ref, stage_ref, qd_ref, kd_ref, vd_ref, *, group):
    hb = pl.program_id(1)
    j = pl.program_id(2)
    rate = DIL_RATES[group]
    dh = DIL_HEAD_DIM
    nh = q_ref.shape[-1] // dh
    sup = BAND * rate
    chunks = q_ref.shape[0] // sup
    per_iter = DIL_HEADS_PER_GROUP // nh if rate > 1 else 1

    @pl.when(j == 0)
    def _():
        kd_ref[0] = jnp.zeros(kd_ref.shape[1:], BF16)
        vd_ref[0] = jnp.zeros(vd_ref.shape[1:], BF16)

    def split(src_ref, store):
        if rate == 1:
            for c in range(chunks):
                store(c, 0, src_ref[c * BAND:(c + 1) * BAND, :])
            return
        for hh in range(nh):
            stage_ref[0, hh] = src_ref[:, hh * dh:(hh + 1) * dh].astype(F32)
        for c in range(chunks):
            for r in range(rate):
                rows = pl.ds(c * sup + r, BAND, stride=rate)
                store(c, r, jnp.concatenate([stage_ref[0, hh, rows, :].astype(BF16) for hh in range(nh)], axis=1))

    def store_q(c, r, x):
        qd_ref[c, r] = x

    def store_k(c, r, x):
        kd_ref[1 + c, r] = x

    def store_v(c, r, x):
        vd_ref[1 + c, r] = x

    split(q_ref, store_q)
    split(k_ref, store_k)
    split(v_ref, store_v)

    def biases(first_flag):
        mask, distf = _band_mask(first_flag, DIL_WINDOWS[group] // rate)
        out = []
        for hh in range(nh):
            slope = slopes_ref[group * DIL_HEADS_PER_GROUP + hb * nh + hh] * rate
            out.append(jnp.where(mask, -slope * distf, -jnp.inf))
        return out

    sls = [slice(hh * dh, (hh + 1) * dh) for hh in range(nh)]
    for c in range(chunks):
        bias = biases(j if c == 0 else 1)

        def sub(it, carry, c=c, bias=bias):
            chains = []
            for u in range(per_iter):
                r = it * per_iter + u
                q = qd_ref[c, r]
                kw = jnp.concatenate([kd_ref[c, r], kd_ref[c + 1, r]], axis=0)
                vw = jnp.concatenate([vd_ref[c, r], vd_ref[c + 1, r]], axis=0)
                chains.extend((r, hh, _qk(q[:, sl], kw[:, sl]) + bias[hh], vw[:, sl]) for hh, sl in enumerate(sls))
            stats = []
            for _, _, s, _ in chains:
                m = jnp.max(s, axis=-1, keepdims=True)
                p = jnp.exp2(s - m)
                stats.append((m, jnp.sum(p, axis=-1, keepdims=True), p.astype(BF16)))
            for (r, hh, _, vwh), (m, l, p) in zip(chains, stats):
                out = jnp.dot(p, vwh, preferred_element_type=F32) / l
                lse = jnp.broadcast_to(m + jnp.log2(l), (BAND, dh))
                if rate == 1:
                    o_ref[c * BAND:(c + 1) * BAND, sls[hh]] = out
                    lse_ref[c * BAND:(c + 1) * BAND, sls[hh]] = lse
                else:
                    rows = pl.ds(c * sup + r, BAND, stride=rate)
                    stage_ref[0, hh, rows, :] = out
                    stage_ref[1, hh, rows, :] = lse
            return carry

        if rate == 1:
            sub(0, 0)
        else:
            lax.fori_loop(0, rate // per_iter, sub, 0)

    if rate > 1:
        for hh in range(nh):
            o_ref[:, sls[hh]] = stage_ref[0, hh]
            lse_ref[:, sls[hh]] = stage_ref[1, hh]
    kd_ref[0] = kd_ref[chunks]
    vd_ref[0] = vd_ref[chunks]


def dilated_group(z, slopes, *, group, batch, seq, heads, chunks):
    rate = DIL_RATES[group]
    tile = BAND * rate * chunks
    gw = DIL_HEADS_PER_GROUP * DIL_HEAD_DIM
    w = heads * DIL_HEAD_DIM
    nhb = DIL_HEADS_PER_GROUP // heads
    qo, ko, vo = (group * gw // w, (N_DIL + group) * gw // w, (2 * N_DIL + group) * gw // w)
    stage_rows = tile if rate > 1 else 8

    def spec(off):
        return pl.BlockSpec((None, tile, w), lambda b, h, j: (b, j, off + h))

    out_spec = pl.BlockSpec((None, tile, w), lambda b, h, j: (b, j, h))
    o, lse = pl.pallas_call(
        functools.partial(_dil_kernel, group=group),
        grid=(batch, nhb, seq // tile),
        in_specs=[pl.BlockSpec(memory_space=pltpu.SMEM), spec(qo), spec(ko), spec(vo)],
        out_specs=[out_spec, out_spec],
        out_shape=[jax.ShapeDtypeStruct((batch, seq, gw), F32)] * 2,
        scratch_shapes=[
            pltpu.VMEM((2, heads, stage_rows, DIL_HEAD_DIM), F32),
            pltpu.VMEM((chunks, rate, BAND, w), BF16),
            pltpu.VMEM((chunks + 1, rate, BAND, w), BF16),
            pltpu.VMEM((chunks + 1, rate, BAND, w), BF16),
        ],
        compiler_params=_params(("parallel", "parallel", "arbitrary")),
        name=f"dilated_group{group}",
    )(slopes, z, z, z)
    return o.reshape(batch * seq, gw), lse.reshape(batch * seq, gw)


def _odd_proj_kernel(x_ref, o0_ref, l0_ref, o1_ref, l1_ref, o2_ref, l2_ref, d_ref, wc_ref, wd_ref, out_ref):
    l0, l1, l2 = l0_ref[...], l1_ref[...], l2_ref[...]
    mx = jnp.maximum(jnp.maximum(l0, l1), l2)
    w0, w1, w2 = jnp.exp2(l0 - mx), jnp.exp2(l1 - mx), jnp.exp2(l2 - mx)
    num = o0_ref[...] * w0 + o1_ref[...] * w1 + o2_ref[...] * w2
    c = (num / (w0 + w1 + w2)).astype(BF16)
    y = jnp.dot(c, wc_ref[...], preferred_element_type=F32)
    y = y + jnp.dot(d_ref[...], wd_ref[...], preferred_element_type=F32)
    out_ref[...] = x_ref[...] + y


def odd_proj_residual(x, parts, dd, wc, wd, *, tm):
    m, d = x.shape
    kc, kd = wc.shape[0], wd.shape[0]
    part_spec = pl.BlockSpec((tm, kc), lambda i: (i, 0))
    return pl.pallas_call(
        _odd_proj_kernel,
        grid=(m // tm,),
        in_specs=[pl.BlockSpec((tm, d), lambda i: (i, 0))] + [part_spec] * 6 + [
            pl.BlockSpec((tm, kd), lambda i: (i, 0)),
            pl.BlockSpec((kc, d), lambda i: (0, 0), pipeline_mode=pl.Buffered(1)),
            pl.BlockSpec((kd, d), lambda i: (0, 0), pipeline_mode=pl.Buffered(1)),
        ],
        out_specs=pl.BlockSpec((tm, d), lambda i: (i, 0)),
        out_shape=jax.ShapeDtypeStruct((m, d), F32),
        compiler_params=_params(("parallel",)),
        name="odd_proj_residual",
    )(x, *parts, dd, wc, wd)


def _swa_kernel(sinks_ref, qa_ref, qb_ref, kp_ref, ko_ref, vp_ref, vo_ref, o_ref, bias_ref):
    j = pl.program_id(1)
    heads_per_kv = SWA_Q_HEADS // SWA_KV_HEADS
    half = LANES // 2

    @pl.when(j == 0)
    def _():
        for later in (0, 1):
            mask, distf = _band_mask(later, SWA_WINDOW - 1)
            for head in range(SWA_Q_HEADS):
                slope = 2.0 ** (-8.0 * (head + 1) / SWA_Q_HEADS) * LOG2E
                bias_ref[later, head] = jnp.where(mask, -slope * distf, -jnp.inf)

    chunks = ko_ref.shape[0] // BAND

    kw = jnp.concatenate([kp_ref[...], ko_ref[...]], axis=0).astype(F32)
    vw = jnp.concatenate([vp_ref[...], vo_ref[...]], axis=0).astype(F32)
    low = lax.broadcasted_iota(jnp.int32, kw.shape, 1) < half

    def halves(x):
        xr = pltpu.roll(x, half, 1)
        even = (jnp.where(low, x, 0.0).astype(BF16), jnp.where(low, xr, 0.0).astype(BF16))
        odd = (jnp.where(low, 0.0, xr).astype(BF16), jnp.where(low, 0.0, x).astype(BF16))
        return even, odd

    k_even, k_odd = halves(kw)
    v_even, v_odd = halves(vw)

    for blk in range(chunks):
        later = jnp.minimum(j, 1) if blk == 0 else 1
        rows = slice(blk * BAND, (blk + 1) * BAND)
        win = slice(blk * BAND, (blk + 2) * BAND)
        for kvh in range(SWA_KV_HEADS):
            heads = list(range(kvh * heads_per_kv, (kvh + 1) * heads_per_kv))
            ss = []
            q_ref = (qa_ref, qb_ref)[kvh]
            for head in heads:
                pair = (head - heads[0]) // 2
                slab = q_ref[rows, pair * LANES:(pair + 1) * LANES]
                kk = (k_even if head % 2 == 0 else k_odd)[kvh][win]
                ss.append(_qk(slab, kk) + bias_ref[later, head])
            stats = []
            for s in ss:
                m = jnp.max(s, axis=-1, keepdims=True)
                p = jnp.exp2(s - m)
                stats.append((m, jnp.sum(p, axis=-1, keepdims=True), p.astype(BF16)))
            outs = {}
            for head, (m, l, p) in zip(heads, stats):
                vv = (v_even if head % 2 == 0 else v_odd)[kvh][win]
                acc = jnp.dot(p, vv, preferred_element_type=F32)
                sk = sinks_ref[head] * LOG2E
                mx = jnp.maximum(m, sk)
                a = jnp.exp2(m - mx)
                outs[head] = acc * a / (l * a + jnp.exp2(sk - mx))
            for pair in range(heads[0] // 2, heads[-1] // 2 + 1):
                o_ref[rows, pair * LANES:(pair + 1) * LANES] = (outs[2 * pair] + outs[2 * pair + 1]).astype(o_ref.dtype)


def swa_attention(z, sinks, *, batch, seq, chunks):
    wq = SWA_Q_HEADS * SWA_HEAD_DIM
    wq_kv = wq // SWA_KV_HEADS
    q0 = 3 * N_DIL * DIL_HEADS_PER_GROUP * DIL_HEAD_DIM
    qo, ko, vo = q0 // wq_kv, (q0 + wq) // LANES, (q0 + wq) // LANES + 1

    tile = chunks * BAND

    def kvspec(off, prev):
        if prev:
            return pl.BlockSpec((None, BAND, LANES), lambda b, j: (b, jnp.maximum(j * chunks - 1, 0), off))
        return pl.BlockSpec((None, tile, LANES), lambda b, j: (b, j, off))

    return pl.pallas_call(
        _swa_kernel,
        grid=(batch, seq // tile),
        in_specs=[
            pl.BlockSpec(memory_space=pltpu.SMEM),
            pl.BlockSpec((None, tile, wq_kv), lambda b, j: (b, j, qo)),
            pl.BlockSpec((None, tile, wq_kv), lambda b, j: (b, j, qo + 1)),
            kvspec(ko, True), kvspec(ko, False), kvspec(vo, True), kvspec(vo, False),
        ],
        out_specs=pl.BlockSpec((None, tile, wq), lambda b, j: (b, j, 0)),
        out_shape=jax.ShapeDtypeStruct((batch, seq, wq), BF16),
        scratch_shapes=[pltpu.VMEM((2, SWA_Q_HEADS, BAND, 2 * BAND), F32)],
        compiler_params=_params(("parallel", "arbitrary")),
        name="swa_attention",
    )(sinks, z, z, z, z, z, z)


def _rope_pad(w):
    half = MLA_ROPE_DIM // 2
    z = jnp.zeros(w.shape[:-1] + (half,), w.dtype)
    return jnp.concatenate([w[..., :half], z, w[..., half:], z], axis=-1)


def _even_weights(w_in, w_uq, w_ukv, w_out):
    hd = MOBA_HEADS * MOBA_HEAD_DIM
    lat = MLA_Q_RANK + MLA_KV_RANK
    scale = jnp.concatenate([jnp.ones((lat + MLA_ROPE_DIM,), F32),
                             jnp.full((hd,), MOBA_HEAD_DIM ** -0.5 * LOG2E, F32), jnp.ones((2 * hd,), F32)])
    w_bf = (w_in * scale[None, :]).astype(BF16)
    w_in_p = jnp.concatenate([w_bf[:, :lat], w_bf[:, lat + MLA_ROPE_DIM:],
                              _rope_pad(w_bf[:, lat:lat + MLA_ROPE_DIM])], axis=-1)

    r = w_uq.shape[0]
    uq = w_uq.reshape(r, MLA_HEADS, MLA_NOPE_DIM + MLA_ROPE_DIM) * ((MLA_NOPE_DIM + MLA_ROPE_DIM) ** -0.5 * LOG2E)
    wq = jnp.concatenate([uq[..., :MLA_NOPE_DIM], _rope_pad(uq[..., MLA_NOPE_DIM:])], axis=-1)
    wq = wq.reshape(r, MLA_HEADS * MLA_QK_PAD).astype(BF16)
    ukv = w_ukv.reshape(w_ukv.shape[0], MLA_HEADS, MLA_NOPE_DIM + MLA_V_DIM)
    wk = ukv[..., :MLA_NOPE_DIM].reshape(-1, MLA_HEADS * MLA_NOPE_DIM).astype(BF16)
    wv = ukv[..., MLA_NOPE_DIM:].reshape(-1, MLA_HEADS * MLA_V_DIM).astype(BF16)
    na = MLA_HEADS * MLA_V_DIM
    return w_in_p, wq, wk, wv, w_out[:na].astype(BF16), w_out[na:].astype(BF16)


def _odd_weights(w_in, w_out):
    cd = N_DIL * DIL_HEADS_PER_GROUP * DIL_HEAD_DIM
    qd_w = SWA_Q_HEADS * SWA_HEAD_DIM
    kd_w = SWA_KV_HEADS * SWA_HEAD_DIM
    scale = jnp.concatenate([
        jnp.full((cd,), DIL_HEAD_DIM ** -0.5 * LOG2E, F32), jnp.ones((2 * cd,), F32),
        jnp.full((qd_w,), SWA_HEAD_DIM ** -0.5 * LOG2E, F32), jnp.ones((2 * kd_w,), F32)])
    w_in_p = (w_in * scale[None, :]).astype(BF16)
    nc = DIL_HEADS_PER_GROUP * DIL_HEAD_DIM
    return w_in_p, w_out[:nc].astype(BF16), w_out[nc:].astype(BF16)


def _rope_tables(seq):
    half = MLA_ROPE_DIM // 2
    inv_freq = ROPE_THETA ** (-jnp.arange(half, dtype=F32) / half)
    ang = jnp.arange(seq, dtype=jnp.int32).astype(F32)[:, None] * inv_freq[None, :]
    c, s = jnp.cos(ang), jnp.sin(ang)
    z = jnp.zeros_like(c)
    return jnp.concatenate([c, z, c, z], axis=-1), jnp.concatenate([-s, z, s, z], axis=-1)


def kernel(x, attn_norm, mlp_norm, w_up, w_down, ev_w_in, ev_q_norm, ev_w_uq, ev_kv_norm, ev_w_ukv, ev_w_out,
           od_w_in, od_sinks, od_w_out, final_norm):
    batch, seq, d = x.shape
    depth = attn_norm.shape[0]
    t = batch * seq
    xt = x.reshape(t, d)
    cos, sin = _rope_tables(seq)
    moba_slopes = jnp.exp2(-8.0 * jnp.arange(1, MOBA_HEADS + 1, dtype=F32) / MOBA_HEADS) * LOG2E
    n_dil_heads = N_DIL * DIL_HEADS_PER_GROUP
    dil_slopes = jnp.exp2(-8.0 * jnp.arange(1, n_dil_heads + 1, dtype=F32) / n_dil_heads) * LOG2E
    fg = final_norm.reshape(1, d)

    for layer in range(depth):
        i = layer // 2
        g = attn_norm[layer].reshape(1, d)
        if layer % 2 == 0:
            w_in_p, wq, wk, wv, wo_a, wo_b = _even_weights(ev_w_in[i], ev_w_uq[i], ev_w_ukv[i], ev_w_out[i])
            z = norm_matmul(xt, g, w_in_p, tm=PROJ_ROWS_EVEN)
            q, k, v = mla_up(z, ev_q_norm[i].reshape(1, -1), ev_kv_norm[i].reshape(1, -1), wq, wk, wv, cos, sin,
                             seq=seq, tm=MLA_UP_ROWS)
            a = mla_attention(q.reshape(batch, seq, -1), k.reshape(batch, seq, -1), v.reshape(batch, seq, -1),
                              batch=batch, seq=seq, blk=ATTN_BLOCK, heads=ATTN_HEADS_PER_STEP)
            b = moba_attention(z.reshape(batch, seq, EVEN_Z), moba_slopes, batch=batch, seq=seq,
                               heads=ATTN_HEADS_PER_STEP)
            xt = proj_residual(xt, a.reshape(t, -1), b.reshape(t, -1), wo_a, wo_b, tm=OUT_PROJ_ROWS)
        else:
            w_in_p, wo_c, wo_d = _odd_weights(od_w_in[i], od_w_out[i])
            z = norm_matmul(xt, g, w_in_p, tm=PROJ_ROWS_ODD)
            z3 = z.reshape(batch, seq, ODD_Z)
            parts = []
            for grp in range(N_DIL):
                parts.extend(dilated_group(z3, dil_slopes, group=grp, batch=batch, seq=seq,
                                           heads=DIL_GROUP_HEADS_PER_STEP[grp], chunks=DIL_GROUP_CHUNKS[grp]))
            dd = swa_attention(z3, od_sinks[i], batch=batch, seq=seq, chunks=SWA_CHUNKS)
            xt = odd_proj_residual(xt, parts, dd.reshape(t, -1), wo_c, wo_d, tm=OUT_PROJ_ROWS)
        xt = mlp_residual(xt, mlp_norm[layer].reshape(1, d), w_up[layer].astype(BF16), w_down[layer].astype(BF16), fg,
                          tm=MLP_ROWS, tf=MLP_FF_COLS, final_norm=(layer == depth - 1))
    return xt.reshape(batch, seq, d)
```
